```python
import jax
import jax.numpy as jnp
from jax import lax
import numpy as np

D_MODEL = 2048
BATCH = 1
SEQ = 8192
DEPTH = 2

GRID_W = 64
CTX_LEN = 256
HEAD_DIM = 128

NA_HEADS = 4
WIN_H = 8
WIN_W = 16
SG_GROUPS = 4
SG_CHUNK = 128
GQA_Q_HEADS = 8
GQA_KV_HEADS = 2
ROPE_THETA = 10000.0
ROPE_PAIRS = HEAD_DIM // 4
Q_BLOCK = 128

NA_WIDTH = NA_HEADS * HEAD_DIM
SG_WIDTH = SG_GROUPS * HEAD_DIM
GQA_WIDTH = GQA_Q_HEADS * HEAD_DIM
KV_WIDTH = GQA_KV_HEADS * HEAD_DIM
MIX_WIDTH = NA_WIDTH + SG_WIDTH + GQA_WIDTH
IN_WIDTH = 3 * NA_WIDTH + 2 * SG_WIDTH + GQA_WIDTH + 2 * KV_WIDTH
SPLITS = (NA_WIDTH, 2 * NA_WIDTH, 3 * NA_WIDTH, 3 * NA_WIDTH + 2 * SG_WIDTH,
          3 * NA_WIDTH + 2 * SG_WIDTH + GQA_WIDTH,
          3 * NA_WIDTH + 2 * SG_WIDTH + GQA_WIDTH + KV_WIDTH)

N_EXPERTS = 32
N_GROUPS = 4
EXPERTS_PER_GROUP = N_EXPERTS // N_GROUPS
TOP_K = 2
D_EXPERT = 1024
MOE_BLOCK = 128

ALPHA = (2 * DEPTH) ** 0.25
BETA = (8 * DEPTH) ** -0.25
LN_EPS = 1e-6

kernel_name = 'hybrid_dit_natten_sgmlp_gqa_moe'


def layer_norm(x, gain=None, bias=None):
    xf = x.astype(jnp.float32)
    mu = jnp.mean(xf, axis=-1, keepdims=True)
    var = jnp.mean(jnp.square(xf - mu), axis=-1, keepdims=True)
    y = (xf - mu) * lax.rsqrt(var + LN_EPS)
    if gain is not None:
        y = y * gain.astype(jnp.float32) + bias.astype(jnp.float32)
    return y.astype(x.dtype)


def rms_norm(x, gain):
    xf = x.astype(jnp.float32)
    y = xf * lax.rsqrt(jnp.mean(jnp.square(xf), axis=-1, keepdims=True) + LN_EPS)
    return (y * gain.astype(jnp.float32)).astype(x.dtype)


def modulate(x, shift, scale):
    return layer_norm(x) * (1 + scale) + shift


def axial_rope_tables(seq_len):
    t = jnp.arange(seq_len, dtype=jnp.int32)
    row = (t // GRID_W).astype(jnp.float32)
    col = (t % GRID_W).astype(jnp.float32)
    inv_freq = ROPE_THETA ** (-jnp.arange(ROPE_PAIRS, dtype=jnp.float32) / ROPE_PAIRS)
    ang = jnp.stack([row[:, None] * inv_freq, col[:, None] * inv_freq], axis=1)
    return jnp.cos(ang), jnp.sin(ang)


def axial_rope(x, cos, sin):
    b, s, h, _ = x.shape
    xf = x.astype(jnp.float32).reshape(b, s, h, 2, 2, ROPE_PAIRS)
    x1, x2 = xf[..., 0, :], xf[..., 1, :]
    cs, sn = cos[None, :, None], sin[None, :, None]
    out = jnp.stack([x1 * cs - x2 * sn, x2 * cs + x1 * sn], axis=-2)
    return out.reshape(b, s, h, HEAD_DIM).astype(x.dtype)


def attend(q, k, v):
    b, lq, hq, d = q.shape
    hkv = k.shape[2]
    qg = q.reshape(b, lq, hkv, hq // hkv, d)
    s = jnp.einsum('bqkgd,bskd->bkgqs', qg, k).astype(jnp.float32) * (d ** -0.5)
    p = jax.nn.softmax(s, axis=-1).astype(v.dtype)
    return jnp.einsum('bkgqs,bskd->bqkgd', p, v).reshape(b, lq, hq, d)


def gqa_latent(q, k, v, k_ctx, v_ctx):
    b, s, hq, d = q.shape
    keys = jnp.concatenate([k, k_ctx], axis=1)
    vals = jnp.concatenate([v, v_ctx], axis=1)
    q_blocks = q.reshape(b, s // Q_BLOCK, Q_BLOCK, hq, d).transpose(1, 0, 2, 3, 4)
    out = lax.map(lambda qb: attend(qb, keys, vals), q_blocks)
    return out.transpose(1, 0, 2, 3, 4).reshape(b, s, hq, d)


def neighbourhood_attention(q, k, v, k_ctx, v_ctx, rpb):
    b, s, h, d = q.shape
    rows = s // GRID_W
    kh = min(WIN_H, rows)
    n_loc = kh * WIN_W
    qg = q.reshape(b, rows, GRID_W, h, d)
    kg = k.reshape(b, rows, GRID_W, h, d)
    vg = v.reshape(b, rows, GRID_W, h, d)
    col = jnp.arange(GRID_W)
    col_idx = jnp.clip(col - WIN_W // 2, 0, GRID_W - WIN_W)[:, None] + jnp.arange(WIN_W)
    dcol = col_idx - col[:, None] + (WIN_W - 1)
    scale = d ** -0.5

    def one_row(r):
        row_start = jnp.clip(r - WIN_H // 2, 0, rows - kh)
        k_win = lax.dynamic_slice_in_dim(kg, row_start, kh, axis=1)[:, :, col_idx]
        v_win = lax.dynamic_slice_in_dim(vg, row_start, kh, axis=1)[:, :, col_idx]
        q_r = lax.dynamic_index_in_dim(qg, r, axis=1, keepdims=False)
        drow = row_start + jnp.arange(kh) - r + (WIN_H - 1)
        bias = rpb[:, drow[:, None, None], dcol[None]].transpose(0, 2, 1, 3)
        s_loc = jnp.einsum('bqhd,brqwhd->bhqrw', q_r, k_win).astype(jnp.float32) * scale + bias.astype(jnp.float32)
        s_ctx = jnp.einsum('bqhd,blhd->bhql', q_r, k_ctx).astype(jnp.float32) * scale
        scores = jnp.concatenate([s_loc.reshape(b, h, GRID_W, n_loc), s_ctx], axis=-1)
        p = jax.nn.softmax(scores, axis=-1).astype(v.dtype)
        o_loc = jnp.einsum('bhqrw,brqwhd->bqhd', p[..., :n_loc].reshape(b, h, GRID_W, kh, WIN_W), v_win)
        o_ctx = jnp.einsum('bhql,blhd->bqhd', p[..., n_loc:], v_ctx)
        return o_loc + o_ctx

    out = lax.map(one_row, jnp.arange(rows))
    return out.transpose(1, 0, 2, 3, 4).reshape(b, s, h, d)


def spatial_gating(z, ln_g, ln_b, w_s, b_s):
    b, s, _ = z.shape
    u, v = jnp.split(z, 2, axis=-1)
    v = v.reshape(b, s // SG_CHUNK, SG_CHUNK, SG_GROUPS, HEAD_DIM)
    v = layer_norm(v, ln_g, ln_b)
    v = jnp.einsum('gij,bcjgd->bcigd', w_s, v) + b_s.T[:, :, None]
    return u * v.reshape(b, s, SG_WIDTH)


def project(h, w_in):
    b, s, _ = h.shape
    na_q, na_k, na_v, sg_z, g_q, g_k, g_v = jnp.split(h @ w_in, SPLITS, axis=-1)
    heads = lambda t, n: t.reshape(b, s, n, HEAD_DIM)
    return (heads(na_q, NA_HEADS), heads(na_k, NA_HEADS), heads(na_v, NA_HEADS), jax.nn.gelu(sg_z),
            heads(g_q, GQA_Q_HEADS), heads(g_k, GQA_KV_HEADS), heads(g_v, GQA_KV_HEADS))


def mixing_sublayer(h_lat, h_ctx, cos, sin, w_in, rpb, sg_ln_g, sg_ln_b, sg_w, sg_b,
                    q_norm_g, k_norm_g, w_out, with_ctx_out):
    b, s, _ = h_lat.shape
    aq, ak, av, sz, gq, gk, gv = project(h_lat, w_in)
    caq, cak, cav, csz, cgq, cgk, cgv = project(h_ctx, w_in)
    gq = axial_rope(rms_norm(gq, q_norm_g), cos, sin)
    gk = axial_rope(rms_norm(gk, k_norm_g), cos, sin)
    cgk = rms_norm(cgk, k_norm_g)
    o_na = neighbourhood_attention(aq, ak, av, cak, cav, rpb)
    o_sg = spatial_gating(sz, sg_ln_g, sg_ln_b, sg_w, sg_b)
    o_gq = gqa_latent(gq, gk, gv, cgk, cgv)
    y_lat = jnp.concatenate([o_na.reshape(b, s, NA_WIDTH), o_sg, o_gq.reshape(b, s, GQA_WIDTH)], axis=-1) @ w_out
    if not with_ctx_out:
        return y_lat, None
    l = h_ctx.shape[1]
    c_na = attend(caq, cak, cav)
    c_sg = spatial_gating(csz, sg_ln_g, sg_ln_b, sg_w, sg_b)
    c_gq = attend(rms_norm(cgq, q_norm_g), cgk, cgv)
    y_ctx = jnp.concatenate([c_na.reshape(b, l, NA_WIDTH), c_sg, c_gq.reshape(b, l, GQA_WIDTH)], axis=-1) @ w_out
    return y_lat, y_ctx


def moe_ffn(h, router_w, router_bias, w_gate, w_up, w_down):
    n, d = h.shape
    scores = jax.nn.sigmoid((h @ router_w).astype(jnp.float32))
    sel = scores + router_bias.astype(jnp.float32)
    grp_score = lax.top_k(sel.reshape(n, N_GROUPS, EXPERTS_PER_GROUP), 2)[0].sum(-1)
    best = jnp.argmax(grp_score, axis=-1)
    in_grp = (jnp.arange(N_EXPERTS) // EXPERTS_PER_GROUP)[None, :] == best[:, None]
    _, top_idx = lax.top_k(jnp.where(in_grp, sel, -jnp.inf), TOP_K)
    top_s = jnp.take_along_axis(scores, top_idx, axis=-1)
    gates = (top_s / jnp.sum(top_s, axis=-1, keepdims=True)).astype(h.dtype)

    nk = n * TOP_K
    flat_e = top_idx.reshape(nk)
    flat_tok = jnp.repeat(jnp.arange(n, dtype=jnp.int32), TOP_K)
    order = jnp.argsort(flat_e)
    se = flat_e[order]
    counts = jnp.bincount(flat_e, length=N_EXPERTS)
    padded = (counts + MOE_BLOCK - 1) // MOE_BLOCK * MOE_BLOCK
    pad_end = jnp.cumsum(padded)
    pad_start = pad_end - padded
    start = jnp.cumsum(counts) - counts
    dest = pad_start[se] + jnp.arange(nk) - start[se]
    n_blocks = (nk + N_EXPERTS * (MOE_BLOCK - 1) + MOE_BLOCK - 1) // MOE_BLOCK
    cap = n_blocks * MOE_BLOCK
    slot_tok = jnp.full((cap,), n, jnp.int32).at[dest].set(flat_tok[order])
    slot_gate = jnp.zeros((cap,), h.dtype).at[dest].set(gates.reshape(nk)[order])
    block_e = jnp.minimum(jnp.searchsorted(pad_end, jnp.arange(n_blocks) * MOE_BLOCK, side='right'), N_EXPERTS - 1)
    h_pad = jnp.concatenate([h, jnp.zeros((1, d), h.dtype)], axis=0)

    def one_block(args):
        e, tok = args
        xb = h_pad[tok]
        return (jax.nn.silu(xb @ w_gate[e]) * (xb @ w_up[e])) @ w_down[e]

    y = lax.map(one_block, (block_e, slot_tok.reshape(n_blocks, MOE_BLOCK))).reshape(cap, d)
    out = jnp.zeros((n + 1, d), h.dtype).at[slot_tok].add(y * slot_gate[:, None])
    return out[:n]


def setup_inputs(seed: int = 0) -> dict:
    key = jax.random.key(seed)
    ks = jax.random.split(key, 24)
    nrm = lambda k, shape, std: jax.random.normal(k, shape, jnp.float32) * std
    L = DEPTH
    return {
        'x': nrm(ks[0], (BATCH, SEQ, D_MODEL), 1.0),
        'c': nrm(ks[1], (BATCH, D_MODEL), 1.0),
        'ctx': nrm(ks[2], (BATCH, CTX_LEN, D_MODEL), 1.0),
        'c_ctx': nrm(ks[3], (D_MODEL,), 1.0),
        'ada_w': nrm(ks[4], (L, D_MODEL, 6 * D_MODEL), 0.5 * D_MODEL ** -0.5),
        'ada_b': nrm(ks[5], (L, 6 * D_MODEL), 0.02),
        'w_in': nrm(ks[6], (L, D_MODEL, IN_WIDTH), D_MODEL ** -0.5),
        'na_rpb': nrm(ks[7], (L, NA_HEADS, 2 * WIN_H - 1, 2 * WIN_W - 1), 0.1),
        'sg_ln_g': 1.0 + nrm(ks[8], (L, SG_GROUPS, HEAD_DIM), 0.1),
        'sg_ln_b': nrm(ks[9], (L, SG_GROUPS, HEAD_DIM), 0.02),
        'sg_w': nrm(ks[10], (L, SG_GROUPS, SG_CHUNK, SG_CHUNK), SG_CHUNK ** -0.5),
        'sg_b': 1.0 + nrm(ks[11], (L, SG_GROUPS, SG_CHUNK), 0.1),
        'q_norm_g': 1.0 + nrm(ks[12], (L, HEAD_DIM), 0.1),
        'k_norm_g': 1.0 + nrm(ks[13], (L, HEAD_DIM), 0.1),
        'w_out': nrm(ks[14], (L, MIX_WIDTH, D_MODEL), BETA * MIX_WIDTH ** -0.5),
        'ln_mix_g': 1.0 + nrm(ks[15], (L, D_MODEL), 0.1),
        'ln_mix_b': nrm(ks[16], (L, D_MODEL), 0.02),
        'router_w': nrm(ks[17], (D_MODEL, N_EXPERTS), D_MODEL ** -0.5),
        'router_bias': nrm(ks[18], (N_EXPERTS,), 0.01),
        'moe_w_gate': nrm(ks[19], (L, N_EXPERTS, D_MODEL, D_EXPERT), D_MODEL ** -0.5),
        'moe_w_up': nrm(ks[20], (L, N_EXPERTS, D_MODEL, D_EXPERT), D_MODEL ** -0.5),
        'moe_w_down': nrm(ks[21], (L, N_EXPERTS, D_EXPERT, D_MODEL), BETA * D_EXPERT ** -0.5),
        'ln_ffn_g': 1.0 + nrm(ks[22], (L, D_MODEL), 0.1),
        'ln_ffn_b': nrm(ks[23], (L, D_MODEL), 0.02),
    }


def reference(x, c, ctx, c_ctx, ada_w, ada_b, w_in, na_rpb, sg_ln_g, sg_ln_b, sg_w, sg_b,
              q_norm_g, k_norm_g, w_out, ln_mix_g, ln_mix_b, router_w, router_bias,
              moe_w_gate, moe_w_up, moe_w_down, ln_ffn_g, ln_ffn_b):
    b, s, d = x.shape
    cos, sin = axial_rope_tables(s)
    xc = ctx
    for layer in range(DEPTH):
        last = layer == DEPTH - 1
        mod_lat = jnp.split(jax.nn.silu(c) @ ada_w[layer] + ada_b[layer], 6, axis=-1)
        sh1, sc1, g1, sh2, sc2, g2 = [m[:, None] for m in mod_lat]
        csh1, csc1, cg1, csh2, csc2, cg2 = jnp.split(jax.nn.silu(c_ctx) @ ada_w[layer] + ada_b[layer], 6, axis=-1)

        y_lat, y_ctx = mixing_sublayer(modulate(x, sh1, sc1), modulate(xc, csh1, csc1), cos, sin,
                                       w_in[layer], na_rpb[layer], sg_ln_g[layer], sg_ln_b[layer],
                                       sg_w[layer], sg_b[layer], q_norm_g[layer], k_norm_g[layer],
                                       w_out[layer], not last)
        x = layer_norm(ALPHA * x + g1 * y_lat, ln_mix_g[layer], ln_mix_b[layer])
        h_lat = modulate(x, sh2, sc2).reshape(b * s, d)
        if last:
            y = moe_ffn(h_lat, router_w, router_bias, moe_w_gate[layer], moe_w_up[layer], moe_w_down[layer])
            x = layer_norm(ALPHA * x + g2 * y.reshape(b, s, d), ln_ffn_g[layer], ln_ffn_b[layer])
        else:
            l = xc.shape[1]
            xc = layer_norm(ALPHA * xc + cg1 * y_ctx, ln_mix_g[layer], ln_mix_b[layer])
            h_ctx = modulate(xc, csh2, csc2).reshape(b * l, d)
            y = moe_ffn(jnp.concatenate([h_lat, h_ctx], axis=0), router_w, router_bias,
                        moe_w_gate[layer], moe_w_up[layer], moe_w_down[layer])
            x = layer_norm(ALPHA * x + g2 * y[:b * s].reshape(b, s, d), ln_ffn_g[layer], ln_ffn_b[layer])
            xc = layer_norm(ALPHA * xc + cg2 * y[b * s:].reshape(b, l, d), ln_ffn_g[layer], ln_ffn_b[layer])
    return x
```

```python
import functools

import jax
import jax.numpy as jnp
import numpy as np
from jax import lax
from jax.experimental import pallas as pl
from jax.experimental.pallas import tpu as pltpu

F32 = jnp.float32
BF16 = jnp.bfloat16
I32 = jnp.int32

D_MODEL = 2048
SEQ = 8192
DEPTH = 2
GRID_W = 64
ROWS = SEQ // GRID_W
CTX_LEN = 256
HEAD_DIM = 128

NA_HEADS = 4
WIN_H = 8
WIN_W = 16
SG_GROUPS = 4
SG_CHUNK = 128
GQA_Q_HEADS = 8
GQA_KV_HEADS = 2
ROPE_THETA = 10000.0
ROPE_PAIRS = HEAD_DIM // 4

NA_WIDTH = NA_HEADS * HEAD_DIM
SG_WIDTH = SG_GROUPS * HEAD_DIM
GQA_WIDTH = GQA_Q_HEADS * HEAD_DIM
KV_WIDTH = GQA_KV_HEADS * HEAD_DIM
MIX_WIDTH = NA_WIDTH + SG_WIDTH + GQA_WIDTH
IN_WIDTH = 3 * NA_WIDTH + 2 * SG_WIDTH + GQA_WIDTH + 2 * KV_WIDTH

N_EXPERTS = 32
N_GROUPS = 4
EXPERTS_PER_GROUP = N_EXPERTS // N_GROUPS
TOP_K = 2
D_EXPERT = 1024

ALPHA = (2 * DEPTH) ** 0.25
LN_EPS = 1e-6
ATTN_SCALE = HEAD_DIM ** -0.5
NEG_BIG = -1e30

COL_NA_Q = 0
COL_NA_K = NA_HEADS
COL_NA_V = 2 * NA_HEADS
COL_SG = 3 * NA_HEADS
COL_GQ = COL_SG + 2 * SG_GROUPS
COL_GK = COL_GQ + GQA_Q_HEADS
COL_GV = COL_GK + GQA_KV_HEADS

PROJ_TN = 512
PROJ_NJ = IN_WIDTH // PROJ_TN

NA_R = 4
NA_TQ = NA_R * GRID_W
NA_KR = NA_R + WIN_H - 1
NA_TK = NA_KR * GRID_W
NA_NB = ROWS // NA_R

MOE_TM = 512
MOE_TM_LOG2 = 9
MOE_TF = 512
MOE_NF = D_EXPERT // MOE_TF
POS_CHUNK = 256

ROW_TILE = 256


def _cparams(semantics, vmem_mib):
    return pltpu.CompilerParams(dimension_semantics=semantics,
                                vmem_limit_bytes=vmem_mib * 1024 * 1024)


def _layer_norm_rows(x):
    mu = jnp.mean(x, axis=-1, keepdims=True)
    xc = x - mu
    var = jnp.mean(xc * xc, axis=-1, keepdims=True)
    return xc * lax.rsqrt(var + LN_EPS)


def _dot(a, b):
    return jnp.dot(a, b, preferred_element_type=F32)


def _dot_nt(a, b):
    return lax.dot_general(a, b, (((1,), (1,)), ((), ())), preferred_element_type=F32)


ADA_TN = 1024


def _ada_kernel(c_ref, w_ref, b_ref, o_ref):
    cv = c_ref[...]
    act = (cv * jax.nn.sigmoid(cv)).astype(BF16)
    o_ref[...] = _dot(act, w_ref[...].astype(BF16)) + b_ref[...]


def _ada(cvec, ada_w, ada_b):
    n_out = 6 * D_MODEL
    return pl.pallas_call(
        _ada_kernel,
        grid=(DEPTH, n_out // ADA_TN),
        in_specs=[
            pl.BlockSpec((8, D_MODEL), lambda l, j: (0, 0)),
            pl.BlockSpec((None, D_MODEL, ADA_TN), lambda l, j: (l, 0, j)),
            pl.BlockSpec((None, 1, ADA_TN), lambda l, j: (l, 0, j)),
        ],
        out_specs=pl.BlockSpec((None, 8, ADA_TN), lambda l, j: (l, 0, j)),
        out_shape=jax.ShapeDtypeStruct((DEPTH, 8, n_out), F32),
        compiler_params=_cparams(("arbitrary", "arbitrary"), 40),
        name="ada_mod",
    )(cvec, ada_w, ada_b.reshape(DEPTH, 1, n_out))


def _rope_partner(y):
    lane = lax.broadcasted_iota(I32, y.shape, 1)
    first = (lane % (2 * ROPE_PAIRS)) < ROPE_PAIRS
    return jnp.where(first, pltpu.roll(y, HEAD_DIM - ROPE_PAIRS, 1), pltpu.roll(y, ROPE_PAIRS, 1))


def _norm_rope(zh, gain, cos, sin):
    ms = jnp.mean(zh * zh, axis=-1, keepdims=True)
    y = zh * lax.rsqrt(ms + LN_EPS) * gain
    return y * cos + _rope_partner(y) * sin


def _proj_in_kernel(x_ref, sh_ref, sc_ref, w_ref, cos_ref, sin_ref, qg_ref, kg_ref,
                    proj_ref, sgz_ref, h_scr):
    j = pl.program_id(1)

    @pl.when(j == 0)
    def _():
        h = _layer_norm_rows(x_ref[...]) * (1.0 + sc_ref[...]) + sh_ref[...]
        h_scr[...] = h.astype(BF16)

    z = _dot(h_scr[...], w_ref[...])

    @pl.when(j == 0)
    def _():
        proj_ref[...] = (z * ATTN_SCALE).astype(BF16)

    @pl.when((j == 1) | (j == 2))
    def _():
        proj_ref[...] = z.astype(BF16)

    @pl.when((j == 3) | (j == 4))
    def _():
        proj_ref[...] = z.astype(BF16)
        sgz_ref[...] = z

    @pl.when((j == 5) | (j == 6))
    def _():
        cos = cos_ref[...]
        sin = sin_ref[...]
        for hh in range(PROJ_TN // HEAD_DIM):
            cols = slice(hh * HEAD_DIM, (hh + 1) * HEAD_DIM)
            q = _norm_rope(z[:, cols], qg_ref[...], cos, sin)
            proj_ref[:, cols] = (q * ATTN_SCALE).astype(BF16)

    @pl.when(j == 7)
    def _():
        cos = cos_ref[...]
        sin = sin_ref[...]
        for hh in range(GQA_KV_HEADS):
            cols = slice(hh * HEAD_DIM, (hh + 1) * HEAD_DIM)
            proj_ref[:, cols] = _norm_rope(z[:, cols], kg_ref[...], cos, sin).astype(BF16)
        proj_ref[:, KV_WIDTH:] = z[:, KV_WIDTH:].astype(BF16)


def _proj_in(x, sh, sc, w_bf16, cos, sin, q_gain, k_gain, tm):
    n = x.shape[0]
    return pl.pallas_call(
        _proj_in_kernel,
        grid=(n // tm, PROJ_NJ),
        in_specs=[
            pl.BlockSpec((tm, D_MODEL), lambda i, j: (i, 0)),
            pl.BlockSpec((1, D_MODEL), lambda i, j: (0, 0)),
            pl.BlockSpec((1, D_MODEL), lambda i, j: (0, 0)),
            pl.BlockSpec((D_MODEL, PROJ_TN), lambda i, j: (0, j)),
            pl.BlockSpec((tm, HEAD_DIM), lambda i, j: (i, 0)),
            pl.BlockSpec((tm, HEAD_DIM), lambda i, j: (i, 0)),
            pl.BlockSpec((1, HEAD_DIM), lambda i, j: (0, 0)),
            pl.BlockSpec((1, HEAD_DIM), lambda i, j: (0, 0)),
        ],
        out_specs=[
            pl.BlockSpec((tm, PROJ_TN), lambda i, j: (i, j)),
            pl.BlockSpec((tm, PROJ_TN), lambda i, j: (i, jnp.clip(j - 3, 0, 1))),
        ],
        out_shape=[
            jax.ShapeDtypeStruct((n, IN_WIDTH), BF16),
            jax.ShapeDtypeStruct((n, 2 * SG_WIDTH), F32),
        ],
        scratch_shapes=[pltpu.VMEM((tm, D_MODEL), BF16)],
        compiler_params=_cparams(("arbitrary", "arbitrary"), 48),
        name="proj_in",
    )(x, sh, sc, w_bf16, cos, sin, q_gain, k_gain)


def _na_kernel(q_ref, k_ref, v_ref, kc_ref, vc_ref, bias_ref, o_ref):
    b = pl.program_id(1)
    key_row0 = jnp.clip(b * NA_R - WIN_H // 2, 0, ROWS - NA_KR)
    start = pl.multiple_of(key_row0 * GRID_W, GRID_W)
    q = q_ref[...]
    kw = k_ref[pl.ds(start, NA_TK), :]
    vw = v_ref[pl.ds(start, NA_TK), :]
    s_loc = _dot_nt(q, kw) + bias_ref[...]
    s_ctx = _dot_nt(q, kc_ref[...])
    m = jnp.maximum(jnp.max(s_loc, axis=-1, keepdims=True), jnp.max(s_ctx, axis=-1, keepdims=True))
    p_loc = jnp.exp(s_loc - m)
    p_ctx = jnp.exp(s_ctx - m)
    denom = jnp.sum(p_loc, axis=-1, keepdims=True) + jnp.sum(p_ctx, axis=-1, keepdims=True)
    o = _dot(p_loc.astype(BF16), vw) + _dot(p_ctx.astype(BF16), vc_ref[...])
    o_ref[...] = (o / denom).astype(BF16)


def _na_bias_tables(rpb):
    tables = []
    for r0, k0 in ((0, 0), (2 * NA_R, 2 * NA_R - WIN_H // 2), (ROWS - NA_R, ROWS - NA_KR)):
        r = r0 + np.arange(NA_R)
        rs = np.clip(r - WIN_H // 2, 0, ROWS - WIN_H)
        kr = k0 + np.arange(NA_KR)
        ok_r = (kr[None, :] >= rs[:, None]) & (kr[None, :] < rs[:, None] + WIN_H)
        drow = kr[None, :] - r[:, None] + (WIN_H - 1)
        c = np.arange(GRID_W)
        cs = np.clip(c - WIN_W // 2, 0, GRID_W - WIN_W)
        kc = np.arange(GRID_W)
        ok_c = (kc[None, :] >= cs[:, None]) & (kc[None, :] < cs[:, None] + WIN_W)
        dcol = kc[None, :] - c[:, None] + (WIN_W - 1)
        ok = ok_r[:, None, :, None] & ok_c[None, :, None, :]
        drow_i = np.broadcast_to(np.clip(drow, 0, 2 * WIN_H - 2)[:, None, :, None], ok.shape)
        dcol_i = np.broadcast_to(np.clip(dcol, 0, 2 * WIN_W - 2)[None, :, None, :], ok.shape)
        bias = rpb[:, drow_i.reshape(NA_TQ, NA_TK), dcol_i.reshape(NA_TQ, NA_TK)]
        tables.append(jnp.where(ok.reshape(NA_TQ, NA_TK)[None], bias.astype(F32), NEG_BIG))
    return jnp.stack(tables)


def _na_attention(proj, proj_ctx, bias_tab):
    def bias_idx(h, b):
        return (jnp.where(b == 0, 0, jnp.where(b == NA_NB - 1, 2, 1)), h, 0, 0)

    return pl.pallas_call(
        _na_kernel,
        grid=(NA_HEADS, NA_NB),
        in_specs=[
            pl.BlockSpec((NA_TQ, HEAD_DIM), lambda h, b: (b, COL_NA_Q + h)),
            pl.BlockSpec((SEQ, HEAD_DIM), lambda h, b: (0, COL_NA_K + h)),
            pl.BlockSpec((SEQ, HEAD_DIM), lambda h, b: (0, COL_NA_V + h)),
            pl.BlockSpec((CTX_LEN, HEAD_DIM), lambda h, b: (0, COL_NA_K + h)),
            pl.BlockSpec((CTX_LEN, HEAD_DIM), lambda h, b: (0, COL_NA_V + h)),
            pl.BlockSpec((None, None, NA_TQ, NA_TK), bias_idx),
        ],
        out_specs=pl.BlockSpec((NA_TQ, HEAD_DIM), lambda h, b: (b, h)),
        out_shape=jax.ShapeDtypeStruct((SEQ, NA_WIDTH), BF16),
        compiler_params=_cparams(("arbitrary", "arbitrary"), 32),
        name="na_attn",
    )(proj, proj, proj, proj_ctx, proj_ctx, bias_tab)


def _ctx_attn_kernel(q_ref, k_ref, v_ref, o_ref):
    s = _dot_nt(q_ref[...], k_ref[...])
    m = jnp.max(s, axis=-1, keepdims=True)
    p = jnp.exp(s - m)
    denom = jnp.sum(p, axis=-1, keepdims=True)
    o_ref[...] = (_dot(p.astype(BF16), v_ref[...]) / denom).astype(BF16)


def _ctx_attention(proj_ctx, n_heads, q_col, k_col, v_col, q_per_kv):
    return pl.pallas_call(
        _ctx_attn_kernel,
        grid=(n_heads,),
        in_specs=[
            pl.BlockSpec((CTX_LEN, HEAD_DIM), lambda h: (0, q_col + h)),
            pl.BlockSpec((CTX_LEN, HEAD_DIM), lambda h: (0, k_col + h // q_per_kv)),
            pl.BlockSpec((CTX_LEN, HEAD_DIM), lambda h: (0, v_col + h // q_per_kv)),
        ],
        out_specs=pl.BlockSpec((CTX_LEN, HEAD_DIM), lambda h: (0, h)),
        out_shape=jax.ShapeDtypeStruct((CTX_LEN, n_heads * HEAD_DIM), BF16),
        compiler_params=_cparams(("arbitrary",), 16),
        name="ctx_attn",
    )(proj_ctx, proj_ctx, proj_ctx)


SG_ROWS = 256


def _sg_kernel(zu_ref, zv_ref, g_ref, b_ref, w_ref, bs_ref, o_ref):
    for c in range(SG_ROWS // SG_CHUNK):
        rows = slice(c * SG_CHUNK, (c + 1) * SG_CHUNK)
        for g in range(SG_GROUPS):
            cols = slice(g * HEAD_DIM, (g + 1) * HEAD_DIM)
            v = jax.nn.gelu(zv_ref[rows, cols])
            vn = _layer_norm_rows(v) * g_ref[:, cols] + b_ref[:, cols]
            t = _dot(w_ref[g], vn.astype(BF16)) + bs_ref[:, g:g + 1]
            o_ref[rows, cols] = (jax.nn.gelu(zu_ref[rows, cols]) * t).astype(BF16)


def _spatial_gating(sgz, ln_g, ln_b, w_bf16, b_s):
    n = sgz.shape[0]
    return pl.pallas_call(
        _sg_kernel,
        grid=(n // SG_ROWS,),
        in_specs=[
            pl.BlockSpec((SG_ROWS, SG_WIDTH), lambda i: (i, 0)),
            pl.BlockSpec((SG_ROWS, SG_WIDTH), lambda i: (i, 1)),
            pl.BlockSpec((1, SG_WIDTH), lambda i: (0, 0)),
            pl.BlockSpec((1, SG_WIDTH), lambda i: (0, 0)),
            pl.BlockSpec((SG_GROUPS, SG_CHUNK, SG_CHUNK), lambda i: (0, 0, 0)),
            pl.BlockSpec((SG_CHUNK, SG_GROUPS), lambda i: (0, 0)),
        ],
        out_specs=pl.BlockSpec((SG_ROWS, SG_WIDTH), lambda i: (i, 0)),
        out_shape=jax.ShapeDtypeStruct((n, SG_WIDTH), BF16),
        compiler_params=_cparams(("arbitrary",), 16),
        name="spatial_gating",
    )(sgz, sgz, ln_g.reshape(1, SG_WIDTH), ln_b.reshape(1, SG_WIDTH), w_bf16, b_s.T)


GQA_TQ = 512
GQA_TK = 512


def _gqa_kernel(q_ref, k_ref, v_ref, kc_ref, vc_ref, o_ref):
    q = q_ref[...]
    s = _dot_nt(q, kc_ref[...])
    m0 = jnp.max(s, axis=-1, keepdims=True)
    p = jnp.exp(s - m0)
    l0 = jnp.sum(p, axis=-1, keepdims=True)
    acc0 = _dot(p.astype(BF16), vc_ref[...])

    def body(t, carry):
        m, l, acc = carry
        off = pl.multiple_of(t * GQA_TK, GQA_TK)
        s = _dot_nt(q, k_ref[pl.ds(off, GQA_TK), :])
        m_new = jnp.maximum(m, jnp.max(s, axis=-1, keepdims=True))
        alpha = jnp.exp(m - m_new)
        p = jnp.exp(s - m_new)
        l = alpha * l + jnp.sum(p, axis=-1, keepdims=True)
        acc = alpha * acc + _dot(p.astype(BF16), v_ref[pl.ds(off, GQA_TK), :])
        return m_new, l, acc

    _, l, acc = lax.fori_loop(0, SEQ // GQA_TK, body, (m0, l0, acc0))
    o_ref[...] = (acc / l).astype(BF16)


def _gqa_attention(proj, proj_ctx):
    grp = GQA_Q_HEADS // GQA_KV_HEADS
    return pl.pallas_call(
        _gqa_kernel,
        grid=(GQA_Q_HEADS, SEQ // GQA_TQ),
        in_specs=[
            pl.BlockSpec((GQA_TQ, HEAD_DIM), lambda h, i: (i, COL_GQ + h)),
            pl.BlockSpec((SEQ, HEAD_DIM), lambda h, i: (0, COL_GK + h // grp)),
            pl.BlockSpec((SEQ, HEAD_DIM), lambda h, i: (0, COL_GV + h // grp)),
            pl.BlockSpec((CTX_LEN, HEAD_DIM), lambda h, i: (0, COL_GK + h // grp)),
            pl.BlockSpec((CTX_LEN, HEAD_DIM), lambda h, i: (0, COL_GV + h // grp)),
        ],
        out_specs=pl.BlockSpec((GQA_TQ, HEAD_DIM), lambda h, i: (i, h)),
        out_shape=jax.ShapeDtypeStruct((SEQ, GQA_WIDTH), BF16),
        compiler_params=_cparams(("arbitrary", "arbitrary"), 32),
        name="gqa_attn",
    )(proj, proj, proj, proj_ctx, proj_ctx)


def _proj_out_kernel(ona_ref, osg_ref, ogq_ref, w_ref, x_ref, g1_ref, lng_ref, lnb_ref,
                     sh2_ref, sc2_ref, xmid_ref, h2_ref):
    y = (_dot(ona_ref[...], w_ref[0:NA_WIDTH, :])
         + _dot(osg_ref[...], w_ref[NA_WIDTH:NA_WIDTH + SG_WIDTH, :])
         + _dot(ogq_ref[...], w_ref[NA_WIDTH + SG_WIDTH:, :]))
    z = ALPHA * x_ref[...] + g1_ref[...] * y
    xm = _layer_norm_rows(z) * lng_ref[...] + lnb_ref[...]
    xmid_ref[...] = xm
    h2_ref[...] = _layer_norm_rows(xm) * (1.0 + sc2_ref[...]) + sh2_ref[...]


def _proj_out(o_na, o_sg, o_gq, w_bf16, x, g1, ln_g, ln_b, sh2, sc2):
    n = x.shape[0]
    tm = ROW_TILE
    vec = pl.BlockSpec((1, D_MODEL), lambda i: (0, 0))
    return pl.pallas_call(
        _proj_out_kernel,
        grid=(n // tm,),
        in_specs=[
            pl.BlockSpec((tm, NA_WIDTH), lambda i: (i, 0)),
            pl.BlockSpec((tm, SG_WIDTH), lambda i: (i, 0)),
            pl.BlockSpec((tm, GQA_WIDTH), lambda i: (i, 0)),
            pl.BlockSpec((MIX_WIDTH, D_MODEL), lambda i: (0, 0)),
            pl.BlockSpec((tm, D_MODEL), lambda i: (i, 0)),
            vec, vec, vec, vec, vec,
        ],
        out_specs=[pl.BlockSpec((tm, D_MODEL), lambda i: (i, 0)),
                   pl.BlockSpec((tm, D_MODEL), lambda i: (i, 0))],
        out_shape=[jax.ShapeDtypeStruct((n, D_MODEL), F32),
                   jax.ShapeDtypeStruct((n, D_MODEL), F32)],
        compiler_params=_cparams(("arbitrary",), 48),
        name="proj_out",
    )(o_na, o_sg, o_gq, w_bf16, x, g1, ln_g, ln_b, sh2, sc2)


def _router_kernel(h_ref, rwt_ref, rb_ref, e_ref, g_ref):
    tm = h_ref.shape[0]
    logits = _dot_nt(rwt_ref[...], h_ref[...].astype(BF16))
    scores = jax.nn.sigmoid(logits)
    sel = scores + rb_ref[...]
    sub = lax.broadcasted_iota(I32, (EXPERTS_PER_GROUP, tm), 0)
    best = None
    for g in range(N_GROUPS):
        rows = slice(g * EXPERTS_PER_GROUP, (g + 1) * EXPERTS_PER_GROUP)
        v = sel[rows, :]
        sc = scores[rows, :]
        m1 = jnp.max(v, axis=0, keepdims=True)
        i1 = jnp.min(jnp.where(v == m1, sub, EXPERTS_PER_GROUP), axis=0, keepdims=True)
        v2 = jnp.where(sub == i1, -jnp.inf, v)
        m2 = jnp.max(v2, axis=0, keepdims=True)
        i2 = jnp.min(jnp.where(v2 == m2, sub, EXPERTS_PER_GROUP), axis=0, keepdims=True)
        s1 = jnp.sum(jnp.where(sub == i1, sc, 0.0), axis=0, keepdims=True)
        s2 = jnp.sum(jnp.where(sub == i2, sc, 0.0), axis=0, keepdims=True)
        cand = (m1 + m2, i1 + g * EXPERTS_PER_GROUP, i2 + g * EXPERTS_PER_GROUP, s1, s2)
        if best is None:
            best = cand
        else:
            better = cand[0] > best[0]
            best = tuple(jnp.where(better, cn, bs) for cn, bs in zip(cand, best))
    _, e1, e2, s1, s2 = best
    tot = s1 + s2
    e_ref[...] = jnp.concatenate([e1, e2], axis=0)
    g_ref[...] = jnp.concatenate([s1 / tot, s2 / tot], axis=0)


def _router(h2, rw_t_bf16, rb_col):
    n = h2.shape[0]
    tm = ROW_TILE
    return pl.pallas_call(
        _router_kernel,
        grid=(n // tm,),
        in_specs=[
            pl.BlockSpec((tm, D_MODEL), lambda i: (i, 0)),
            pl.BlockSpec((N_EXPERTS, D_MODEL), lambda i: (0, 0)),
            pl.BlockSpec((N_EXPERTS, 1), lambda i: (0, 0)),
        ],
        out_specs=[pl.BlockSpec((TOP_K, tm), lambda i: (0, i)),
                   pl.BlockSpec((TOP_K, tm), lambda i: (0, i))],
        out_shape=[jax.ShapeDtypeStruct((TOP_K, n), I32),
                   jax.ShapeDtypeStruct((TOP_K, n), F32)],
        compiler_params=_cparams(("arbitrary",), 16),
        name="router",
    )(h2, rw_t_bf16, rb_col)


def _num_tiles(n_tok):
    return (n_tok * TOP_K + N_EXPERTS * (MOE_TM - 1) + MOE_TM - 1) // MOE_TM


def _positions_kernel(e_ref, dest_ref, meta_ref, rank_scr, *, n_tok):
    ch = POS_CHUNK
    n_chunks = n_tok // ch
    iota_e = lax.broadcasted_iota(I32, (N_EXPERTS, ch), 0)
    upper = (lax.broadcasted_iota(I32, (ch, ch), 0) <= lax.broadcasted_iota(I32, (ch, ch), 1)).astype(BF16)

    def one_hots(off):
        e0 = e_ref[0:1, pl.ds(off, ch)]
        e1 = e_ref[1:2, pl.ds(off, ch)]
        return (iota_e == e0).astype(F32), (iota_e == e1).astype(F32)

    def rank_body(c, carry):
        off = pl.multiple_of(c * ch, ch)
        oh0, oh1 = one_hots(off)
        both = oh0 + oh1
        incl = _dot(both.astype(BF16), upper)
        before = incl - both + carry
        rank_scr[0:1, pl.ds(off, ch)] = jnp.sum(oh0 * before, axis=0, keepdims=True)
        rank_scr[1:2, pl.ds(off, ch)] = jnp.sum(oh1 * (before + oh0), axis=0, keepdims=True)
        return carry + jnp.sum(both, axis=1, keepdims=True)

    counts = lax.fori_loop(0, n_chunks, rank_body, jnp.zeros((N_EXPERTS, 1), F32))
    tiles = (counts.astype(I32) + (MOE_TM - 1)) >> MOE_TM_LOG2
    tiles_b = jnp.broadcast_to(tiles.astype(F32), (N_EXPERTS, 128)).astype(BF16)
    lower = (lax.broadcasted_iota(I32, (N_EXPERTS, N_EXPERTS), 1)
             < lax.broadcasted_iota(I32, (N_EXPERTS, N_EXPERTS), 0)).astype(BF16)
    tile_start = _dot(lower, tiles_b)
    tile_end = tile_start + tiles.astype(F32)
    pad_start = tile_start[:, 0:1] * float(MOE_TM)

    def dest_body(c, carry):
        off = pl.multiple_of(c * ch, ch)
        oh0, oh1 = one_hots(off)
        d0 = rank_scr[0:1, pl.ds(off, ch)] + jnp.sum(oh0 * pad_start, axis=0, keepdims=True)
        d1 = rank_scr[1:2, pl.ds(off, ch)] + jnp.sum(oh1 * pad_start, axis=0, keepdims=True)
        dest_ref[0:1, pl.ds(off, ch)] = d0.astype(I32)
        dest_ref[1:2, pl.ds(off, ch)] = d1.astype(I32)
        return carry

    lax.fori_loop(0, n_chunks, dest_body, 0)

    tile_id = lax.broadcasted_iota(I32, (N_EXPERTS, 128), 1).astype(F32)
    tile_expert = jnp.sum((tile_end <= tile_id).astype(F32), axis=0, keepdims=True)
    tile_expert = jnp.minimum(tile_expert, float(N_EXPERTS - 1))
    n_used = jnp.sum(tiles.astype(F32), axis=0, keepdims=True)
    row = lax.broadcasted_iota(I32, (8, 128), 0)
    meta = jnp.where(row == 0, jnp.broadcast_to(tile_expert, (8, 128)),
                     jnp.where(row == 1, jnp.broadcast_to(n_used, (8, 128)), 0.0))
    meta_ref[...] = meta.astype(I32)


def _positions(e_all):
    n_tok = e_all.shape[1]
    return pl.pallas_call(
        functools.partial(_positions_kernel, n_tok=n_tok),
        out_shape=[jax.ShapeDtypeStruct((TOP_K, n_tok), I32),
                   jax.ShapeDtypeStruct((8, 128), I32)],
        scratch_shapes=[pltpu.VMEM((TOP_K, n_tok), F32)],
        compiler_params=pltpu.CompilerParams(vmem_limit_bytes=16 * 1024 * 1024),
        name="moe_positions",
    )(e_all)


def _dispatch_kernel(dest_ref, h_ref, xs_in_ref, xs_ref, sem, *, n_tok):
    del xs_in_ref
    tm = h_ref.shape[0]
    base = pl.program_id(0) * tm

    def row_copy(r, k):
        d = dest_ref[k * n_tok + base + r]
        return pltpu.make_async_copy(h_ref.at[pl.ds(r, 1), :], xs_ref.at[pl.ds(d, 1), :], sem)

    def start(r, carry):
        for k in range(TOP_K):
            row_copy(r, k).start()
        return carry

    def wait(r, carry):
        for k in range(TOP_K):
            row_copy(r, k).wait()
        return carry

    lax.fori_loop(0, tm, start, 0)
    lax.fori_loop(0, tm, wait, 0)


def _dispatch(dest_flat, h2, xs):
    n = h2.shape[0]
    tm = ROW_TILE
    return pl.pallas_call(
        functools.partial(_dispatch_kernel, n_tok=n),
        grid_spec=pltpu.PrefetchScalarGridSpec(
            num_scalar_prefetch=1,
            grid=(n // tm,),
            in_specs=[pl.BlockSpec((tm, D_MODEL), lambda i, d: (i, 0)),
                      pl.BlockSpec(memory_space=pl.ANY)],
            out_specs=pl.BlockSpec(memory_space=pl.ANY),
            scratch_shapes=[pltpu.SemaphoreType.DMA(())],
        ),
        out_shape=jax.ShapeDtypeStruct(xs.shape, xs.dtype),
        input_output_aliases={2: 0},
        compiler_params=_cparams(("arbitrary",), 16),
        name="moe_dispatch",
    )(dest_flat, h2, xs)


def _moe_kernel(meta_ref, x_ref, wg_ref, wu_ref, wd_ref, y_ref, xb_scr, acc_scr, *, n_tiles):
    i = pl.program_id(0)
    f = pl.program_id(1)
    used = i < meta_ref[n_tiles]

    @pl.when(used & (f == 0))
    def _():
        xb_scr[...] = x_ref[...].astype(BF16)

    @pl.when(used)
    def _():
        xb = xb_scr[...]
        gate = _dot(xb, wg_ref[...].astype(BF16))
        up = _dot(xb, wu_ref[...].astype(BF16))
        hidden = (gate * jax.nn.sigmoid(gate) * up).astype(BF16)
        part = _dot(hidden, wd_ref[...].astype(BF16))

        @pl.when(f == 0)
        def _():
            acc_scr[...] = part

        @pl.when(f != 0)
        def _():
            acc_scr[...] += part

    @pl.when(f == MOE_NF - 1)
    def _():
        @pl.when(used)
        def _():
            y_ref[...] = acc_scr[...]

        @pl.when(jnp.logical_not(used))
        def _():
            y_ref[...] = jnp.zeros_like(y_ref)


def _moe_ffn(meta_flat, xs, w_gate, w_up, w_down, layer, n_tiles):
    def last_used(i, m):
        return jnp.minimum(i, m[n_tiles] - 1)

    def f_eff(i, f, m):
        return jnp.where(i < m[n_tiles], f, MOE_NF - 1)

    return pl.pallas_call(
        functools.partial(_moe_kernel, n_tiles=n_tiles),
        grid_spec=pltpu.PrefetchScalarGridSpec(
            num_scalar_prefetch=1,
            grid=(n_tiles, MOE_NF),
            in_specs=[
                pl.BlockSpec((MOE_TM, D_MODEL), lambda i, f, m: (last_used(i, m), 0)),
                pl.BlockSpec((None, None, D_MODEL, MOE_TF),
                             lambda i, f, m: (layer, m[last_used(i, m)], 0, f_eff(i, f, m))),
                pl.BlockSpec((None, None, D_MODEL, MOE_TF),
                             lambda i, f, m: (layer, m[last_used(i, m)], 0, f_eff(i, f, m))),
                pl.BlockSpec((None, None, MOE_TF, D_MODEL),
                             lambda i, f, m: (layer, m[last_used(i, m)], f_eff(i, f, m), 0)),
            ],
            out_specs=pl.BlockSpec((MOE_TM, D_MODEL), lambda i, f, m: (i, 0)),
            scratch_shapes=[pltpu.VMEM((MOE_TM, D_MODEL), BF16),
                            pltpu.VMEM((MOE_TM, D_MODEL), F32)],
        ),
        out_shape=jax.ShapeDtypeStruct((n_tiles * MOE_TM, D_MODEL), F32),
        compiler_params=_cparams(("arbitrary", "arbitrary"), 56),
        name="moe_ffn",
    )(meta_flat, xs, w_gate, w_up, w_down)


def _combine_kernel(dest_ref, y_ref, gt_ref, x_ref, g2_ref, lng_ref, lnb_ref, o_ref,
                    buf_ref, sem, *, n_tok):
    tm = x_ref.shape[0]
    base = pl.program_id(0) * tm

    def row_copy(r, k):
        d = dest_ref[k * n_tok + base + r]
        return pltpu.make_async_copy(y_ref.at[pl.ds(d, 1), :], buf_ref.at[k, pl.ds(r, 1), :], sem)

    def start(r, carry):
        for k in range(TOP_K):
            row_copy(r, k).start()
        return carry

    def wait(r, carry):
        for k in range(TOP_K):
            row_copy(r, k).wait()
        return carry

    lax.fori_loop(0, tm, start, 0)
    lax.fori_loop(0, tm, wait, 0)
    gt = gt_ref[...]
    moe = buf_ref[0] * gt[:, 0:1] + buf_ref[1] * gt[:, 1:2]
    z = ALPHA * x_ref[...] + g2_ref[...] * moe
    o_ref[...] = _layer_norm_rows(z) * lng_ref[...] + lnb_ref[...]


def _combine(dest_flat, y, gates_t, x_mid, g2, ln_g, ln_b):
    n = x_mid.shape[0]
    tm = ROW_TILE
    vec = pl.BlockSpec((1, D_MODEL), lambda i, d: (0, 0))
    return pl.pallas_call(
        functools.partial(_combine_kernel, n_tok=n),
        grid_spec=pltpu.PrefetchScalarGridSpec(
            num_scalar_prefetch=1,
            grid=(n // tm,),
            in_specs=[pl.BlockSpec(memory_space=pl.ANY),
                      pl.BlockSpec((tm, TOP_K), lambda i, d: (i, 0)),
                      pl.BlockSpec((tm, D_MODEL), lambda i, d: (i, 0)),
                      vec, vec, vec],
            out_specs=pl.BlockSpec((tm, D_MODEL), lambda i, d: (i, 0)),
            scratch_shapes=[pltpu.VMEM((TOP_K, tm, D_MODEL), F32),
                            pltpu.SemaphoreType.DMA(())],
        ),
        out_shape=jax.ShapeDtypeStruct((n, D_MODEL), F32),
        compiler_params=_cparams(("arbitrary",), 32),
        name="moe_combine",
    )(dest_flat, y, gates_t, x_mid, g2, ln_g, ln_b)


def _rope_tables():
    t = jnp.arange(SEQ, dtype=jnp.int32)
    row = (t // GRID_W).astype(F32)
    col = (t % GRID_W).astype(F32)
    inv_freq = ROPE_THETA ** (-jnp.arange(ROPE_PAIRS, dtype=F32) / ROPE_PAIRS)
    ang_r = row[:, None] * inv_freq
    ang_c = col[:, None] * inv_freq
    cos = jnp.concatenate([jnp.cos(ang_r), jnp.cos(ang_r), jnp.cos(ang_c), jnp.cos(ang_c)], axis=-1)
    sin = jnp.concatenate([-jnp.sin(ang_r), jnp.sin(ang_r), -jnp.sin(ang_c), jnp.sin(ang_c)], axis=-1)
    return cos, sin


def kernel(x, c, ctx, c_ctx, ada_w, ada_b, w_in, na_rpb, sg_ln_g, sg_ln_b, sg_w, sg_b, q_norm_g, k_norm_g, w_out, ln_mix_g, ln_mix_b, router_w, router_bias, moe_w_gate, moe_w_up, moe_w_down, ln_ffn_g, ln_ffn_b):
    cos, sin = _rope_tables()
    cos_ctx = jnp.ones((CTX_LEN, HEAD_DIM), F32)
    sin_ctx = jnp.zeros((CTX_LEN, HEAD_DIM), F32)

    cvec = jnp.zeros((8, D_MODEL), F32).at[0].set(c[0]).at[1].set(c_ctx)
    mods = _ada(cvec, ada_w, ada_b)

    rw_t = router_w.T.astype(BF16)
    rb_col = router_bias.reshape(N_EXPERTS, 1).astype(F32)

    xl = x[0]
    xc = ctx[0]
    for layer in range(DEPTH):
        last = layer == DEPTH - 1
        sh1, sc1, g1, sh2, sc2, g2 = jnp.split(mods[layer, 0:1], 6, axis=-1)
        csh1, csc1, cg1, csh2, csc2, cg2 = jnp.split(mods[layer, 1:2], 6, axis=-1)
        w_in_b = w_in[layer].astype(BF16)
        w_out_b = w_out[layer].astype(BF16)
        sg_w_b = sg_w[layer].astype(BF16)
        q_gain = q_norm_g[layer].reshape(1, HEAD_DIM)
        k_gain = k_norm_g[layer].reshape(1, HEAD_DIM)
        lng = ln_mix_g[layer].reshape(1, D_MODEL)
        lnb = ln_mix_b[layer].reshape(1, D_MODEL)
        fng = ln_ffn_g[layer].reshape(1, D_MODEL)
        fnb = ln_ffn_b[layer].reshape(1, D_MODEL)

        proj, sgz = _proj_in(xl, sh1, sc1, w_in_b, cos, sin, q_gain, k_gain, tm=1024)
        proj_c, sgz_c = _proj_in(xc, csh1, csc1, w_in_b, cos_ctx, sin_ctx, q_gain, k_gain, tm=CTX_LEN)

        o_na = _na_attention(proj, proj_c, _na_bias_tables(na_rpb[layer]))
        o_sg = _spatial_gating(sgz, sg_ln_g[layer], sg_ln_b[layer], sg_w_b, sg_b[layer])
        o_gq = _gqa_attention(proj, proj_c)
        x_mid, h2 = _proj_out(o_na, o_sg, o_gq, w_out_b, xl, g1, lng, lnb, sh2, sc2)

        if last:
            n_tok = SEQ
            e_all, gates = _router(h2, rw_t, rb_col)
        else:
            n_tok = SEQ + CTX_LEN
            c_na = _ctx_attention(proj_c, NA_HEADS, COL_NA_Q, COL_NA_K, COL_NA_V, 1)
            c_sg = _spatial_gating(sgz_c, sg_ln_g[layer], sg_ln_b[layer], sg_w_b, sg_b[layer])
            c_gq = _ctx_attention(proj_c, GQA_Q_HEADS, COL_GQ, COL_GK, COL_GV, GQA_Q_HEADS // GQA_KV_HEADS)
            xc_mid, h2_c = _proj_out(c_na, c_sg, c_gq, w_out_b, xc, cg1, lng, lnb, csh2, csc2)
            e_lat, gates_lat = _router(h2, rw_t, rb_col)
            e_ctx, gates_ctx = _router(h2_c, rw_t, rb_col)
            e_all = jnp.concatenate([e_lat, e_ctx], axis=1)
            gates = jnp.concatenate([gates_lat, gates_ctx], axis=1)

        n_tiles = _num_tiles(n_tok)
        dest, meta = _positions(e_all)
        meta_flat = jnp.concatenate([meta[0, :n_tiles], meta[1, :1]])
        xs = jnp.zeros((n_tiles * MOE_TM, D_MODEL), F32)
        xs = _dispatch(dest[:, :SEQ].reshape(-1), h2, xs)
        if not last:
            xs = _dispatch(dest[:, SEQ:].reshape(-1), h2_c, xs)
        y = _moe_ffn(meta_flat, xs, moe_w_gate, moe_w_up, moe_w_down, layer, n_tiles)
        gates_t = gates.T
        xl = _combine(dest[:, :SEQ].reshape(-1), y, gates_t[:SEQ], x_mid, g2, fng, fnb)
        if not last:
            xc = _combine(dest[:, SEQ:].reshape(-1), y, gates_t[SEQ:], xc_mid, cg2, fng, fnb)
    return xl[None]
```

```python
import functools

import jax
import jax.numpy as jnp
import numpy as np
from jax import lax
from jax.experimental import pallas as pl
from jax.experimental.pallas import tpu as pltpu

F32 = jnp.float32
BF16 = jnp.bfloat16
I32 = jnp.int32

D_MODEL = 2048
SEQ = 8192
DEPTH = 2
GRID_W = 64
ROWS = SEQ // GRID_W
CTX_LEN = 256
HEAD_DIM = 128

NA_HEADS = 4
WIN_H = 8
WIN_W = 16
SG_GROUPS = 4
SG_CHUNK = 128
GQA_Q_HEADS = 8
GQA_KV_HEADS = 2
ROPE_THETA = 10000.0
ROPE_PAIRS = HEAD_DIM // 4

NA_WIDTH = NA_HEADS * HEAD_DIM
SG_WIDTH = SG_GROUPS * HEAD_DIM
GQA_WIDTH = GQA_Q_HEADS * HEAD_DIM
KV_WIDTH = GQA_KV_HEADS * HEAD_DIM
MIX_WIDTH = NA_WIDTH + SG_WIDTH + GQA_WIDTH
IN_WIDTH = 3 * NA_WIDTH + 2 * SG_WIDTH + GQA_WIDTH + 2 * KV_WIDTH

N_EXPERTS = 32
N_GROUPS = 4
EXPERTS_PER_GROUP = N_EXPERTS // N_GROUPS
TOP_K = 2
D_EXPERT = 1024

ALPHA = (2 * DEPTH) ** 0.25
LN_EPS = 1e-6
ATTN_SCALE = HEAD_DIM ** -0.5
LOG2E = 1.4426950408889634
NEG_BIG = -1e30
KEYS = SEQ + CTX_LEN

COL_NA_Q = 0
COL_NA_K = NA_HEADS
COL_NA_V = 2 * NA_HEADS
COL_SG = 3 * NA_HEADS
COL_GQ = COL_SG + 2 * SG_GROUPS
COL_GK = COL_GQ + GQA_Q_HEADS
COL_GV = COL_GK + GQA_KV_HEADS

PROJ_TN = 512
PROJ_NJ = IN_WIDTH // PROJ_TN

NA_R = 4
NA_TQ = NA_R * GRID_W
NA_KR = NA_R + WIN_H - 1
NA_TK = NA_KR * GRID_W
NA_NB = ROWS // NA_R

MOE_TM = 512
MOE_TM_LOG2 = 9
MOE_TF = 512
MOE_NF = D_EXPERT // MOE_TF
POS_CHUNK = 256

ROW_TILE = 256


def _cparams(semantics, vmem_mib):
    return pltpu.CompilerParams(dimension_semantics=semantics,
                                vmem_limit_bytes=vmem_mib * 1024 * 1024)


def _layer_norm_rows(x):
    mu = jnp.mean(x, axis=-1, keepdims=True)
    xc = x - mu
    var = jnp.mean(xc * xc, axis=-1, keepdims=True)
    return xc * lax.rsqrt(var + LN_EPS)


def _dot(a, b):
    return jnp.dot(a, b, preferred_element_type=F32)


def _dot_nt(a, b):
    return lax.dot_general(a, b, (((1,), (1,)), ((), ())), preferred_element_type=F32)


ADA_TN = 1024


def _ada_kernel(c_ref, w_ref, b_ref, o_ref):
    cv = c_ref[...]
    act = (cv * jax.nn.sigmoid(cv)).astype(BF16)
    o_ref[...] = _dot(act, w_ref[...].astype(BF16)) + b_ref[...]


def _ada(cvec, ada_w, ada_b):
    n_out = 6 * D_MODEL
    return pl.pallas_call(
        _ada_kernel,
        grid=(DEPTH, n_out // ADA_TN),
        in_specs=[
            pl.BlockSpec((8, D_MODEL), lambda l, j: (0, 0)),
            pl.BlockSpec((None, D_MODEL, ADA_TN), lambda l, j: (l, 0, j)),
            pl.BlockSpec((None, 1, ADA_TN), lambda l, j: (l, 0, j)),
        ],
        out_specs=pl.BlockSpec((None, 8, ADA_TN), lambda l, j: (l, 0, j)),
        out_shape=jax.ShapeDtypeStruct((DEPTH, 8, n_out), F32),
        compiler_params=_cparams(("arbitrary", "arbitrary"), 40),
        name="ada_mod",
    )(cvec, ada_w, ada_b.reshape(DEPTH, 1, n_out))


def _rope_partner(y):
    lane = lax.broadcasted_iota(I32, y.shape, 1)
    first = (lane % (2 * ROPE_PAIRS)) < ROPE_PAIRS
    return jnp.where(first, pltpu.roll(y, HEAD_DIM - ROPE_PAIRS, 1), pltpu.roll(y, ROPE_PAIRS, 1))


def _norm_rope(zh, gain, cos, sin):
    ms = jnp.mean(zh * zh, axis=-1, keepdims=True)
    y = zh * lax.rsqrt(ms + LN_EPS) * gain
    return y * cos + _rope_partner(y) * sin


def _proj_in_kernel(x_ref, sh_ref, sc_ref, w_ref, cos_ref, sin_ref, qg_ref, kg_ref,
                    proj_ref, sgz_ref, h_scr):
    j = pl.program_id(1)

    @pl.when(j == 0)
    def _():
        h = _layer_norm_rows(x_ref[...]) * (1.0 + sc_ref[...]) + sh_ref[...]
        h_scr[...] = h.astype(BF16)

    z = _dot(h_scr[...], w_ref[...])

    @pl.when(j == 0)
    def _():
        proj_ref[...] = (z * ATTN_SCALE).astype(BF16)

    @pl.when((j == 1) | (j == 2))
    def _():
        proj_ref[...] = z.astype(BF16)

    @pl.when((j == 3) | (j == 4))
    def _():
        proj_ref[...] = z.astype(BF16)
        sgz_ref[...] = z

    @pl.when((j == 5) | (j == 6))
    def _():
        cos = cos_ref[...]
        sin = sin_ref[...]
        for hh in range(PROJ_TN // HEAD_DIM):
            cols = slice(hh * HEAD_DIM, (hh + 1) * HEAD_DIM)
            q = _norm_rope(z[:, cols], qg_ref[...], cos, sin)
            proj_ref[:, cols] = (q * (ATTN_SCALE * LOG2E)).astype(BF16)

    @pl.when(j == 7)
    def _():
        cos = cos_ref[...]
        sin = sin_ref[...]
        for hh in range(GQA_KV_HEADS):
            cols = slice(hh * HEAD_DIM, (hh + 1) * HEAD_DIM)
            proj_ref[:, cols] = _norm_rope(z[:, cols], kg_ref[...], cos, sin).astype(BF16)
        proj_ref[:, KV_WIDTH:] = z[:, KV_WIDTH:].astype(BF16)


def _proj_in(x, sh, sc, w_bf16, cos, sin, q_gain, k_gain, tm):
    n = x.shape[0]
    return pl.pallas_call(
        _proj_in_kernel,
        grid=(n // tm, PROJ_NJ),
        in_specs=[
            pl.BlockSpec((tm, D_MODEL), lambda i, j: (i, 0)),
            pl.BlockSpec((1, D_MODEL), lambda i, j: (0, 0)),
            pl.BlockSpec((1, D_MODEL), lambda i, j: (0, 0)),
            pl.BlockSpec((D_MODEL, PROJ_TN), lambda i, j: (0, j)),
            pl.BlockSpec((tm, HEAD_DIM), lambda i, j: (i, 0)),
            pl.BlockSpec((tm, HEAD_DIM), lambda i, j: (i, 0)),
            pl.BlockSpec((1, HEAD_DIM), lambda i, j: (0, 0)),
            pl.BlockSpec((1, HEAD_DIM), lambda i, j: (0, 0)),
        ],
        out_specs=[
            pl.BlockSpec((tm, PROJ_TN), lambda i, j: (i, j)),
            pl.BlockSpec((tm, PROJ_TN), lambda i, j: (i, jnp.clip(j - 3, 0, 1))),
        ],
        out_shape=[
            jax.ShapeDtypeStruct((n, IN_WIDTH), BF16),
            jax.ShapeDtypeStruct((n, 2 * SG_WIDTH), F32),
        ],
        scratch_shapes=[pltpu.VMEM((tm, D_MODEL), BF16)],
        compiler_params=_cparams(("arbitrary", "arbitrary"), 48),
        name="proj_in",
    )(x, sh, sc, w_bf16, cos, sin, q_gain, k_gain)


def _na_kernel(q_ref, k_ref, v_ref, kc_ref, vc_ref, bias_ref, o_ref):
    b = pl.program_id(1)
    key_row0 = jnp.clip(b * NA_R - WIN_H // 2, 0, ROWS - NA_KR)
    start = pl.multiple_of(key_row0 * GRID_W, GRID_W)
    q = q_ref[...]
    kw = k_ref[pl.ds(start, NA_TK), :]
    vw = v_ref[pl.ds(start, NA_TK), :]
    s_loc = _dot_nt(q, kw) + bias_ref[...]
    s_ctx = _dot_nt(q, kc_ref[...])
    m = jnp.maximum(jnp.max(s_loc, axis=-1, keepdims=True), jnp.max(s_ctx, axis=-1, keepdims=True))
    p_loc = jnp.exp(s_loc - m)
    p_ctx = jnp.exp(s_ctx - m)
    denom = jnp.sum(p_loc, axis=-1, keepdims=True) + jnp.sum(p_ctx, axis=-1, keepdims=True)
    o = _dot(p_loc.astype(BF16), vw) + _dot(p_ctx.astype(BF16), vc_ref[...])
    o_ref[...] = (o / denom).astype(BF16)


def _na_bias_tables(rpb):
    tables = []
    for r0, k0 in ((0, 0), (2 * NA_R, 2 * NA_R - WIN_H // 2), (ROWS - NA_R, ROWS - NA_KR)):
        r = r0 + np.arange(NA_R)
        rs = np.clip(r - WIN_H // 2, 0, ROWS - WIN_H)
        kr = k0 + np.arange(NA_KR)
        ok_r = (kr[None, :] >= rs[:, None]) & (kr[None, :] < rs[:, None] + WIN_H)
        drow = kr[None, :] - r[:, None] + (WIN_H - 1)
        c = np.arange(GRID_W)
        cs = np.clip(c - WIN_W // 2, 0, GRID_W - WIN_W)
        kc = np.arange(GRID_W)
        ok_c = (kc[None, :] >= cs[:, None]) & (kc[None, :] < cs[:, None] + WIN_W)
        dcol = kc[None, :] - c[:, None] + (WIN_W - 1)
        ok = ok_r[:, None, :, None] & ok_c[None, :, None, :]
        pick_r = (np.clip(drow, 0, 2 * WIN_H - 2)[..., None] == np.arange(2 * WIN_H - 1)).astype(np.float32)
        pick_c = (np.clip(dcol, 0, 2 * WIN_W - 2)[..., None] == np.arange(2 * WIN_W - 1)).astype(np.float32)
        bias = jnp.einsum('rka,hab,cwb->hrckw', pick_r, rpb.astype(F32), pick_c,
                          precision=lax.Precision.HIGHEST)
        tables.append(jnp.where(ok[None], bias, NEG_BIG).reshape(NA_HEADS, NA_TQ, NA_TK))
    return jnp.stack(tables)


def _na_attention(proj, proj_ctx, bias_tab):
    def bias_idx(h, b):
        return (jnp.where(b == 0, 0, jnp.where(b == NA_NB - 1, 2, 1)), h, 0, 0)

    return pl.pallas_call(
        _na_kernel,
        grid=(NA_HEADS, NA_NB),
        in_specs=[
            pl.BlockSpec((NA_TQ, HEAD_DIM), lambda h, b: (b, COL_NA_Q + h)),
            pl.BlockSpec((SEQ, HEAD_DIM), lambda h, b: (0, COL_NA_K + h)),
            pl.BlockSpec((SEQ, HEAD_DIM), lambda h, b: (0, COL_NA_V + h)),
            pl.BlockSpec((CTX_LEN, HEAD_DIM), lambda h, b: (0, COL_NA_K + h)),
            pl.BlockSpec((CTX_LEN, HEAD_DIM), lambda h, b: (0, COL_NA_V + h)),
            pl.BlockSpec((None, None, NA_TQ, NA_TK), bias_idx),
        ],
        out_specs=pl.BlockSpec((NA_TQ, HEAD_DIM), lambda h, b: (b, h)),
        out_shape=jax.ShapeDtypeStruct((SEQ, NA_WIDTH), BF16),
        compiler_params=_cparams(("arbitrary", "arbitrary"), 32),
        name="na_attn",
    )(proj, proj, proj, proj_ctx, proj_ctx, bias_tab)


def _ctx_attn_kernel(q_ref, k_ref, v_ref, o_ref, *, log2_domain):
    s = _dot_nt(q_ref[...], k_ref[...])
    m = jnp.max(s, axis=-1, keepdims=True)
    p = jnp.exp2(s - m) if log2_domain else jnp.exp(s - m)
    denom = jnp.sum(p, axis=-1, keepdims=True)
    o_ref[...] = (_dot(p.astype(BF16), v_ref[...]) / denom).astype(BF16)


def _ctx_attention(proj_ctx, n_heads, q_col, k_col, v_col, q_per_kv, log2_domain):
    return pl.pallas_call(
        functools.partial(_ctx_attn_kernel, log2_domain=log2_domain),
        grid=(n_heads,),
        in_specs=[
            pl.BlockSpec((CTX_LEN, HEAD_DIM), lambda h: (0, q_col + h)),
            pl.BlockSpec((CTX_LEN, HEAD_DIM), lambda h: (0, k_col + h // q_per_kv)),
            pl.BlockSpec((CTX_LEN, HEAD_DIM), lambda h: (0, v_col + h // q_per_kv)),
        ],
        out_specs=pl.BlockSpec((CTX_LEN, HEAD_DIM), lambda h: (0, h)),
        out_shape=jax.ShapeDtypeStruct((CTX_LEN, n_heads * HEAD_DIM), BF16),
        compiler_params=_cparams(("arbitrary",), 16),
        name="ctx_attn",
    )(proj_ctx, proj_ctx, proj_ctx)


SG_ROWS = 256


def _sg_kernel(zu_ref, zv_ref, g_ref, b_ref, w_ref, bs_ref, o_ref):
    for c in range(SG_ROWS // SG_CHUNK):
        rows = slice(c * SG_CHUNK, (c + 1) * SG_CHUNK)
        for g in range(SG_GROUPS):
            cols = slice(g * HEAD_DIM, (g + 1) * HEAD_DIM)
            v = jax.nn.gelu(zv_ref[rows, cols])
            vn = _layer_norm_rows(v) * g_ref[:, cols] + b_ref[:, cols]
            t = _dot(w_ref[g], vn.astype(BF16)) + bs_ref[:, g:g + 1]
            o_ref[rows, cols] = (jax.nn.gelu(zu_ref[rows, cols]) * t).astype(BF16)


def _spatial_gating(sgz, ln_g, ln_b, w_bf16, b_s):
    n = sgz.shape[0]
    return pl.pallas_call(
        _sg_kernel,
        grid=(n // SG_ROWS,),
        in_specs=[
            pl.BlockSpec((SG_ROWS, SG_WIDTH), lambda i: (i, 0)),
            pl.BlockSpec((SG_ROWS, SG_WIDTH), lambda i: (i, 1)),
            pl.BlockSpec((1, SG_WIDTH), lambda i: (0, 0)),
            pl.BlockSpec((1, SG_WIDTH), lambda i: (0, 0)),
            pl.BlockSpec((SG_GROUPS, SG_CHUNK, SG_CHUNK), lambda i: (0, 0, 0)),
            pl.BlockSpec((SG_CHUNK, SG_GROUPS), lambda i: (0, 0)),
        ],
        out_specs=pl.BlockSpec((SG_ROWS, SG_WIDTH), lambda i: (i, 0)),
        out_shape=jax.ShapeDtypeStruct((n, SG_WIDTH), BF16),
        compiler_params=_cparams(("arbitrary",), 16),
        name="spatial_gating",
    )(sgz, sgz, ln_g.reshape(1, SG_WIDTH), ln_b.reshape(1, SG_WIDTH), w_bf16, b_s.T)


GQA_TQ = 512
GQA_TK = 512


def _gqa_kernel(qt_ref, k_ref, vt_ref, ot_ref, sa_ref, sb_ref):
    qt = qt_ref[...]
    n_chunks = SEQ // GQA_TK

    def scores(t, dst_ref):
        off = pl.multiple_of(t * GQA_TK, GQA_TK)
        s = _dot(k_ref[pl.ds(off, GQA_TK), :], qt)
        dst_ref[...] = s
        return jnp.max(s, axis=0, keepdims=True)

    def consume(t, src_ref, m_chunk, m, l, acc):
        off = pl.multiple_of(t * GQA_TK, GQA_TK)
        m_new = jnp.maximum(m, m_chunk)
        alpha = jnp.exp2(m - m_new)
        p = jnp.exp2(src_ref[...] - m_new)
        l = alpha * l + jnp.sum(p, axis=0, keepdims=True)
        acc = alpha * acc + _dot(vt_ref[:, pl.ds(off, GQA_TK)], p.astype(BF16))
        return m_new, l, acc

    s = _dot(k_ref[pl.ds(SEQ, CTX_LEN), :], qt)
    m = jnp.max(s, axis=0, keepdims=True)
    p = jnp.exp2(s - m)
    l = jnp.sum(p, axis=0, keepdims=True)
    acc = _dot(vt_ref[:, pl.ds(SEQ, CTX_LEN)], p.astype(BF16))
    mc_a = scores(0, sa_ref)

    def body(u, carry):
        m, l, acc, mc_a = carry
        mc_b = scores(2 * u + 1, sb_ref)
        m, l, acc = consume(2 * u, sa_ref, mc_a, m, l, acc)
        mc_a = scores(2 * u + 2, sa_ref)
        m, l, acc = consume(2 * u + 1, sb_ref, mc_b, m, l, acc)
        return m, l, acc, mc_a

    m, l, acc, mc_a = lax.fori_loop(0, n_chunks // 2 - 1, body, (m, l, acc, mc_a))
    mc_b = scores(n_chunks - 1, sb_ref)
    m, l, acc = consume(n_chunks - 2, sa_ref, mc_a, m, l, acc)
    m, l, acc = consume(n_chunks - 1, sb_ref, mc_b, m, l, acc)
    ot_ref[...] = (acc / l).astype(BF16)


def _gqa_attention(q_t, k_all, v_t):
    grp = GQA_Q_HEADS // GQA_KV_HEADS
    return pl.pallas_call(
        _gqa_kernel,
        grid=(GQA_Q_HEADS, SEQ // GQA_TQ),
        in_specs=[
            pl.BlockSpec((HEAD_DIM, GQA_TQ), lambda h, i: (h, i)),
            pl.BlockSpec((KEYS, HEAD_DIM), lambda h, i: (0, h // grp)),
            pl.BlockSpec((HEAD_DIM, KEYS), lambda h, i: (h // grp, 0)),
        ],
        out_specs=pl.BlockSpec((HEAD_DIM, GQA_TQ), lambda h, i: (h, i)),
        out_shape=jax.ShapeDtypeStruct((GQA_WIDTH, SEQ), BF16),
        scratch_shapes=[pltpu.VMEM((GQA_TK, GQA_TQ), F32), pltpu.VMEM((GQA_TK, GQA_TQ), F32)],
        compiler_params=_cparams(("arbitrary", "arbitrary"), 32),
        name="gqa_attn",
    )(q_t, k_all, v_t)


def _proj_out_kernel(ona_ref, osg_ref, ogq_ref, w_ref, x_ref, g1_ref, lng_ref, lnb_ref,
                     sh2_ref, sc2_ref, xmid_ref, h2_ref):
    y = (_dot(ona_ref[...], w_ref[0:NA_WIDTH, :])
         + _dot(osg_ref[...], w_ref[NA_WIDTH:NA_WIDTH + SG_WIDTH, :])
         + _dot(ogq_ref[...], w_ref[NA_WIDTH + SG_WIDTH:, :]))
    z = ALPHA * x_ref[...] + g1_ref[...] * y
    xm = _layer_norm_rows(z) * lng_ref[...] + lnb_ref[...]
    xmid_ref[...] = xm
    h2_ref[...] = _layer_norm_rows(xm) * (1.0 + sc2_ref[...]) + sh2_ref[...]


def _proj_out(o_na, o_sg, o_gq, w_bf16, x, g1, ln_g, ln_b, sh2, sc2):
    n = x.shape[0]
    tm = ROW_TILE
    vec = pl.BlockSpec((1, D_MODEL), lambda i: (0, 0))
    return pl.pallas_call(
        _proj_out_kernel,
        grid=(n // tm,),
        in_specs=[
            pl.BlockSpec((tm, NA_WIDTH), lambda i: (i, 0)),
            pl.BlockSpec((tm, SG_WIDTH), lambda i: (i, 0)),
            pl.BlockSpec((tm, GQA_WIDTH), lambda i: (i, 0)),
            pl.BlockSpec((MIX_WIDTH, D_MODEL), lambda i: (0, 0)),
            pl.BlockSpec((tm, D_MODEL), lambda i: (i, 0)),
            vec, vec, vec, vec, vec,
        ],
        out_specs=[pl.BlockSpec((tm, D_MODEL), lambda i: (i, 0)),
                   pl.BlockSpec((tm, D_MODEL), lambda i: (i, 0))],
        out_shape=[jax.ShapeDtypeStruct((n, D_MODEL), F32),
                   jax.ShapeDtypeStruct((n, D_MODEL), F32)],
        compiler_params=_cparams(("arbitrary",), 48),
        name="proj_out",
    )(o_na, o_sg, o_gq, w_bf16, x, g1, ln_g, ln_b, sh2, sc2)


def _router_kernel(h_ref, rwt_ref, rb_ref, e_ref, g_ref):
    tm = h_ref.shape[0]
    logits = _dot_nt(rwt_ref[...], h_ref[...].astype(BF16))
    scores = jax.nn.sigmoid(logits)
    sel = scores + rb_ref[...]
    sub = lax.broadcasted_iota(I32, (EXPERTS_PER_GROUP, tm), 0)
    best = None
    for g in range(N_GROUPS):
        rows = slice(g * EXPERTS_PER_GROUP, (g + 1) * EXPERTS_PER_GROUP)
        v = sel[rows, :]
        sc = scores[rows, :]
        m1 = jnp.max(v, axis=0, keepdims=True)
        i1 = jnp.min(jnp.where(v == m1, sub, EXPERTS_PER_GROUP), axis=0, keepdims=True)
        v2 = jnp.where(sub == i1, -jnp.inf, v)
        m2 = jnp.max(v2, axis=0, keepdims=True)
        i2 = jnp.min(jnp.where(v2 == m2, sub, EXPERTS_PER_GROUP), axis=0, keepdims=True)
        s1 = jnp.sum(jnp.where(sub == i1, sc, 0.0), axis=0, keepdims=True)
        s2 = jnp.sum(jnp.where(sub == i2, sc, 0.0), axis=0, keepdims=True)
        cand = (m1 + m2, i1 + g * EXPERTS_PER_GROUP, i2 + g * EXPERTS_PER_GROUP, s1, s2)
        if best is None:
            best = cand
        else:
            better = cand[0] > best[0]
            best = tuple(jnp.where(better, cn, bs) for cn, bs in zip(cand, best))
    _, e1, e2, s1, s2 = best
    tot = s1 + s2
    e_ref[...] = jnp.concatenate([e1, e2], axis=0)
    g_ref[...] = jnp.concatenate([s1 / tot, s2 / tot], axis=0)


def _router(h2, rw_t_bf16, rb_col):
    n = h2.shape[0]
    tm = ROW_TILE
    return pl.pallas_call(
        _router_kernel,
        grid=(n // tm,),
        in_specs=[
            pl.BlockSpec((tm, D_MODEL), lambda i: (i, 0)),
            pl.BlockSpec((N_EXPERTS, D_MODEL), lambda i: (0, 0)),
            pl.BlockSpec((N_EXPERTS, 1), lambda i: (0, 0)),
        ],
        out_specs=[pl.BlockSpec((TOP_K, tm), lambda i: (0, i)),
                   pl.BlockSpec((TOP_K, tm), lambda i: (0, i))],
        out_shape=[jax.ShapeDtypeStruct((TOP_K, n), I32),
                   jax.ShapeDtypeStruct((TOP_K, n), F32)],
        compiler_params=_cparams(("arbitrary",), 16),
        name="router",
    )(h2, rw_t_bf16, rb_col)


def _num_tiles(n_tok):
    return (n_tok * TOP_K + N_EXPERTS * (MOE_TM - 1) + MOE_TM - 1) // MOE_TM


def _positions_kernel(e_ref, dest_ref, meta_ref, rank_scr, *, n_tok):
    ch = POS_CHUNK
    n_chunks = n_tok // ch
    iota_e = lax.broadcasted_iota(I32, (N_EXPERTS, ch), 0)
    upper = (lax.broadcasted_iota(I32, (ch, ch), 0) <= lax.broadcasted_iota(I32, (ch, ch), 1)).astype(BF16)

    def one_hots(off):
        e0 = e_ref[0:1, pl.ds(off, ch)]
        e1 = e_ref[1:2, pl.ds(off, ch)]
        return (iota_e == e0).astype(F32), (iota_e == e1).astype(F32)

    def rank_body(c, carry):
        off = pl.multiple_of(c * ch, ch)
        oh0, oh1 = one_hots(off)
        both = oh0 + oh1
        incl = _dot(both.astype(BF16), upper)
        before = incl - both + carry
        rank_scr[0:1, pl.ds(off, ch)] = jnp.sum(oh0 * before, axis=0, keepdims=True)
        rank_scr[1:2, pl.ds(off, ch)] = jnp.sum(oh1 * (before + oh0), axis=0, keepdims=True)
        return carry + jnp.sum(both, axis=1, keepdims=True)

    counts = lax.fori_loop(0, n_chunks, rank_body, jnp.zeros((N_EXPERTS, 1), F32))
    tiles = (counts.astype(I32) + (MOE_TM - 1)) >> MOE_TM_LOG2
    tiles_b = jnp.broadcast_to(tiles.astype(F32), (N_EXPERTS, 128)).astype(BF16)
    lower = (lax.broadcasted_iota(I32, (N_EXPERTS, N_EXPERTS), 1)
             < lax.broadcasted_iota(I32, (N_EXPERTS, N_EXPERTS), 0)).astype(BF16)
    tile_start = _dot(lower, tiles_b)
    tile_end = tile_start + tiles.astype(F32)
    pad_start = tile_start[:, 0:1] * float(MOE_TM)

    def dest_body(c, carry):
        off = pl.multiple_of(c * ch, ch)
        oh0, oh1 = one_hots(off)
        d0 = rank_scr[0:1, pl.ds(off, ch)] + jnp.sum(oh0 * pad_start, axis=0, keepdims=True)
        d1 = rank_scr[1:2, pl.ds(off, ch)] + jnp.sum(oh1 * pad_start, axis=0, keepdims=True)
        dest_ref[0:1, pl.ds(off, ch)] = d0.astype(I32)
        dest_ref[1:2, pl.ds(off, ch)] = d1.astype(I32)
        return carry

    lax.fori_loop(0, n_chunks, dest_body, 0)

    tile_id = lax.broadcasted_iota(I32, (N_EXPERTS, 128), 1).astype(F32)
    tile_expert = jnp.sum((tile_end <= tile_id).astype(F32), axis=0, keepdims=True)
    tile_expert = jnp.minimum(tile_expert, float(N_EXPERTS - 1))
    n_used = jnp.sum(tiles.astype(F32), axis=0, keepdims=True)
    row = lax.broadcasted_iota(I32, (8, 128), 0)
    meta = jnp.where(row == 0, jnp.broadcast_to(tile_expert, (8, 128)),
                     jnp.where(row == 1, jnp.broadcast_to(n_used, (8, 128)), 0.0))
    meta_ref[...] = meta.astype(I32)


def _positions(e_all):
    n_tok = e_all.shape[1]
    return pl.pallas_call(
        functools.partial(_positions_kernel, n_tok=n_tok),
        out_shape=[jax.ShapeDtypeStruct((TOP_K, n_tok), I32),
                   jax.ShapeDtypeStruct((8, 128), I32)],
        scratch_shapes=[pltpu.VMEM((TOP_K, n_tok), F32)],
        compiler_params=pltpu.CompilerParams(vmem_limit_bytes=16 * 1024 * 1024),
        name="moe_positions",
    )(e_all)


def _dispatch_kernel(dest_ref, h_ref, xs_in_ref, xs_ref, sem, *, n_tok):
    del xs_in_ref
    tm = h_ref.shape[0]
    base = pl.program_id(0) * tm

    def row_copy(r, k):
        d = dest_ref[k * n_tok + base + r]
        return pltpu.make_async_copy(h_ref.at[pl.ds(r, 1), :], xs_ref.at[pl.ds(d, 1), :], sem)

    def start(r, carry):
        for k in range(TOP_K):
            row_copy(r, k).start()
        return carry

    def wait(r, carry):
        for k in range(TOP_K):
            row_copy(r, k).wait()
        return carry

    lax.fori_loop(0, tm, start, 0)
    lax.fori_loop(0, tm, wait, 0)


def _dispatch(dest_flat, h2, xs):
    n = h2.shape[0]
    tm = ROW_TILE
    return pl.pallas_call(
        functools.partial(_dispatch_kernel, n_tok=n),
        grid_spec=pltpu.PrefetchScalarGridSpec(
            num_scalar_prefetch=1,
            grid=(n // tm,),
            in_specs=[pl.BlockSpec((tm, D_MODEL), lambda i, d: (i, 0)),
                      pl.BlockSpec(memory_space=pl.ANY)],
            out_specs=pl.BlockSpec(memory_space=pl.ANY),
            scratch_shapes=[pltpu.SemaphoreType.DMA(())],
        ),
        out_shape=jax.ShapeDtypeStruct(xs.shape, xs.dtype),
        input_output_aliases={2: 0},
        compiler_params=_cparams(("arbitrary",), 16),
        name="moe_dispatch",
    )(dest_flat, h2, xs)


def _moe_kernel(meta_ref, x_ref, wg_ref, wu_ref, wd_ref, y_ref, xb_scr, acc_scr, *, n_tiles):
    i = pl.program_id(0)
    f = pl.program_id(1)
    used = i < meta_ref[n_tiles]

    @pl.when(used & (f == 0))
    def _():
        xb_scr[...] = x_ref[...].astype(BF16)

    @pl.when(used)
    def _():
        xb = xb_scr[...]
        gate = _dot(xb, wg_ref[...].astype(BF16))
        up = _dot(xb, wu_ref[...].astype(BF16))
        hidden = (gate * jax.nn.sigmoid(gate) * up).astype(BF16)
        part = _dot(hidden, wd_ref[...].astype(BF16))

        @pl.when(f == 0)
        def _():
            acc_scr[...] = part

        @pl.when(f != 0)
        def _():
            acc_scr[...] += part

    @pl.when(f == MOE_NF - 1)
    def _():
        @pl.when(used)
        def _():
            y_ref[...] = acc_scr[...]

        @pl.when(jnp.logical_not(used))
        def _():
            y_ref[...] = jnp.zeros_like(y_ref)


def _moe_ffn(meta_flat, xs, w_gate, w_up, w_down, layer, n_tiles):
    def last_used(i, m):
        return jnp.minimum(i, m[n_tiles] - 1)

    def f_eff(i, f, m):
        return jnp.where(i < m[n_tiles], f, MOE_NF - 1)

    return pl.pallas_call(
        functools.partial(_moe_kernel, n_tiles=n_tiles),
        grid_spec=pltpu.PrefetchScalarGridSpec(
            num_scalar_prefetch=1,
            grid=(n_tiles, MOE_NF),
            in_specs=[
                pl.BlockSpec((MOE_TM, D_MODEL), lambda i, f, m: (last_used(i, m), 0)),
                pl.BlockSpec((None, None, D_MODEL, MOE_TF),
                             lambda i, f, m: (layer, m[last_used(i, m)], 0, f_eff(i, f, m))),
                pl.BlockSpec((None, None, D_MODEL, MOE_TF),
                             lambda i, f, m: (layer, m[last_used(i, m)], 0, f_eff(i, f, m))),
                pl.BlockSpec((None, None, MOE_TF, D_MODEL),
                             lambda i, f, m: (layer, m[last_used(i, m)], f_eff(i, f, m), 0)),
            ],
            out_specs=pl.BlockSpec((MOE_TM, D_MODEL), lambda i, f, m: (i, 0)),
            scratch_shapes=[pltpu.VMEM((MOE_TM, D_MODEL), BF16),
                            pltpu.VMEM((MOE_TM, D_MODEL), F32)],
        ),
        out_shape=jax.ShapeDtypeStruct((n_tiles * MOE_TM, D_MODEL), F32),
        compiler_params=_cparams(("arbitrary", "arbitrary"), 56),
        name="moe_ffn",
    )(meta_flat, xs, w_gate, w_up, w_down)


def _combine_kernel(dest_ref, y_ref, gt_ref, x_ref, g2_ref, lng_ref, lnb_ref, o_ref,
                    buf_ref, sem, *, n_tok):
    tm = x_ref.shape[0]
    base = pl.program_id(0) * tm

    def row_copy(r, k):
        d = dest_ref[k * n_tok + base + r]
        return pltpu.make_async_copy(y_ref.at[pl.ds(d, 1), :], buf_ref.at[k, pl.ds(r, 1), :], sem)

    def start(r, carry):
        for k in range(TOP_K):
            row_copy(r, k).start()
        return carry

    def wait(r, carry):
        for k in range(TOP_K):
            row_copy(r, k).wait()
        return carry

    lax.fori_loop(0, tm, start, 0)
    lax.fori_loop(0, tm, wait, 0)
    gt = gt_ref[...]
    moe = buf_ref[0] * gt[:, 0:1] + buf_ref[1] * gt[:, 1:2]
    z = ALPHA * x_ref[...] + g2_ref[...] * moe
    o_ref[...] = _layer_norm_rows(z) * lng_ref[...] + lnb_ref[...]


def _combine(dest_flat, y, gates_t, x_mid, g2, ln_g, ln_b):
    n = x_mid.shape[0]
    tm = ROW_TILE
    vec = pl.BlockSpec((1, D_MODEL), lambda i, d: (0, 0))
    return pl.pallas_call(
        functools.partial(_combine_kernel, n_tok=n),
        grid_spec=pltpu.PrefetchScalarGridSpec(
            num_scalar_prefetch=1,
            grid=(n // tm,),
            in_specs=[pl.BlockSpec(memory_space=pl.ANY),
                      pl.BlockSpec((tm, TOP_K), lambda i, d: (i, 0)),
                      pl.BlockSpec((tm, D_MODEL), lambda i, d: (i, 0)),
                      vec, vec, vec],
            out_specs=pl.BlockSpec((tm, D_MODEL), lambda i, d: (i, 0)),
            scratch_shapes=[pltpu.VMEM((TOP_K, tm, D_MODEL), F32),
                            pltpu.SemaphoreType.DMA(())],
        ),
        out_shape=jax.ShapeDtypeStruct((n, D_MODEL), F32),
        compiler_params=_cparams(("arbitrary",), 32),
        name="moe_combine",
    )(dest_flat, y, gates_t, x_mid, g2, ln_g, ln_b)


def _rope_tables():
    t = jnp.arange(SEQ, dtype=jnp.int32)
    row = (t // GRID_W).astype(F32)
    col = (t % GRID_W).astype(F32)
    inv_freq = ROPE_THETA ** (-jnp.arange(ROPE_PAIRS, dtype=F32) / ROPE_PAIRS)
    ang_r = row[:, None] * inv_freq
    ang_c = col[:, None] * inv_freq
    cos = jnp.concatenate([jnp.cos(ang_r), jnp.cos(ang_r), jnp.cos(ang_c), jnp.cos(ang_c)], axis=-1)
    sin = jnp.concatenate([-jnp.sin(ang_r), jnp.sin(ang_r), -jnp.sin(ang_c), jnp.sin(ang_c)], axis=-1)
    return cos, sin


def kernel(x, c, ctx, c_ctx, ada_w, ada_b, w_in, na_rpb, sg_ln_g, sg_ln_b, sg_w, sg_b, q_norm_g, k_norm_g, w_out, ln_mix_g, ln_mix_b, router_w, router_bias, moe_w_gate, moe_w_up, moe_w_down, ln_ffn_g, ln_ffn_b):
    cos, sin = _rope_tables()
    cos_ctx = jnp.ones((CTX_LEN, HEAD_DIM), F32)
    sin_ctx = jnp.zeros((CTX_LEN, HEAD_DIM), F32)

    cvec = jnp.zeros((8, D_MODEL), F32).at[0].set(c[0]).at[1].set(c_ctx)
    mods = _ada(cvec, ada_w, ada_b)

    rw_t = router_w.T.astype(BF16)
    rb_col = router_bias.reshape(N_EXPERTS, 1).astype(F32)

    xl = x[0]
    xc = ctx[0]
    for layer in range(DEPTH):
        last = layer == DEPTH - 1
        sh1, sc1, g1, sh2, sc2, g2 = jnp.split(mods[layer, 0:1], 6, axis=-1)
        csh1, csc1, cg1, csh2, csc2, cg2 = jnp.split(mods[layer, 1:2], 6, axis=-1)
        w_in_b = w_in[layer].astype(BF16)
        w_out_b = w_out[layer].astype(BF16)
        sg_w_b = sg_w[layer].astype(BF16)
        q_gain = q_norm_g[layer].reshape(1, HEAD_DIM)
        k_gain = k_norm_g[layer].reshape(1, HEAD_DIM)
        lng = ln_mix_g[layer].reshape(1, D_MODEL)
        lnb = ln_mix_b[layer].reshape(1, D_MODEL)
        fng = ln_ffn_g[layer].reshape(1, D_MODEL)
        fnb = ln_ffn_b[layer].reshape(1, D_MODEL)

        proj, sgz = _proj_in(xl, sh1, sc1, w_in_b, cos, sin, q_gain, k_gain, tm=1024)
        proj_c, sgz_c = _proj_in(xc, csh1, csc1, w_in_b, cos_ctx, sin_ctx, q_gain, k_gain, tm=CTX_LEN)

        o_na = _na_attention(proj, proj_c, _na_bias_tables(na_rpb[layer]))
        o_sg = _spatial_gating(sgz, sg_ln_g[layer], sg_ln_b[layer], sg_w_b, sg_b[layer])
        q_t = proj[:, COL_GQ * HEAD_DIM:COL_GK * HEAD_DIM].T
        k_all = jnp.concatenate([proj[:, COL_GK * HEAD_DIM:COL_GV * HEAD_DIM],
                                 proj_c[:, COL_GK * HEAD_DIM:COL_GV * HEAD_DIM]], axis=0)
        v_t = jnp.concatenate([proj[:, COL_GV * HEAD_DIM:], proj_c[:, COL_GV * HEAD_DIM:]], axis=0).T
        o_gq = _gqa_attention(q_t, k_all, v_t).T
        x_mid, h2 = _proj_out(o_na, o_sg, o_gq, w_out_b, xl, g1, lng, lnb, sh2, sc2)

        if last:
            n_tok = SEQ
            e_all, gates = _router(h2, rw_t, rb_col)
        else:
            n_tok = SEQ + CTX_LEN
            c_na = _ctx_attention(proj_c, NA_HEADS, COL_NA_Q, COL_NA_K, COL_NA_V, 1, False)
            c_sg = _spatial_gating(sgz_c, sg_ln_g[layer], sg_ln_b[layer], sg_w_b, sg_b[layer])
            c_gq = _ctx_attention(proj_c, GQA_Q_HEADS, COL_GQ, COL_GK, COL_GV,
                                  GQA_Q_HEADS // GQA_KV_HEADS, True)
            xc_mid, h2_c = _proj_out(c_na, c_sg, c_gq, w_out_b, xc, cg1, lng, lnb, csh2, csc2)
            e_lat, gates_lat = _router(h2, rw_t, rb_col)
            e_ctx, gates_ctx = _router(h2_c, rw_t, rb_col)
            e_all = jnp.concatenate([e_lat, e_ctx], axis=1)
            gates = jnp.concatenate([gates_lat, gates_ctx], axis=1)

        n_tiles = _num_tiles(n_tok)
        dest, meta = _positions(e_all)
        meta_flat = jnp.concatenate([meta[0, :n_tiles], meta[1, :1]])
        xs = jnp.zeros((n_tiles * MOE_TM, D_MODEL), F32)
        xs = _dispatch(dest[:, :SEQ].reshape(-1), h2, xs)
        if not last:
            xs = _dispatch(dest[:, SEQ:].reshape(-1), h2_c, xs)
        y = _moe_ffn(meta_flat, xs, moe_w_gate, moe_w_up, moe_w_down, layer, n_tiles)
        gates_t = gates.T
        xl = _combine(dest[:, :SEQ].reshape(-1), y, gates_t[:SEQ], x_mid, g2, fng, fnb)
        if not last:
            xc = _combine(dest[:, SEQ:].reshape(-1), y, gates_t[SEQ:], xc_mid, cg2, fng, fnb)
    return xl[None]
```

```python
import functools

import jax
import jax.numpy as jnp
import numpy as np
from jax import lax
from jax.experimental import pallas as pl
from jax.experimental.pallas import tpu as pltpu

F32 = jnp.float32
BF16 = jnp.bfloat16
I32 = jnp.int32

D_MODEL = 2048
SEQ = 8192
DEPTH = 2
GRID_W = 64
ROWS = SEQ // GRID_W
CTX_LEN = 256
HEAD_DIM = 128

NA_HEADS = 4
WIN_H = 8
WIN_W = 16
SG_GROUPS = 4
SG_CHUNK = 128
GQA_Q_HEADS = 8
GQA_KV_HEADS = 2
ROPE_THETA = 10000.0
ROPE_PAIRS = HEAD_DIM // 4

NA_WIDTH = NA_HEADS * HEAD_DIM
SG_WIDTH = SG_GROUPS * HEAD_DIM
GQA_WIDTH = GQA_Q_HEADS * HEAD_DIM
KV_WIDTH = GQA_KV_HEADS * HEAD_DIM
MIX_WIDTH = NA_WIDTH + SG_WIDTH + GQA_WIDTH
IN_WIDTH = 3 * NA_WIDTH + 2 * SG_WIDTH + GQA_WIDTH + 2 * KV_WIDTH

N_EXPERTS = 32
N_GROUPS = 4
EXPERTS_PER_GROUP = N_EXPERTS // N_GROUPS
TOP_K = 2
D_EXPERT = 1024

ALPHA = (2 * DEPTH) ** 0.25
LN_EPS = 1e-6
ATTN_SCALE = HEAD_DIM ** -0.5
LOG2E = 1.4426950408889634
NEG_BIG = -1e30
KEYS = SEQ + CTX_LEN

COL_NA_Q = 0
COL_NA_K = NA_HEADS
COL_NA_V = 2 * NA_HEADS
COL_SG = 3 * NA_HEADS
COL_GQ = COL_SG + 2 * SG_GROUPS
COL_GK = COL_GQ + GQA_Q_HEADS
COL_GV = COL_GK + GQA_KV_HEADS

PROJ_TN = 512
PROJ_NJ = IN_WIDTH // PROJ_TN

NA_R = 4
NA_TQ = NA_R * GRID_W
NA_KR = NA_R + WIN_H - 1
NA_TK = NA_KR * GRID_W
NA_NB = ROWS // NA_R

MOE_TM = 768
MOE_SUB = 128
MOE_SUB_LOG2 = 7
MOE_TF = 512
MOE_NF = D_EXPERT // MOE_TF
META_TILE_EXPERT = 0
META_TILE_ROWS = 1
META_N_USED = 2
META_LANES = 128
POS_CHUNK = 256

ROW_TILE = 256


def _cparams(semantics, vmem_mib):
    return pltpu.CompilerParams(dimension_semantics=semantics,
                                vmem_limit_bytes=vmem_mib * 1024 * 1024)


def _layer_norm_rows(x):
    mu = jnp.mean(x, axis=-1, keepdims=True)
    xc = x - mu
    var = jnp.mean(xc * xc, axis=-1, keepdims=True)
    return xc * lax.rsqrt(var + LN_EPS)


def _dot(a, b):
    return jnp.dot(a, b, preferred_element_type=F32)


def _dot_nt(a, b):
    return lax.dot_general(a, b, (((1,), (1,)), ((), ())), preferred_element_type=F32)


ADA_TN = 1024


def _ada_kernel(c_ref, w_ref, b_ref, o_ref):
    cv = c_ref[...]
    act = (cv * jax.nn.sigmoid(cv)).astype(BF16)
    o_ref[...] = _dot(act, w_ref[...].astype(BF16)) + b_ref[...]


def _ada(cvec, ada_w, ada_b):
    n_out = 6 * D_MODEL
    return pl.pallas_call(
        _ada_kernel,
        grid=(DEPTH, n_out // ADA_TN),
        in_specs=[
            pl.BlockSpec((8, D_MODEL), lambda l, j: (0, 0)),
            pl.BlockSpec((None, D_MODEL, ADA_TN), lambda l, j: (l, 0, j)),
            pl.BlockSpec((None, 1, ADA_TN), lambda l, j: (l, 0, j)),
        ],
        out_specs=pl.BlockSpec((None, 8, ADA_TN), lambda l, j: (l, 0, j)),
        out_shape=jax.ShapeDtypeStruct((DEPTH, 8, n_out), F32),
        compiler_params=_cparams(("arbitrary", "arbitrary"), 40),
        name="ada_mod",
    )(cvec, ada_w, ada_b.reshape(DEPTH, 1, n_out))


def _rope_partner(y):
    lane = lax.broadcasted_iota(I32, y.shape, 1)
    first = (lane % (2 * ROPE_PAIRS)) < ROPE_PAIRS
    return jnp.where(first, pltpu.roll(y, HEAD_DIM - ROPE_PAIRS, 1), pltpu.roll(y, ROPE_PAIRS, 1))


def _norm_rope(zh, gain, cos, sin):
    ms = jnp.mean(zh * zh, axis=-1, keepdims=True)
    y = zh * lax.rsqrt(ms + LN_EPS) * gain
    return y * cos + _rope_partner(y) * sin


def _proj_in_kernel(x_ref, sh_ref, sc_ref, w_ref, cos_ref, sin_ref, qg_ref, kg_ref,
                    proj_ref, sgz_ref, h_scr):
    j = pl.program_id(1)

    @pl.when(j == 0)
    def _():
        h = _layer_norm_rows(x_ref[...]) * (1.0 + sc_ref[...]) + sh_ref[...]
        h_scr[...] = h.astype(BF16)

    z = _dot(h_scr[...], w_ref[...])

    @pl.when(j == 0)
    def _():
        proj_ref[...] = (z * ATTN_SCALE).astype(BF16)

    @pl.when((j == 1) | (j == 2))
    def _():
        proj_ref[...] = z.astype(BF16)

    @pl.when((j == 3) | (j == 4))
    def _():
        proj_ref[...] = z.astype(BF16)
        sgz_ref[...] = z

    @pl.when((j == 5) | (j == 6))
    def _():
        cos = cos_ref[...]
        sin = sin_ref[...]
        for hh in range(PROJ_TN // HEAD_DIM):
            cols = slice(hh * HEAD_DIM, (hh + 1) * HEAD_DIM)
            q = _norm_rope(z[:, cols], qg_ref[...], cos, sin)
            proj_ref[:, cols] = (q * (ATTN_SCALE * LOG2E)).astype(BF16)

    @pl.when(j == 7)
    def _():
        cos = cos_ref[...]
        sin = sin_ref[...]
        for hh in range(GQA_KV_HEADS):
            cols = slice(hh * HEAD_DIM, (hh + 1) * HEAD_DIM)
            proj_ref[:, cols] = _norm_rope(z[:, cols], kg_ref[...], cos, sin).astype(BF16)
        proj_ref[:, KV_WIDTH:] = z[:, KV_WIDTH:].astype(BF16)


def _proj_in(x, sh, sc, w_bf16, cos, sin, q_gain, k_gain, tm):
    n = x.shape[0]
    return pl.pallas_call(
        _proj_in_kernel,
        grid=(n // tm, PROJ_NJ),
        in_specs=[
            pl.BlockSpec((tm, D_MODEL), lambda i, j: (i, 0)),
            pl.BlockSpec((1, D_MODEL), lambda i, j: (0, 0)),
            pl.BlockSpec((1, D_MODEL), lambda i, j: (0, 0)),
            pl.BlockSpec((D_MODEL, PROJ_TN), lambda i, j: (0, j)),
            pl.BlockSpec((tm, HEAD_DIM), lambda i, j: (i, 0)),
            pl.BlockSpec((tm, HEAD_DIM), lambda i, j: (i, 0)),
            pl.BlockSpec((1, HEAD_DIM), lambda i, j: (0, 0)),
            pl.BlockSpec((1, HEAD_DIM), lambda i, j: (0, 0)),
        ],
        out_specs=[
            pl.BlockSpec((tm, PROJ_TN), lambda i, j: (i, j)),
            pl.BlockSpec((tm, PROJ_TN), lambda i, j: (i, jnp.clip(j - 3, 0, 1))),
        ],
        out_shape=[
            jax.ShapeDtypeStruct((n, IN_WIDTH), BF16),
            jax.ShapeDtypeStruct((n, 2 * SG_WIDTH), F32),
        ],
        scratch_shapes=[pltpu.VMEM((tm, D_MODEL), BF16)],
        compiler_params=_cparams(("arbitrary", "arbitrary"), 48),
        name="proj_in",
    )(x, sh, sc, w_bf16, cos, sin, q_gain, k_gain)


def _na_kernel(q_ref, k_ref, v_ref, kc_ref, vc_ref, bias_ref, o_ref):
    b = pl.program_id(1)
    key_row0 = jnp.clip(b * NA_R - WIN_H // 2, 0, ROWS - NA_KR)
    start = pl.multiple_of(key_row0 * GRID_W, GRID_W)
    q = q_ref[...]
    kw = k_ref[pl.ds(start, NA_TK), :]
    vw = v_ref[pl.ds(start, NA_TK), :]
    s_loc = _dot_nt(q, kw) + bias_ref[...]
    s_ctx = _dot_nt(q, kc_ref[...])
    m = jnp.maximum(jnp.max(s_loc, axis=-1, keepdims=True), jnp.max(s_ctx, axis=-1, keepdims=True))
    p_loc = jnp.exp(s_loc - m)
    p_ctx = jnp.exp(s_ctx - m)
    denom = jnp.sum(p_loc, axis=-1, keepdims=True) + jnp.sum(p_ctx, axis=-1, keepdims=True)
    o = _dot(p_loc.astype(BF16), vw) + _dot(p_ctx.astype(BF16), vc_ref[...])
    o_ref[...] = (o / denom).astype(BF16)


def _na_bias_tables(rpb):
    tables = []
    for r0, k0 in ((0, 0), (2 * NA_R, 2 * NA_R - WIN_H // 2), (ROWS - NA_R, ROWS - NA_KR)):
        r = r0 + np.arange(NA_R)
        rs = np.clip(r - WIN_H // 2, 0, ROWS - WIN_H)
        kr = k0 + np.arange(NA_KR)
        ok_r = (kr[None, :] >= rs[:, None]) & (kr[None, :] < rs[:, None] + WIN_H)
        drow = kr[None, :] - r[:, None] + (WIN_H - 1)
        c = np.arange(GRID_W)
        cs = np.clip(c - WIN_W // 2, 0, GRID_W - WIN_W)
        kc = np.arange(GRID_W)
        ok_c = (kc[None, :] >= cs[:, None]) & (kc[None, :] < cs[:, None] + WIN_W)
        dcol = kc[None, :] - c[:, None] + (WIN_W - 1)
        ok = ok_r[:, None, :, None] & ok_c[None, :, None, :]
        pick_r = (np.clip(drow, 0, 2 * WIN_H - 2)[..., None] == np.arange(2 * WIN_H - 1)).astype(np.float32)
        pick_c = (np.clip(dcol, 0, 2 * WIN_W - 2)[..., None] == np.arange(2 * WIN_W - 1)).astype(np.float32)
        bias = jnp.einsum('rka,hab,cwb->hrckw', pick_r, rpb.astype(F32), pick_c,
                          precision=lax.Precision.HIGHEST)
        tables.append(jnp.where(ok[None], bias, NEG_BIG).reshape(NA_HEADS, NA_TQ, NA_TK))
    return jnp.stack(tables)


def _na_attention(proj, proj_ctx, bias_tab):
    def bias_idx(h, b):
        return (jnp.where(b == 0, 0, jnp.where(b == NA_NB - 1, 2, 1)), h, 0, 0)

    return pl.pallas_call(
        _na_kernel,
        grid=(NA_HEADS, NA_NB),
        in_specs=[
            pl.BlockSpec((NA_TQ, HEAD_DIM), lambda h, b: (b, COL_NA_Q + h)),
            pl.BlockSpec((SEQ, HEAD_DIM), lambda h, b: (0, COL_NA_K + h)),
            pl.BlockSpec((SEQ, HEAD_DIM), lambda h, b: (0, COL_NA_V + h)),
            pl.BlockSpec((CTX_LEN, HEAD_DIM), lambda h, b: (0, COL_NA_K + h)),
            pl.BlockSpec((CTX_LEN, HEAD_DIM), lambda h, b: (0, COL_NA_V + h)),
            pl.BlockSpec((None, None, NA_TQ, NA_TK), bias_idx),
        ],
        out_specs=pl.BlockSpec((NA_TQ, HEAD_DIM), lambda h, b: (b, h)),
        out_shape=jax.ShapeDtypeStruct((SEQ, NA_WIDTH), BF16),
        compiler_params=_cparams(("arbitrary", "arbitrary"), 32),
        name="na_attn",
    )(proj, proj, proj, proj_ctx, proj_ctx, bias_tab)


def _ctx_attn_kernel(q_ref, k_ref, v_ref, o_ref, *, log2_domain):
    s = _dot_nt(q_ref[...], k_ref[...])
    m = jnp.max(s, axis=-1, keepdims=True)
    p = jnp.exp2(s - m) if log2_domain else jnp.exp(s - m)
    denom = jnp.sum(p, axis=-1, keepdims=True)
    o_ref[...] = (_dot(p.astype(BF16), v_ref[...]) / denom).astype(BF16)


def _ctx_attention(proj_ctx, n_heads, q_col, k_col, v_col, q_per_kv, log2_domain):
    return pl.pallas_call(
        functools.partial(_ctx_attn_kernel, log2_domain=log2_domain),
        grid=(n_heads,),
        in_specs=[
            pl.BlockSpec((CTX_LEN, HEAD_DIM), lambda h: (0, q_col + h)),
            pl.BlockSpec((CTX_LEN, HEAD_DIM), lambda h: (0, k_col + h // q_per_kv)),
            pl.BlockSpec((CTX_LEN, HEAD_DIM), lambda h: (0, v_col + h // q_per_kv)),
        ],
        out_specs=pl.BlockSpec((CTX_LEN, HEAD_DIM), lambda h: (0, h)),
        out_shape=jax.ShapeDtypeStruct((CTX_LEN, n_heads * HEAD_DIM), BF16),
        compiler_params=_cparams(("arbitrary",), 16),
        name="ctx_attn",
    )(proj_ctx, proj_ctx, proj_ctx)


SG_ROWS = 256


def _sg_kernel(zu_ref, zv_ref, g_ref, b_ref, w_ref, bs_ref, o_ref):
    for c in range(SG_ROWS // SG_CHUNK):
        rows = slice(c * SG_CHUNK, (c + 1) * SG_CHUNK)
        for g in range(SG_GROUPS):
            cols = slice(g * HEAD_DIM, (g + 1) * HEAD_DIM)
            v = jax.nn.gelu(zv_ref[rows, cols])
            vn = _layer_norm_rows(v) * g_ref[:, cols] + b_ref[:, cols]
            t = _dot(w_ref[g], vn.astype(BF16)) + bs_ref[:, g:g + 1]
            o_ref[rows, cols] = (jax.nn.gelu(zu_ref[rows, cols]) * t).astype(BF16)


def _spatial_gating(sgz, ln_g, ln_b, w_bf16, b_s):
    n = sgz.shape[0]
    return pl.pallas_call(
        _sg_kernel,
        grid=(n // SG_ROWS,),
        in_specs=[
            pl.BlockSpec((SG_ROWS, SG_WIDTH), lambda i: (i, 0)),
            pl.BlockSpec((SG_ROWS, SG_WIDTH), lambda i: (i, 1)),
            pl.BlockSpec((1, SG_WIDTH), lambda i: (0, 0)),
            pl.BlockSpec((1, SG_WIDTH), lambda i: (0, 0)),
            pl.BlockSpec((SG_GROUPS, SG_CHUNK, SG_CHUNK), lambda i: (0, 0, 0)),
            pl.BlockSpec((SG_CHUNK, SG_GROUPS), lambda i: (0, 0)),
        ],
        out_specs=pl.BlockSpec((SG_ROWS, SG_WIDTH), lambda i: (i, 0)),
        out_shape=jax.ShapeDtypeStruct((n, SG_WIDTH), BF16),
        compiler_params=_cparams(("arbitrary",), 16),
        name="spatial_gating",
    )(sgz, sgz, ln_g.reshape(1, SG_WIDTH), ln_b.reshape(1, SG_WIDTH), w_bf16, b_s.T)


GQA_TQ = 512
GQA_TK = 512


def _gqa_kernel(qt_ref, k_ref, vt_ref, ot_ref, sa_ref, sb_ref):
    qt = qt_ref[...]
    n_chunks = SEQ // GQA_TK

    def scores(t, dst_ref):
        off = pl.multiple_of(t * GQA_TK, GQA_TK)
        s = _dot(k_ref[pl.ds(off, GQA_TK), :], qt)
        dst_ref[...] = s
        return jnp.max(s, axis=0, keepdims=True)

    def consume(t, src_ref, m_chunk, m, l, acc):
        off = pl.multiple_of(t * GQA_TK, GQA_TK)
        m_new = jnp.maximum(m, m_chunk)
        alpha = jnp.exp2(m - m_new)
        p = jnp.exp2(src_ref[...] - m_new)
        l = alpha * l + jnp.sum(p, axis=0, keepdims=True)
        acc = alpha * acc + _dot(vt_ref[:, pl.ds(off, GQA_TK)], p.astype(BF16))
        return m_new, l, acc

    s = _dot(k_ref[pl.ds(SEQ, CTX_LEN), :], qt)
    m = jnp.max(s, axis=0, keepdims=True)
    p = jnp.exp2(s - m)
    l = jnp.sum(p, axis=0, keepdims=True)
    acc = _dot(vt_ref[:, pl.ds(SEQ, CTX_LEN)], p.astype(BF16))
    mc_a = scores(0, sa_ref)

    def body(u, carry):
        m, l, acc, mc_a = carry
        mc_b = scores(2 * u + 1, sb_ref)
        m, l, acc = consume(2 * u, sa_ref, mc_a, m, l, acc)
        mc_a = scores(2 * u + 2, sa_ref)
        m, l, acc = consume(2 * u + 1, sb_ref, mc_b, m, l, acc)
        return m, l, acc, mc_a

    m, l, acc, mc_a = lax.fori_loop(0, n_chunks // 2 - 1, body, (m, l, acc, mc_a))
    mc_b = scores(n_chunks - 1, sb_ref)
    m, l, acc = consume(n_chunks - 2, sa_ref, mc_a, m, l, acc)
    m, l, acc = consume(n_chunks - 1, sb_ref, mc_b, m, l, acc)
    ot_ref[...] = (acc / l).astype(BF16)


def _gqa_attention(q_t, k_all, v_t):
    grp = GQA_Q_HEADS // GQA_KV_HEADS
    return pl.pallas_call(
        _gqa_kernel,
        grid=(GQA_Q_HEADS, SEQ // GQA_TQ),
        in_specs=[
            pl.BlockSpec((HEAD_DIM, GQA_TQ), lambda h, i: (h, i)),
            pl.BlockSpec((KEYS, HEAD_DIM), lambda h, i: (0, h // grp)),
            pl.BlockSpec((HEAD_DIM, KEYS), lambda h, i: (h // grp, 0)),
        ],
        out_specs=pl.BlockSpec((HEAD_DIM, GQA_TQ), lambda h, i: (h, i)),
        out_shape=jax.ShapeDtypeStruct((GQA_WIDTH, SEQ), BF16),
        scratch_shapes=[pltpu.VMEM((GQA_TK, GQA_TQ), F32), pltpu.VMEM((GQA_TK, GQA_TQ), F32)],
        compiler_params=_cparams(("arbitrary", "arbitrary"), 32),
        name="gqa_attn",
    )(q_t, k_all, v_t)


def _proj_out_kernel(ona_ref, osg_ref, ogq_ref, w_ref, x_ref, g1_ref, lng_ref, lnb_ref,
                     sh2_ref, sc2_ref, xmid_ref, h2_ref):
    y = (_dot(ona_ref[...], w_ref[0:NA_WIDTH, :])
         + _dot(osg_ref[...], w_ref[NA_WIDTH:NA_WIDTH + SG_WIDTH, :])
         + _dot(ogq_ref[...], w_ref[NA_WIDTH + SG_WIDTH:, :]))
    z = ALPHA * x_ref[...] + g1_ref[...] * y
    xm = _layer_norm_rows(z) * lng_ref[...] + lnb_ref[...]
    xmid_ref[...] = xm
    h2_ref[...] = _layer_norm_rows(xm) * (1.0 + sc2_ref[...]) + sh2_ref[...]


def _proj_out(o_na, o_sg, o_gq, w_bf16, x, g1, ln_g, ln_b, sh2, sc2):
    n = x.shape[0]
    tm = ROW_TILE
    vec = pl.BlockSpec((1, D_MODEL), lambda i: (0, 0))
    return pl.pallas_call(
        _proj_out_kernel,
        grid=(n // tm,),
        in_specs=[
            pl.BlockSpec((tm, NA_WIDTH), lambda i: (i, 0)),
            pl.BlockSpec((tm, SG_WIDTH), lambda i: (i, 0)),
            pl.BlockSpec((tm, GQA_WIDTH), lambda i: (i, 0)),
            pl.BlockSpec((MIX_WIDTH, D_MODEL), lambda i: (0, 0)),
            pl.BlockSpec((tm, D_MODEL), lambda i: (i, 0)),
            vec, vec, vec, vec, vec,
        ],
        out_specs=[pl.BlockSpec((tm, D_MODEL), lambda i: (i, 0)),
                   pl.BlockSpec((tm, D_MODEL), lambda i: (i, 0))],
        out_shape=[jax.ShapeDtypeStruct((n, D_MODEL), F32),
                   jax.ShapeDtypeStruct((n, D_MODEL), F32)],
        compiler_params=_cparams(("arbitrary",), 48),
        name="proj_out",
    )(o_na, o_sg, o_gq, w_bf16, x, g1, ln_g, ln_b, sh2, sc2)


def _router_kernel(h_ref, rwt_ref, rb_ref, e_ref, g_ref):
    tm = h_ref.shape[0]
    logits = _dot_nt(rwt_ref[...], h_ref[...].astype(BF16))
    scores = jax.nn.sigmoid(logits)
    sel = scores + rb_ref[...]
    sub = lax.broadcasted_iota(I32, (EXPERTS_PER_GROUP, tm), 0)
    best = None
    for g in range(N_GROUPS):
        rows = slice(g * EXPERTS_PER_GROUP, (g + 1) * EXPERTS_PER_GROUP)
        v = sel[rows, :]
        sc = scores[rows, :]
        m1 = jnp.max(v, axis=0, keepdims=True)
        i1 = jnp.min(jnp.where(v == m1, sub, EXPERTS_PER_GROUP), axis=0, keepdims=True)
        v2 = jnp.where(sub == i1, -jnp.inf, v)
        m2 = jnp.max(v2, axis=0, keepdims=True)
        i2 = jnp.min(jnp.where(v2 == m2, sub, EXPERTS_PER_GROUP), axis=0, keepdims=True)
        s1 = jnp.sum(jnp.where(sub == i1, sc, 0.0), axis=0, keepdims=True)
        s2 = jnp.sum(jnp.where(sub == i2, sc, 0.0), axis=0, keepdims=True)
        cand = (m1 + m2, i1 + g * EXPERTS_PER_GROUP, i2 + g * EXPERTS_PER_GROUP, s1, s2)
        if best is None:
            best = cand
        else:
            better = cand[0] > best[0]
            best = tuple(jnp.where(better, cn, bs) for cn, bs in zip(cand, best))
    _, e1, e2, s1, s2 = best
    tot = s1 + s2
    e_ref[...] = jnp.concatenate([e1, e2], axis=0)
    g_ref[...] = jnp.concatenate([s1 / tot, s2 / tot], axis=0)


def _router(h2, rw_t_bf16, rb_col):
    n = h2.shape[0]
    tm = ROW_TILE
    return pl.pallas_call(
        _router_kernel,
        grid=(n // tm,),
        in_specs=[
            pl.BlockSpec((tm, D_MODEL), lambda i: (i, 0)),
            pl.BlockSpec((N_EXPERTS, D_MODEL), lambda i: (0, 0)),
            pl.BlockSpec((N_EXPERTS, 1), lambda i: (0, 0)),
        ],
        out_specs=[pl.BlockSpec((TOP_K, tm), lambda i: (0, i)),
                   pl.BlockSpec((TOP_K, tm), lambda i: (0, i))],
        out_shape=[jax.ShapeDtypeStruct((TOP_K, n), I32),
                   jax.ShapeDtypeStruct((TOP_K, n), F32)],
        compiler_params=_cparams(("arbitrary",), 16),
        name="router",
    )(h2, rw_t_bf16, rb_col)


def _num_tiles(n_tok):
    return (n_tok * TOP_K + N_EXPERTS * (MOE_TM - 1) + MOE_TM - 1) // MOE_TM


def _positions_kernel(e_ref, dest_ref, meta_ref, rank_scr, *, n_tok):
    ch = POS_CHUNK
    n_chunks = n_tok // ch
    iota_e = lax.broadcasted_iota(I32, (N_EXPERTS, ch), 0)
    upper = (lax.broadcasted_iota(I32, (ch, ch), 0) <= lax.broadcasted_iota(I32, (ch, ch), 1)).astype(BF16)

    def one_hots(off):
        e0 = e_ref[0:1, pl.ds(off, ch)]
        e1 = e_ref[1:2, pl.ds(off, ch)]
        return (iota_e == e0).astype(F32), (iota_e == e1).astype(F32)

    def rank_body(c, carry):
        off = pl.multiple_of(c * ch, ch)
        oh0, oh1 = one_hots(off)
        both = oh0 + oh1
        incl = _dot(both.astype(BF16), upper)
        before = incl - both + carry
        rank_scr[0:1, pl.ds(off, ch)] = jnp.sum(oh0 * before, axis=0, keepdims=True)
        rank_scr[1:2, pl.ds(off, ch)] = jnp.sum(oh1 * (before + oh0), axis=0, keepdims=True)
        return carry + jnp.sum(both, axis=1, keepdims=True)

    counts = lax.fori_loop(0, n_chunks, rank_body, jnp.zeros((N_EXPERTS, 1), F32))
    tiles = jnp.zeros((N_EXPERTS, 1), F32)
    for k in range(n_tok * TOP_K // MOE_TM + 1):
        tiles = tiles + (counts > float(k * MOE_TM)).astype(F32)
    tiles_b = jnp.broadcast_to(tiles, (N_EXPERTS, META_LANES)).astype(BF16)
    lower = (lax.broadcasted_iota(I32, (N_EXPERTS, N_EXPERTS), 1)
             < lax.broadcasted_iota(I32, (N_EXPERTS, N_EXPERTS), 0)).astype(BF16)
    tile_start = _dot(lower, tiles_b)
    tile_end = tile_start + tiles
    pad_start = tile_start[:, 0:1] * float(MOE_TM)

    def dest_body(c, carry):
        off = pl.multiple_of(c * ch, ch)
        oh0, oh1 = one_hots(off)
        d0 = rank_scr[0:1, pl.ds(off, ch)] + jnp.sum(oh0 * pad_start, axis=0, keepdims=True)
        d1 = rank_scr[1:2, pl.ds(off, ch)] + jnp.sum(oh1 * pad_start, axis=0, keepdims=True)
        dest_ref[0:1, pl.ds(off, ch)] = d0.astype(I32)
        dest_ref[1:2, pl.ds(off, ch)] = d1.astype(I32)
        return carry

    lax.fori_loop(0, n_chunks, dest_body, 0)

    tile_id = lax.broadcasted_iota(I32, (N_EXPERTS, META_LANES), 1).astype(F32)
    tile_expert = jnp.sum((tile_end <= tile_id).astype(F32), axis=0, keepdims=True)
    tile_expert = jnp.minimum(tile_expert, float(N_EXPERTS - 1))
    inside = (tile_start <= tile_id) & (tile_id < tile_end)
    rows_here = jnp.clip(counts - float(MOE_TM) * (tile_id - tile_start), 0.0, float(MOE_TM))
    tile_rows = jnp.sum(jnp.where(inside, rows_here, 0.0), axis=0, keepdims=True)
    n_used = jnp.sum(tiles, axis=0, keepdims=True)
    row = lax.broadcasted_iota(I32, (8, META_LANES), 0)
    shape = (8, META_LANES)
    meta = jnp.where(row == META_TILE_EXPERT, jnp.broadcast_to(tile_expert, shape),
                     jnp.where(row == META_TILE_ROWS, jnp.broadcast_to(tile_rows, shape),
                               jnp.where(row == META_N_USED, jnp.broadcast_to(n_used, shape), 0.0)))
    meta_ref[...] = meta.astype(I32)


def _positions(e_all):
    n_tok = e_all.shape[1]
    return pl.pallas_call(
        functools.partial(_positions_kernel, n_tok=n_tok),
        out_shape=[jax.ShapeDtypeStruct((TOP_K, n_tok), I32),
                   jax.ShapeDtypeStruct((8, 128), I32)],
        scratch_shapes=[pltpu.VMEM((TOP_K, n_tok), F32)],
        compiler_params=pltpu.CompilerParams(vmem_limit_bytes=16 * 1024 * 1024),
        name="moe_positions",
    )(e_all)


def _dispatch_kernel(dest_ref, h_ref, xs_in_ref, xs_ref, sem, *, n_tok):
    del xs_in_ref
    tm = h_ref.shape[0]
    base = pl.program_id(0) * tm

    def row_copy(r, k):
        d = dest_ref[k * n_tok + base + r]
        return pltpu.make_async_copy(h_ref.at[pl.ds(r, 1), :], xs_ref.at[pl.ds(d, 1), :], sem)

    def start(r, carry):
        for k in range(TOP_K):
            row_copy(r, k).start()
        return carry

    def wait(r, carry):
        for k in range(TOP_K):
            row_copy(r, k).wait()
        return carry

    lax.fori_loop(0, tm, start, 0)
    lax.fori_loop(0, tm, wait, 0)


def _dispatch(dest_flat, h2, xs):
    n = h2.shape[0]
    tm = ROW_TILE
    return pl.pallas_call(
        functools.partial(_dispatch_kernel, n_tok=n),
        grid_spec=pltpu.PrefetchScalarGridSpec(
            num_scalar_prefetch=1,
            grid=(n // tm,),
            in_specs=[pl.BlockSpec((tm, D_MODEL), lambda i, d: (i, 0)),
                      pl.BlockSpec(memory_space=pl.ANY)],
            out_specs=pl.BlockSpec(memory_space=pl.ANY),
            scratch_shapes=[pltpu.SemaphoreType.DMA(())],
        ),
        out_shape=jax.ShapeDtypeStruct(xs.shape, xs.dtype),
        input_output_aliases={2: 0},
        compiler_params=_cparams(("arbitrary",), 16),
        name="moe_dispatch",
    )(dest_flat, h2, xs)


def _moe_kernel(meta_ref, x_ref, wg_ref, wu_ref, wd_ref, y_ref, xb_scr):
    i = pl.program_id(0)
    f = pl.program_id(1)
    used = i < meta_ref[META_N_USED * META_LANES]
    rows = meta_ref[META_TILE_ROWS * META_LANES + i]
    n_sub = (rows + (MOE_SUB - 1)) >> MOE_SUB_LOG2

    @pl.when(f == 0)
    def _():
        y_ref[...] = jnp.zeros_like(y_ref)

    for occupied in range(1, MOE_TM // MOE_SUB + 1):
        m_rows = occupied * MOE_SUB

        @pl.when(used & (n_sub == occupied))
        def _(m_rows=m_rows):
            @pl.when(f == 0)
            def _():
                xb_scr[0:m_rows, :] = x_ref[0:m_rows, :].astype(BF16)

            xb = xb_scr[0:m_rows, :]
            gate = _dot(xb, wg_ref[...].astype(BF16))
            up = _dot(xb, wu_ref[...].astype(BF16))
            hidden = (gate * jax.nn.sigmoid(gate) * up).astype(BF16)
            y_ref[0:m_rows, :] += _dot(hidden, wd_ref[...].astype(BF16))


def _moe_ffn(meta_flat, xs, w_gate, w_up, w_down, layer, n_tiles):
    n_used_at = META_N_USED * META_LANES

    def last_used(i, m):
        return jnp.minimum(i, m[n_used_at] - 1)

    def f_eff(i, f, m):
        return jnp.where(i < m[n_used_at], f, MOE_NF - 1)

    return pl.pallas_call(
        _moe_kernel,
        grid_spec=pltpu.PrefetchScalarGridSpec(
            num_scalar_prefetch=1,
            grid=(n_tiles, MOE_NF),
            in_specs=[
                pl.BlockSpec((MOE_TM, D_MODEL), lambda i, f, m: (last_used(i, m), 0)),
                pl.BlockSpec((None, None, D_MODEL, MOE_TF),
                             lambda i, f, m: (layer, m[last_used(i, m)], 0, f_eff(i, f, m))),
                pl.BlockSpec((None, None, D_MODEL, MOE_TF),
                             lambda i, f, m: (layer, m[last_used(i, m)], 0, f_eff(i, f, m))),
                pl.BlockSpec((None, None, MOE_TF, D_MODEL),
                             lambda i, f, m: (layer, m[last_used(i, m)], f_eff(i, f, m), 0)),
            ],
            out_specs=pl.BlockSpec((MOE_TM, D_MODEL), lambda i, f, m: (i, 0)),
            scratch_shapes=[pltpu.VMEM((MOE_TM, D_MODEL), BF16)],
        ),
        out_shape=jax.ShapeDtypeStruct((n_tiles * MOE_TM, D_MODEL), F32),
        compiler_params=_cparams(("arbitrary", "arbitrary"), 56),
        name="moe_ffn",
    )(meta_flat, xs, w_gate, w_up, w_down)


def _combine_kernel(dest_ref, y_ref, gt_ref, x_ref, g2_ref, lng_ref, lnb_ref, o_ref,
                    buf_ref, sem, *, n_tok):
    tm = x_ref.shape[0]
    base = pl.program_id(0) * tm

    def row_copy(r, k):
        d = dest_ref[k * n_tok + base + r]
        return pltpu.make_async_copy(y_ref.at[pl.ds(d, 1), :], buf_ref.at[k, pl.ds(r, 1), :], sem)

    def start(r, carry):
        for k in range(TOP_K):
            row_copy(r, k).start()
        return carry

    def wait(r, carry):
        for k in range(TOP_K):
            row_copy(r, k).wait()
        return carry

    lax.fori_loop(0, tm, start, 0)
    lax.fori_loop(0, tm, wait, 0)
    gt = gt_ref[...]
    moe = buf_ref[0] * gt[:, 0:1] + buf_ref[1] * gt[:, 1:2]
    z = ALPHA * x_ref[...] + g2_ref[...] * moe
    o_ref[...] = _layer_norm_rows(z) * lng_ref[...] + lnb_ref[...]


def _combine(dest_flat, y, gates_t, x_mid, g2, ln_g, ln_b):
    n = x_mid.shape[0]
    tm = ROW_TILE
    vec = pl.BlockSpec((1, D_MODEL), lambda i, d: (0, 0))
    return pl.pallas_call(
        functools.partial(_combine_kernel, n_tok=n),
        grid_spec=pltpu.PrefetchScalarGridSpec(
            num_scalar_prefetch=1,
            grid=(n // tm,),
            in_specs=[pl.BlockSpec(memory_space=pl.ANY),
                      pl.BlockSpec((tm, TOP_K), lambda i, d: (i, 0)),
                      pl.BlockSpec((tm, D_MODEL), lambda i, d: (i, 0)),
                      vec, vec, vec],
            out_specs=pl.BlockSpec((tm, D_MODEL), lambda i, d: (i, 0)),
            scratch_shapes=[pltpu.VMEM((TOP_K, tm, D_MODEL), F32),
                            pltpu.SemaphoreType.DMA(())],
        ),
        out_shape=jax.ShapeDtypeStruct((n, D_MODEL), F32),
        compiler_params=_cparams(("arbitrary",), 32),
        name="moe_combine",
    )(dest_flat, y, gates_t, x_mid, g2, ln_g, ln_b)


def _rope_tables():
    t = jnp.arange(SEQ, dtype=jnp.int32)
    row = (t // GRID_W).astype(F32)
    col = (t % GRID_W).astype(F32)
    inv_freq = ROPE_THETA ** (-jnp.arange(ROPE_PAIRS, dtype=F32) / ROPE_PAIRS)
    ang_r = row[:, None] * inv_freq
    ang_c = col[:, None] * inv_freq
    cos = jnp.concatenate([jnp.cos(ang_r), jnp.cos(ang_r), jnp.cos(ang_c), jnp.cos(ang_c)], axis=-1)
    sin = jnp.concatenate([-jnp.sin(ang_r), jnp.sin(ang_r), -jnp.sin(ang_c), jnp.sin(ang_c)], axis=-1)
    return cos, sin


def kernel(x, c, ctx, c_ctx, ada_w, ada_b, w_in, na_rpb, sg_ln_g, sg_ln_b, sg_w, sg_b, q_norm_g, k_norm_g, w_out, ln_mix_g, ln_mix_b, router_w, router_bias, moe_w_gate, moe_w_up, moe_w_down, ln_ffn_g, ln_ffn_b):
    cos, sin = _rope_tables()
    cos_ctx = jnp.ones((CTX_LEN, HEAD_DIM), F32)
    sin_ctx = jnp.zeros((CTX_LEN, HEAD_DIM), F32)

    cvec = jnp.zeros((8, D_MODEL), F32).at[0].set(c[0]).at[1].set(c_ctx)
    mods = _ada(cvec, ada_w, ada_b)

    rw_t = router_w.T.astype(BF16)
    rb_col = router_bias.reshape(N_EXPERTS, 1).astype(F32)

    xl = x[0]
    xc = ctx[0]
    for layer in range(DEPTH):
        last = layer == DEPTH - 1
        sh1, sc1, g1, sh2, sc2, g2 = jnp.split(mods[layer, 0:1], 6, axis=-1)
        csh1, csc1, cg1, csh2, csc2, cg2 = jnp.split(mods[layer, 1:2], 6, axis=-1)
        w_in_b = w_in[layer].astype(BF16)
        w_out_b = w_out[layer].astype(BF16)
        sg_w_b = sg_w[layer].astype(BF16)
        q_gain = q_norm_g[layer].reshape(1, HEAD_DIM)
        k_gain = k_norm_g[layer].reshape(1, HEAD_DIM)
        lng = ln_mix_g[layer].reshape(1, D_MODEL)
        lnb = ln_mix_b[layer].reshape(1, D_MODEL)
        fng = ln_ffn_g[layer].reshape(1, D_MODEL)
        fnb = ln_ffn_b[layer].reshape(1, D_MODEL)

        proj, sgz = _proj_in(xl, sh1, sc1, w_in_b, cos, sin, q_gain, k_gain, tm=1024)
        proj_c, sgz_c = _proj_in(xc, csh1, csc1, w_in_b, cos_ctx, sin_ctx, q_gain, k_gain, tm=CTX_LEN)

        o_na = _na_attention(proj, proj_c, _na_bias_tables(na_rpb[layer]))
        o_sg = _spatial_gating(sgz, sg_ln_g[layer], sg_ln_b[layer], sg_w_b, sg_b[layer])
        q_t = proj[:, COL_GQ * HEAD_DIM:COL_GK * HEAD_DIM].T
        k_all = jnp.concatenate([proj[:, COL_GK * HEAD_DIM:COL_GV * HEAD_DIM],
                                 proj_c[:, COL_GK * HEAD_DIM:COL_GV * HEAD_DIM]], axis=0)
        v_t = jnp.concatenate([proj[:, COL_GV * HEAD_DIM:], proj_c[:, COL_GV * HEAD_DIM:]], axis=0).T
        o_gq = _gqa_attention(q_t, k_all, v_t).T
        x_mid, h2 = _proj_out(o_na, o_sg, o_gq, w_out_b, xl, g1, lng, lnb, sh2, sc2)

        if last:
            n_tok = SEQ
            e_all, gates = _router(h2, rw_t, rb_col)
        else:
            n_tok = SEQ + CTX_LEN
            c_na = _ctx_attention(proj_c, NA_HEADS, COL_NA_Q, COL_NA_K, COL_NA_V, 1, False)
            c_sg = _spatial_gating(sgz_c, sg_ln_g[layer], sg_ln_b[layer], sg_w_b, sg_b[layer])
            c_gq = _ctx_attention(proj_c, GQA_Q_HEADS, COL_GQ, COL_GK, COL_GV,
                                  GQA_Q_HEADS // GQA_KV_HEADS, True)
            xc_mid, h2_c = _proj_out(c_na, c_sg, c_gq, w_out_b, xc, cg1, lng, lnb, csh2, csc2)
            e_lat, gates_lat = _router(h2, rw_t, rb_col)
            e_ctx, gates_ctx = _router(h2_c, rw_t, rb_col)
            e_all = jnp.concatenate([e_lat, e_ctx], axis=1)
            gates = jnp.concatenate([gates_lat, gates_ctx], axis=1)

        n_tiles = _num_tiles(n_tok)
        dest, meta = _positions(e_all)
        meta_flat = meta.reshape(-1)
        xs = jnp.zeros((n_tiles * MOE_TM, D_MODEL), F32)
        xs = _dispatch(dest[:, :SEQ].reshape(-1), h2, xs)
        if not last:
            xs = _dispatch(dest[:, SEQ:].reshape(-1), h2_c, xs)
        y = _moe_ffn(meta_flat, xs, moe_w_gate, moe_w_up, moe_w_down, layer, n_tiles)
        gates_t = gates.T
        xl = _combine(dest[:, :SEQ].reshape(-1), y, gates_t[:SEQ], x_mid, g2, fng, fnb)
        if not last:
            xc = _combine(dest[:, SEQ:].reshape(-1), y, gates_t[SEQ:], xc_mid, cg2, fng, fnb)
    return xl[None]
```

```python
import functools

import jax
import jax.numpy as jnp
import numpy as np
from jax import lax
from jax.experimental import pallas as pl
from jax.experimental.pallas import tpu as pltpu

F32 = jnp.float32
BF16 = jnp.bfloat16
I32 = jnp.int32

D_MODEL = 2048
SEQ = 8192
DEPTH = 2
GRID_W = 64
ROWS = SEQ // GRID_W
CTX_LEN = 256
HEAD_DIM = 128

NA_HEADS = 4
WIN_H = 8
WIN_W = 16
SG_GROUPS = 4
SG_CHUNK = 128
GQA_Q_HEADS = 8
GQA_KV_HEADS = 2
ROPE_THETA = 10000.0
ROPE_PAIRS = HEAD_DIM // 4

NA_WIDTH = NA_HEADS * HEAD_DIM
SG_WIDTH = SG_GROUPS * HEAD_DIM
GQA_WIDTH = GQA_Q_HEADS * HEAD_DIM
KV_WIDTH = GQA_KV_HEADS * HEAD_DIM
MIX_WIDTH = NA_WIDTH + SG_WIDTH + GQA_WIDTH
IN_WIDTH = 3 * NA_WIDTH + 2 * SG_WIDTH + GQA_WIDTH + 2 * KV_WIDTH

N_EXPERTS = 32
N_GROUPS = 4
EXPERTS_PER_GROUP = N_EXPERTS // N_GROUPS
TOP_K = 2
D_EXPERT = 1024

ALPHA = (2 * DEPTH) ** 0.25
LN_EPS = 1e-6
ATTN_SCALE = HEAD_DIM ** -0.5
LOG2E = 1.4426950408889634
NEG_BIG = -1e30
KEYS = SEQ + CTX_LEN

COL_NA_Q = 0
COL_NA_K = NA_HEADS
COL_NA_V = 2 * NA_HEADS
COL_SG = 3 * NA_HEADS
COL_GQ = COL_SG + 2 * SG_GROUPS
COL_GK = COL_GQ + GQA_Q_HEADS
COL_GV = COL_GK + GQA_KV_HEADS

PROJ_TN = 512
PROJ_NJ = IN_WIDTH // PROJ_TN
PROJ_ROWS = 256

NA_R = 4
NA_TQ = NA_R * GRID_W
NA_KR = NA_R + WIN_H - 1
NA_TK = NA_KR * GRID_W
NA_NB = ROWS // NA_R

MOE_TM = 768
MOE_SUB = 128
MOE_SUB_LOG2 = 7
MOE_TF = 512
MOE_NF = D_EXPERT // MOE_TF
META_TILE_EXPERT = 0
META_TILE_ROWS = 1
META_N_USED = 2
META_LANES = 128
POS_CHUNK = 256

ROW_TILE = 256


def _cparams(semantics, vmem_mib):
    return pltpu.CompilerParams(dimension_semantics=semantics,
                                vmem_limit_bytes=vmem_mib * 1024 * 1024)


def _layer_norm_rows(x):
    mu = jnp.mean(x, axis=-1, keepdims=True)
    xc = x - mu
    var = jnp.mean(xc * xc, axis=-1, keepdims=True)
    return xc * lax.rsqrt(var + LN_EPS)


def _dot(a, b):
    return jnp.dot(a, b, preferred_element_type=F32)


def _dot_nt(a, b):
    return lax.dot_general(a, b, (((1,), (1,)), ((), ())), preferred_element_type=F32)


ADA_TN = 1024


def _ada_kernel(c_ref, w_ref, b_ref, o_ref):
    cv = c_ref[...]
    act = (cv * jax.nn.sigmoid(cv)).astype(BF16)
    o_ref[...] = _dot(act, w_ref[...].astype(BF16)) + b_ref[...]


def _ada(cvec, ada_w, ada_b):
    n_out = 6 * D_MODEL
    return pl.pallas_call(
        _ada_kernel,
        grid=(DEPTH, n_out // ADA_TN),
        in_specs=[
            pl.BlockSpec((8, D_MODEL), lambda l, j: (0, 0)),
            pl.BlockSpec((None, D_MODEL, ADA_TN), lambda l, j: (l, 0, j)),
            pl.BlockSpec((None, 1, ADA_TN), lambda l, j: (l, 0, j)),
        ],
        out_specs=pl.BlockSpec((None, 8, ADA_TN), lambda l, j: (l, 0, j)),
        out_shape=jax.ShapeDtypeStruct((DEPTH, 8, n_out), F32),
        compiler_params=_cparams(("arbitrary", "arbitrary"), 40),
        name="ada_mod",
    )(cvec, ada_w, ada_b.reshape(DEPTH, 1, n_out))


def _rope_partner(y):
    lane = lax.broadcasted_iota(I32, y.shape, 1)
    first = (lane % (2 * ROPE_PAIRS)) < ROPE_PAIRS
    return jnp.where(first, pltpu.roll(y, HEAD_DIM - ROPE_PAIRS, 1), pltpu.roll(y, ROPE_PAIRS, 1))


def _norm_rope(zh, gain, cos, sin):
    ms = jnp.mean(zh * zh, axis=-1, keepdims=True)
    y = zh * lax.rsqrt(ms + LN_EPS) * gain
    return y * cos + _rope_partner(y) * sin


def _proj_in_kernel(x_ref, sh_ref, sc_ref, w_ref, cos_ref, sin_ref, qg_ref, kg_ref,
                    proj_ref, sgz_ref, h_scr):
    j = pl.program_id(1)
    tm = x_ref.shape[0]
    chunks = [slice(r, r + min(PROJ_ROWS, tm)) for r in range(0, tm, PROJ_ROWS)]

    def zed(rows):
        return _dot(h_scr[rows, :], w_ref[...])

    @pl.when(j == 0)
    def _():
        for rows in chunks:
            h = _layer_norm_rows(x_ref[rows, :]) * (1.0 + sc_ref[...]) + sh_ref[...]
            h_scr[rows, :] = h.astype(BF16)
            proj_ref[rows, :] = (zed(rows) * ATTN_SCALE).astype(BF16)

    @pl.when((j == 1) | (j == 2))
    def _():
        for rows in chunks:
            proj_ref[rows, :] = zed(rows).astype(BF16)

    @pl.when((j == 3) | (j == 4))
    def _():
        for rows in chunks:
            z = zed(rows)
            proj_ref[rows, :] = z.astype(BF16)
            sgz_ref[rows, :] = z

    @pl.when((j == 5) | (j == 6))
    def _():
        for rows in chunks:
            z = zed(rows)
            cos = cos_ref[rows, :]
            sin = sin_ref[rows, :]
            for hh in range(PROJ_TN // HEAD_DIM):
                cols = slice(hh * HEAD_DIM, (hh + 1) * HEAD_DIM)
                q = _norm_rope(z[:, cols], qg_ref[...], cos, sin)
                proj_ref[rows, cols] = (q * (ATTN_SCALE * LOG2E)).astype(BF16)

    @pl.when(j == 7)
    def _():
        for rows in chunks:
            z = zed(rows)
            cos = cos_ref[rows, :]
            sin = sin_ref[rows, :]
            for hh in range(GQA_KV_HEADS):
                cols = slice(hh * HEAD_DIM, (hh + 1) * HEAD_DIM)
                proj_ref[rows, cols] = _norm_rope(z[:, cols], kg_ref[...], cos, sin).astype(BF16)
            proj_ref[rows, KV_WIDTH:] = z[:, KV_WIDTH:].astype(BF16)


def _proj_in(x, sh, sc, w_bf16, cos, sin, q_gain, k_gain, tm):
    n = x.shape[0]
    return pl.pallas_call(
        _proj_in_kernel,
        grid=(n // tm, PROJ_NJ),
        in_specs=[
            pl.BlockSpec((tm, D_MODEL), lambda i, j: (i, 0)),
            pl.BlockSpec((1, D_MODEL), lambda i, j: (0, 0)),
            pl.BlockSpec((1, D_MODEL), lambda i, j: (0, 0)),
            pl.BlockSpec((D_MODEL, PROJ_TN), lambda i, j: (0, j)),
            pl.BlockSpec((tm, HEAD_DIM), lambda i, j: (i, 0)),
            pl.BlockSpec((tm, HEAD_DIM), lambda i, j: (i, 0)),
            pl.BlockSpec((1, HEAD_DIM), lambda i, j: (0, 0)),
            pl.BlockSpec((1, HEAD_DIM), lambda i, j: (0, 0)),
        ],
        out_specs=[
            pl.BlockSpec((tm, PROJ_TN), lambda i, j: (i, j)),
            pl.BlockSpec((tm, PROJ_TN), lambda i, j: (i, jnp.clip(j - 3, 0, 1))),
        ],
        out_shape=[
            jax.ShapeDtypeStruct((n, IN_WIDTH), BF16),
            jax.ShapeDtypeStruct((n, 2 * SG_WIDTH), F32),
        ],
        scratch_shapes=[pltpu.VMEM((tm, D_MODEL), BF16)],
        compiler_params=_cparams(("arbitrary", "arbitrary"), 48),
        name="proj_in",
    )(x, sh, sc, w_bf16, cos, sin, q_gain, k_gain)


def _na_kernel(q_ref, k_ref, v_ref, kc_ref, vc_ref, bias_ref, o_ref):
    b = pl.program_id(0)
    key_row0 = jnp.clip(b * NA_R - WIN_H // 2, 0, ROWS - NA_KR)
    start = pl.multiple_of(key_row0 * GRID_W, GRID_W)
    for h in range(NA_HEADS):
        cols = slice(h * HEAD_DIM, (h + 1) * HEAD_DIM)
        q = q_ref[:, cols]
        kw = k_ref[pl.ds(start, NA_TK), cols]
        vw = v_ref[pl.ds(start, NA_TK), cols]
        s_loc = _dot_nt(q, kw) + bias_ref[h]
        s_ctx = _dot_nt(q, kc_ref[:, cols])
        m = jnp.maximum(jnp.max(s_loc, axis=-1, keepdims=True), jnp.max(s_ctx, axis=-1, keepdims=True))
        p_loc = jnp.exp(s_loc - m)
        p_ctx = jnp.exp(s_ctx - m)
        denom = jnp.sum(p_loc, axis=-1, keepdims=True) + jnp.sum(p_ctx, axis=-1, keepdims=True)
        o = _dot(p_loc.astype(BF16), vw) + _dot(p_ctx.astype(BF16), vc_ref[:, cols])
        o_ref[:, cols] = (o / denom).astype(BF16)


def _na_bias_tables(rpb):
    tables = []
    for r0, k0 in ((0, 0), (2 * NA_R, 2 * NA_R - WIN_H // 2), (ROWS - NA_R, ROWS - NA_KR)):
        r = r0 + np.arange(NA_R)
        rs = np.clip(r - WIN_H // 2, 0, ROWS - WIN_H)
        kr = k0 + np.arange(NA_KR)
        ok_r = (kr[None, :] >= rs[:, None]) & (kr[None, :] < rs[:, None] + WIN_H)
        drow = kr[None, :] - r[:, None] + (WIN_H - 1)
        c = np.arange(GRID_W)
        cs = np.clip(c - WIN_W // 2, 0, GRID_W - WIN_W)
        kc = np.arange(GRID_W)
        ok_c = (kc[None, :] >= cs[:, None]) & (kc[None, :] < cs[:, None] + WIN_W)
        dcol = kc[None, :] - c[:, None] + (WIN_W - 1)
        ok = ok_r[:, None, :, None] & ok_c[None, :, None, :]
        pick_r = (np.clip(drow, 0, 2 * WIN_H - 2)[..., None] == np.arange(2 * WIN_H - 1)).astype(np.float32)
        pick_c = (np.clip(dcol, 0, 2 * WIN_W - 2)[..., None] == np.arange(2 * WIN_W - 1)).astype(np.float32)
        bias = jnp.einsum('rka,hab,cwb->hrckw', pick_r, rpb.astype(F32), pick_c,
                          precision=lax.Precision.HIGHEST)
        tables.append(jnp.where(ok[None], bias, NEG_BIG).reshape(NA_HEADS, NA_TQ, NA_TK))
    return jnp.stack(tables)


def _na_attention(proj, proj_ctx, bias_tab):
    def bias_idx(b):
        return (jnp.where(b == 0, 0, jnp.where(b == NA_NB - 1, 2, 1)), 0, 0, 0)

    return pl.pallas_call(
        _na_kernel,
        grid=(NA_NB,),
        in_specs=[
            pl.BlockSpec((NA_TQ, NA_WIDTH), lambda b: (b, 0)),
            pl.BlockSpec((SEQ, NA_WIDTH), lambda b: (0, 1)),
            pl.BlockSpec((SEQ, NA_WIDTH), lambda b: (0, 2)),
            pl.BlockSpec((CTX_LEN, NA_WIDTH), lambda b: (0, 1)),
            pl.BlockSpec((CTX_LEN, NA_WIDTH), lambda b: (0, 2)),
            pl.BlockSpec((None, NA_HEADS, NA_TQ, NA_TK), bias_idx),
        ],
        out_specs=pl.BlockSpec((NA_TQ, NA_WIDTH), lambda b: (b, 0)),
        out_shape=jax.ShapeDtypeStruct((SEQ, NA_WIDTH), BF16),
        compiler_params=_cparams(("arbitrary",), 48),
        name="na_attn",
    )(proj, proj, proj, proj_ctx, proj_ctx, bias_tab)


def _ctx_attn_kernel(q_ref, k_ref, v_ref, o_ref, *, log2_domain):
    s = _dot_nt(q_ref[...], k_ref[...])
    m = jnp.max(s, axis=-1, keepdims=True)
    p = jnp.exp2(s - m) if log2_domain else jnp.exp(s - m)
    denom = jnp.sum(p, axis=-1, keepdims=True)
    o_ref[...] = (_dot(p.astype(BF16), v_ref[...]) / denom).astype(BF16)


def _ctx_attention(proj_ctx, n_heads, q_col, k_col, v_col, q_per_kv, log2_domain):
    return pl.pallas_call(
        functools.partial(_ctx_attn_kernel, log2_domain=log2_domain),
        grid=(n_heads,),
        in_specs=[
            pl.BlockSpec((CTX_LEN, HEAD_DIM), lambda h: (0, q_col + h)),
            pl.BlockSpec((CTX_LEN, HEAD_DIM), lambda h: (0, k_col + h // q_per_kv)),
            pl.BlockSpec((CTX_LEN, HEAD_DIM), lambda h: (0, v_col + h // q_per_kv)),
        ],
        out_specs=pl.BlockSpec((CTX_LEN, HEAD_DIM), lambda h: (0, h)),
        out_shape=jax.ShapeDtypeStruct((CTX_LEN, n_heads * HEAD_DIM), BF16),
        compiler_params=_cparams(("arbitrary",), 16),
        name="ctx_attn",
    )(proj_ctx, proj_ctx, proj_ctx)


SG_ROWS = 256


def _sg_kernel(zu_ref, zv_ref, g_ref, b_ref, w_ref, bs_ref, o_ref):
    for c in range(SG_ROWS // SG_CHUNK):
        rows = slice(c * SG_CHUNK, (c + 1) * SG_CHUNK)
        for g in range(SG_GROUPS):
            cols = slice(g * HEAD_DIM, (g + 1) * HEAD_DIM)
            v = jax.nn.gelu(zv_ref[rows, cols])
            vn = _layer_norm_rows(v) * g_ref[:, cols] + b_ref[:, cols]
            t = _dot(w_ref[g], vn.astype(BF16)) + bs_ref[:, g:g + 1]
            o_ref[rows, cols] = (jax.nn.gelu(zu_ref[rows, cols]) * t).astype(BF16)


def _spatial_gating(sgz, ln_g, ln_b, w_bf16, b_s):
    n = sgz.shape[0]
    return pl.pallas_call(
        _sg_kernel,
        grid=(n // SG_ROWS,),
        in_specs=[
            pl.BlockSpec((SG_ROWS, SG_WIDTH), lambda i: (i, 0)),
            pl.BlockSpec((SG_ROWS, SG_WIDTH), lambda i: (i, 1)),
            pl.BlockSpec((1, SG_WIDTH), lambda i: (0, 0)),
            pl.BlockSpec((1, SG_WIDTH), lambda i: (0, 0)),
            pl.BlockSpec((SG_GROUPS, SG_CHUNK, SG_CHUNK), lambda i: (0, 0, 0)),
            pl.BlockSpec((SG_CHUNK, SG_GROUPS), lambda i: (0, 0)),
        ],
        out_specs=pl.BlockSpec((SG_ROWS, SG_WIDTH), lambda i: (i, 0)),
        out_shape=jax.ShapeDtypeStruct((n, SG_WIDTH), BF16),
        compiler_params=_cparams(("arbitrary",), 16),
        name="spatial_gating",
    )(sgz, sgz, ln_g.reshape(1, SG_WIDTH), ln_b.reshape(1, SG_WIDTH), w_bf16, b_s.T)


GQA_TQ = 512
GQA_TK = 512


def _gqa_kernel(qt_ref, k_ref, vt_ref, ot_ref, slots_ref, sa_ref, sb_ref, zero_ref, zero_sem):
    qt = qt_ref[...]
    n_chunks = SEQ // GQA_TK

    step = pl.program_id(0) * pl.num_programs(1) + pl.program_id(1)
    n_steps = pl.num_programs(0) * pl.num_programs(1)
    n_zero_blocks = slots_ref.shape[0] // MOE_TM

    def zero_copy(c):
        off = pl.multiple_of(c * MOE_TM, MOE_TM)
        return pltpu.make_async_copy(zero_ref, slots_ref.at[pl.ds(off, MOE_TM), :], zero_sem)

    @pl.when(step == 0)
    def _():
        zero_ref[...] = jnp.zeros_like(zero_ref)

    @pl.when(step < n_zero_blocks)
    def _():
        zero_copy(step).start()

    def scores(t, dst_ref):
        off = pl.multiple_of(t * GQA_TK, GQA_TK)
        s = _dot(k_ref[pl.ds(off, GQA_TK), :], qt)
        dst_ref[...] = s
        return jnp.max(s, axis=0, keepdims=True)

    def consume(t, src_ref, m_chunk, m, l, acc):
        off = pl.multiple_of(t * GQA_TK, GQA_TK)
        m_new = jnp.maximum(m, m_chunk)
        alpha = jnp.exp2(m - m_new)
        p = jnp.exp2(src_ref[...] - m_new)
        l = alpha * l + jnp.sum(p, axis=0, keepdims=True)
        acc = alpha * acc + _dot(vt_ref[:, pl.ds(off, GQA_TK)], p.astype(BF16))
        return m_new, l, acc

    s = _dot(k_ref[pl.ds(SEQ, CTX_LEN), :], qt)
    m = jnp.max(s, axis=0, keepdims=True)
    p = jnp.exp2(s - m)
    l = jnp.sum(p, axis=0, keepdims=True)
    acc = _dot(vt_ref[:, pl.ds(SEQ, CTX_LEN)], p.astype(BF16))
    mc_a = scores(0, sa_ref)

    def body(u, carry):
        m, l, acc, mc_a = carry
        mc_b = scores(2 * u + 1, sb_ref)
        m, l, acc = consume(2 * u, sa_ref, mc_a, m, l, acc)
        mc_a = scores(2 * u + 2, sa_ref)
        m, l, acc = consume(2 * u + 1, sb_ref, mc_b, m, l, acc)
        return m, l, acc, mc_a

    m, l, acc, mc_a = lax.fori_loop(0, n_chunks // 2 - 1, body, (m, l, acc, mc_a))
    mc_b = scores(n_chunks - 1, sb_ref)
    m, l, acc = consume(n_chunks - 2, sa_ref, mc_a, m, l, acc)
    m, l, acc = consume(n_chunks - 1, sb_ref, mc_b, m, l, acc)
    ot_ref[...] = (acc / l).astype(BF16)

    @pl.when(step == n_steps - 1)
    def _():
        def wait(c, carry):
            zero_copy(c).wait()
            return carry
        lax.fori_loop(0, n_zero_blocks, wait, 0)


def _gqa_attention(q_t, k_all, v_t, n_slot_rows):
    grp = GQA_Q_HEADS // GQA_KV_HEADS
    grid = (GQA_Q_HEADS, SEQ // GQA_TQ)
    assert n_slot_rows % MOE_TM == 0 and n_slot_rows // MOE_TM <= grid[0] * grid[1]
    return pl.pallas_call(
        _gqa_kernel,
        grid=grid,
        in_specs=[
            pl.BlockSpec((HEAD_DIM, GQA_TQ), lambda h, i: (h, i)),
            pl.BlockSpec((KEYS, HEAD_DIM), lambda h, i: (0, h // grp)),
            pl.BlockSpec((HEAD_DIM, KEYS), lambda h, i: (h // grp, 0)),
        ],
        out_specs=[pl.BlockSpec((HEAD_DIM, GQA_TQ), lambda h, i: (h, i)),
                   pl.BlockSpec(memory_space=pl.ANY)],
        out_shape=[jax.ShapeDtypeStruct((GQA_WIDTH, SEQ), BF16),
                   jax.ShapeDtypeStruct((n_slot_rows, D_MODEL), F32)],
        scratch_shapes=[pltpu.VMEM((GQA_TK, GQA_TQ), F32), pltpu.VMEM((GQA_TK, GQA_TQ), F32),
                        pltpu.VMEM((MOE_TM, D_MODEL), F32), pltpu.SemaphoreType.DMA(())],
        compiler_params=_cparams(("arbitrary", "arbitrary"), 32),
        name="gqa_attn",
    )(q_t, k_all, v_t)


def _proj_out_kernel(ona_ref, osg_ref, ogq_ref, w_ref, x_ref, g1_ref, lng_ref, lnb_ref,
                     sh2_ref, sc2_ref, xmid_ref, h2_ref):
    y = (_dot(ona_ref[...], w_ref[0:NA_WIDTH, :])
         + _dot(osg_ref[...], w_ref[NA_WIDTH:NA_WIDTH + SG_WIDTH, :])
         + _dot(ogq_ref[...], w_ref[NA_WIDTH + SG_WIDTH:, :]))
    z = ALPHA * x_ref[...] + g1_ref[...] * y
    xm = _layer_norm_rows(z) * lng_ref[...] + lnb_ref[...]
    xmid_ref[...] = xm
    h2_ref[...] = _layer_norm_rows(xm) * (1.0 + sc2_ref[...]) + sh2_ref[...]


def _proj_out(o_na, o_sg, o_gq, w_bf16, x, g1, ln_g, ln_b, sh2, sc2):
    n = x.shape[0]
    tm = ROW_TILE
    vec = pl.BlockSpec((1, D_MODEL), lambda i: (0, 0))
    return pl.pallas_call(
        _proj_out_kernel,
        grid=(n // tm,),
        in_specs=[
            pl.BlockSpec((tm, NA_WIDTH), lambda i: (i, 0)),
            pl.BlockSpec((tm, SG_WIDTH), lambda i: (i, 0)),
            pl.BlockSpec((tm, GQA_WIDTH), lambda i: (i, 0)),
            pl.BlockSpec((MIX_WIDTH, D_MODEL), lambda i: (0, 0)),
            pl.BlockSpec((tm, D_MODEL), lambda i: (i, 0)),
            vec, vec, vec, vec, vec,
        ],
        out_specs=[pl.BlockSpec((tm, D_MODEL), lambda i: (i, 0)),
                   pl.BlockSpec((tm, D_MODEL), lambda i: (i, 0))],
        out_shape=[jax.ShapeDtypeStruct((n, D_MODEL), F32),
                   jax.ShapeDtypeStruct((n, D_MODEL), F32)],
        compiler_params=_cparams(("arbitrary",), 48),
        name="proj_out",
    )(o_na, o_sg, o_gq, w_bf16, x, g1, ln_g, ln_b, sh2, sc2)


def _router_kernel(h_ref, rwt_ref, rb_ref, e_ref, g_ref):
    tm = h_ref.shape[0]
    logits = _dot_nt(rwt_ref[...], h_ref[...].astype(BF16))
    scores = jax.nn.sigmoid(logits)
    sel = scores + rb_ref[...]
    sub = lax.broadcasted_iota(I32, (EXPERTS_PER_GROUP, tm), 0)
    best = None
    for g in range(N_GROUPS):
        rows = slice(g * EXPERTS_PER_GROUP, (g + 1) * EXPERTS_PER_GROUP)
        v = sel[rows, :]
        sc = scores[rows, :]
        m1 = jnp.max(v, axis=0, keepdims=True)
        i1 = jnp.min(jnp.where(v == m1, sub, EXPERTS_PER_GROUP), axis=0, keepdims=True)
        v2 = jnp.where(sub == i1, -jnp.inf, v)
        m2 = jnp.max(v2, axis=0, keepdims=True)
        i2 = jnp.min(jnp.where(v2 == m2, sub, EXPERTS_PER_GROUP), axis=0, keepdims=True)
        s1 = jnp.sum(jnp.where(sub == i1, sc, 0.0), axis=0, keepdims=True)
        s2 = jnp.sum(jnp.where(sub == i2, sc, 0.0), axis=0, keepdims=True)
        cand = (m1 + m2, i1 + g * EXPERTS_PER_GROUP, i2 + g * EXPERTS_PER_GROUP, s1, s2)
        if best is None:
            best = cand
        else:
            better = cand[0] > best[0]
            best = tuple(jnp.where(better, cn, bs) for cn, bs in zip(cand, best))
    _, e1, e2, s1, s2 = best
    tot = s1 + s2
    e_ref[...] = jnp.concatenate([e1, e2], axis=0)
    g_ref[...] = jnp.concatenate([s1 / tot, s2 / tot], axis=0)


def _router(h2, rw_t_bf16, rb_col):
    n = h2.shape[0]
    tm = ROW_TILE
    return pl.pallas_call(
        _router_kernel,
        grid=(n // tm,),
        in_specs=[
            pl.BlockSpec((tm, D_MODEL), lambda i: (i, 0)),
            pl.BlockSpec((N_EXPERTS, D_MODEL), lambda i: (0, 0)),
            pl.BlockSpec((N_EXPERTS, 1), lambda i: (0, 0)),
        ],
        out_specs=[pl.BlockSpec((TOP_K, tm), lambda i: (0, i)),
                   pl.BlockSpec((TOP_K, tm), lambda i: (0, i))],
        out_shape=[jax.ShapeDtypeStruct((TOP_K, n), I32),
                   jax.ShapeDtypeStruct((TOP_K, n), F32)],
        compiler_params=_cparams(("arbitrary",), 16),
        name="router",
    )(h2, rw_t_bf16, rb_col)


def _num_tiles(n_tok):
    return (n_tok * TOP_K + N_EXPERTS * (MOE_TM - 1) + MOE_TM - 1) // MOE_TM


def _positions_kernel(e_ref, dest_ref, meta_ref, rank_scr, *, n_tok):
    ch = POS_CHUNK
    n_chunks = n_tok // ch
    iota_e = lax.broadcasted_iota(I32, (N_EXPERTS, ch), 0)
    upper = (lax.broadcasted_iota(I32, (ch, ch), 0) <= lax.broadcasted_iota(I32, (ch, ch), 1)).astype(BF16)

    def one_hots(off):
        e0 = e_ref[0:1, pl.ds(off, ch)]
        e1 = e_ref[1:2, pl.ds(off, ch)]
        return (iota_e == e0).astype(F32), (iota_e == e1).astype(F32)

    def rank_body(c, carry):
        off = pl.multiple_of(c * ch, ch)
        oh0, oh1 = one_hots(off)
        both = oh0 + oh1
        incl = _dot(both.astype(BF16), upper)
        before = incl - both + carry
        rank_scr[0:1, pl.ds(off, ch)] = jnp.sum(oh0 * before, axis=0, keepdims=True)
        rank_scr[1:2, pl.ds(off, ch)] = jnp.sum(oh1 * (before + oh0), axis=0, keepdims=True)
        return carry + jnp.sum(both, axis=1, keepdims=True)

    counts = lax.fori_loop(0, n_chunks, rank_body, jnp.zeros((N_EXPERTS, 1), F32))
    tiles = jnp.zeros((N_EXPERTS, 1), F32)
    for k in range(n_tok * TOP_K // MOE_TM + 1):
        tiles = tiles + (counts > float(k * MOE_TM)).astype(F32)
    tiles_b = jnp.broadcast_to(tiles, (N_EXPERTS, META_LANES)).astype(BF16)
    lower = (lax.broadcasted_iota(I32, (N_EXPERTS, N_EXPERTS), 1)
             < lax.broadcasted_iota(I32, (N_EXPERTS, N_EXPERTS), 0)).astype(BF16)
    tile_start = _dot(lower, tiles_b)
    tile_end = tile_start + tiles
    pad_start = tile_start[:, 0:1] * float(MOE_TM)

    def dest_body(c, carry):
        off = pl.multiple_of(c * ch, ch)
        oh0, oh1 = one_hots(off)
        d0 = rank_scr[0:1, pl.ds(off, ch)] + jnp.sum(oh0 * pad_start, axis=0, keepdims=True)
        d1 = rank_scr[1:2, pl.ds(off, ch)] + jnp.sum(oh1 * pad_start, axis=0, keepdims=True)
        dest_ref[0:1, pl.ds(off, ch)] = d0.astype(I32)
        dest_ref[1:2, pl.ds(off, ch)] = d1.astype(I32)
        return carry

    lax.fori_loop(0, n_chunks, dest_body, 0)

    tile_id = lax.broadcasted_iota(I32, (N_EXPERTS, META_LANES), 1).astype(F32)
    tile_expert = jnp.sum((tile_end <= tile_id).astype(F32), axis=0, keepdims=True)
    tile_expert = jnp.minimum(tile_expert, float(N_EXPERTS - 1))
    inside = (tile_start <= tile_id) & (tile_id < tile_end)
    rows_here = jnp.clip(counts - float(MOE_TM) * (tile_id - tile_start), 0.0, float(MOE_TM))
    tile_rows = jnp.sum(jnp.where(inside, rows_here, 0.0), axis=0, keepdims=True)
    n_used = jnp.sum(tiles, axis=0, keepdims=True)
    row = lax.broadcasted_iota(I32, (8, META_LANES), 0)
    shape = (8, META_LANES)
    meta = jnp.where(row == META_TILE_EXPERT, jnp.broadcast_to(tile_expert, shape),
                     jnp.where(row == META_TILE_ROWS, jnp.broadcast_to(tile_rows, shape),
                               jnp.where(row == META_N_USED, jnp.broadcast_to(n_used, shape), 0.0)))
    meta_ref[...] = meta.astype(I32)


def _positions(e_all):
    n_tok = e_all.shape[1]
    return pl.pallas_call(
        functools.partial(_positions_kernel, n_tok=n_tok),
        out_shape=[jax.ShapeDtypeStruct((TOP_K, n_tok), I32),
                   jax.ShapeDtypeStruct((8, 128), I32)],
        scratch_shapes=[pltpu.VMEM((TOP_K, n_tok), F32)],
        compiler_params=pltpu.CompilerParams(vmem_limit_bytes=16 * 1024 * 1024),
        name="moe_positions",
    )(e_all)


def _dispatch_kernel(dest_ref, h_ref, xs_in_ref, xs_ref, sem, *, n_tok):
    del xs_in_ref
    tm = h_ref.shape[0]
    base = pl.program_id(0) * tm

    def row_copy(r, k):
        d = dest_ref[k * n_tok + base + r]
        return pltpu.make_async_copy(h_ref.at[pl.ds(r, 1), :], xs_ref.at[pl.ds(d, 1), :], sem)

    def start(r, carry):
        for k in range(TOP_K):
            row_copy(r, k).start()
        return carry

    def wait(r, carry):
        for k in range(TOP_K):
            row_copy(r, k).wait()
        return carry

    lax.fori_loop(0, tm, start, 0)
    lax.fori_loop(0, tm, wait, 0)


def _dispatch(dest_flat, h2, xs):
    n = h2.shape[0]
    tm = ROW_TILE
    return pl.pallas_call(
        functools.partial(_dispatch_kernel, n_tok=n),
        grid_spec=pltpu.PrefetchScalarGridSpec(
            num_scalar_prefetch=1,
            grid=(n // tm,),
            in_specs=[pl.BlockSpec((tm, D_MODEL), lambda i, d: (i, 0)),
                      pl.BlockSpec(memory_space=pl.ANY)],
            out_specs=pl.BlockSpec(memory_space=pl.ANY),
            scratch_shapes=[pltpu.SemaphoreType.DMA(())],
        ),
        out_shape=jax.ShapeDtypeStruct(xs.shape, xs.dtype),
        input_output_aliases={2: 0},
        compiler_params=_cparams(("arbitrary",), 16),
        name="moe_dispatch",
    )(dest_flat, h2, xs)


def _moe_kernel(meta_ref, x_ref, wg_ref, wu_ref, wd_ref, y_ref, xb_scr):
    i = pl.program_id(0)
    f = pl.program_id(1)
    used = i < meta_ref[META_N_USED * META_LANES]
    rows = meta_ref[META_TILE_ROWS * META_LANES + i]
    n_sub = (rows + (MOE_SUB - 1)) >> MOE_SUB_LOG2

    @pl.when(f == 0)
    def _():
        y_ref[...] = jnp.zeros_like(y_ref)

    for occupied in range(1, MOE_TM // MOE_SUB + 1):
        m_rows = occupied * MOE_SUB

        @pl.when(used & (n_sub == occupied))
        def _(m_rows=m_rows):
            @pl.when(f == 0)
            def _():
                xb_scr[0:m_rows, :] = x_ref[0:m_rows, :].astype(BF16)

            xb = xb_scr[0:m_rows, :]
            gate = _dot(xb, wg_ref[...].astype(BF16))
            up = _dot(xb, wu_ref[...].astype(BF16))
            hidden = (gate * jax.nn.sigmoid(gate) * up).astype(BF16)
            y_ref[0:m_rows, :] += _dot(hidden, wd_ref[...].astype(BF16))


def _moe_ffn(meta_flat, xs, w_gate, w_up, w_down, layer, n_tiles):
    n_used_at = META_N_USED * META_LANES

    def last_used(i, m):
        return jnp.minimum(i, m[n_used_at] - 1)

    def f_eff(i, f, m):
        return jnp.where(i < m[n_used_at], f, MOE_NF - 1)

    return pl.pallas_call(
        _moe_kernel,
        grid_spec=pltpu.PrefetchScalarGridSpec(
            num_scalar_prefetch=1,
            grid=(n_tiles, MOE_NF),
            in_specs=[
                pl.BlockSpec((MOE_TM, D_MODEL), lambda i, f, m: (last_used(i, m), 0)),
                pl.BlockSpec((None, None, D_MODEL, MOE_TF),
                             lambda i, f, m: (layer, m[last_used(i, m)], 0, f_eff(i, f, m))),
                pl.BlockSpec((None, None, D_MODEL, MOE_TF),
                             lambda i, f, m: (layer, m[last_used(i, m)], 0, f_eff(i, f, m))),
                pl.BlockSpec((None, None, MOE_TF, D_MODEL),
                             lambda i, f, m: (layer, m[last_used(i, m)], f_eff(i, f, m), 0)),
            ],
            out_specs=pl.BlockSpec((MOE_TM, D_MODEL), lambda i, f, m: (i, 0)),
            scratch_shapes=[pltpu.VMEM((MOE_TM, D_MODEL), BF16)],
        ),
        out_shape=jax.ShapeDtypeStruct((n_tiles * MOE_TM, D_MODEL), F32),
        compiler_params=_cparams(("arbitrary", "arbitrary"), 56),
        name="moe_ffn",
    )(meta_flat, xs, w_gate, w_up, w_down)


def _combine_kernel(dest_ref, y_ref, gt_ref, x_ref, g2_ref, lng_ref, lnb_ref, o_ref,
                    buf_ref, sem, *, n_tok):
    tm = x_ref.shape[0]
    i = pl.program_id(0)

    def row_copy(tile, slot, r, k):
        d = dest_ref[k * n_tok + tile * tm + r]
        return pltpu.make_async_copy(y_ref.at[pl.ds(d, 1), :],
                                     buf_ref.at[slot, k, pl.ds(r, 1), :], sem.at[slot])

    def start_tile(tile, slot):
        def start(r, carry):
            for k in range(TOP_K):
                row_copy(tile, slot, r, k).start()
            return carry
        lax.fori_loop(0, tm, start, 0)

    def wait_tile(tile, slot):
        def wait(r, carry):
            for k in range(TOP_K):
                row_copy(tile, slot, r, k).wait()
            return carry
        lax.fori_loop(0, tm, wait, 0)

    @pl.when(i == 0)
    def _():
        start_tile(i, 0)

    for slot in range(2):
        @pl.when((i + 1 < pl.num_programs(0)) & ((i + 1) % 2 == slot))
        def _(slot=slot):
            start_tile(i + 1, slot)

    for slot in range(2):
        @pl.when(i % 2 == slot)
        def _(slot=slot):
            wait_tile(i, slot)
            gt = gt_ref[...]
            moe = buf_ref[slot, 0] * gt[:, 0:1] + buf_ref[slot, 1] * gt[:, 1:2]
            z = ALPHA * x_ref[...] + g2_ref[...] * moe
            o_ref[...] = _layer_norm_rows(z) * lng_ref[...] + lnb_ref[...]


def _combine(dest_flat, y, gates_t, x_mid, g2, ln_g, ln_b):
    n = x_mid.shape[0]
    tm = ROW_TILE
    vec = pl.BlockSpec((1, D_MODEL), lambda i, d: (0, 0))
    return pl.pallas_call(
        functools.partial(_combine_kernel, n_tok=n),
        grid_spec=pltpu.PrefetchScalarGridSpec(
            num_scalar_prefetch=1,
            grid=(n // tm,),
            in_specs=[pl.BlockSpec(memory_space=pl.ANY),
                      pl.BlockSpec((tm, TOP_K), lambda i, d: (i, 0)),
                      pl.BlockSpec((tm, D_MODEL), lambda i, d: (i, 0)),
                      vec, vec, vec],
            out_specs=pl.BlockSpec((tm, D_MODEL), lambda i, d: (i, 0)),
            scratch_shapes=[pltpu.VMEM((2, TOP_K, tm, D_MODEL), F32),
                            pltpu.SemaphoreType.DMA((2,))],
        ),
        out_shape=jax.ShapeDtypeStruct((n, D_MODEL), F32),
        compiler_params=_cparams(("arbitrary",), 32),
        name="moe_combine",
    )(dest_flat, y, gates_t, x_mid, g2, ln_g, ln_b)


def _rope_tables():
    t = jnp.arange(SEQ, dtype=jnp.int32)
    row = (t // GRID_W).astype(F32)
    col = (t % GRID_W).astype(F32)
    inv_freq = ROPE_THETA ** (-jnp.arange(ROPE_PAIRS, dtype=F32) / ROPE_PAIRS)
    ang_r = row[:, None] * inv_freq
    ang_c = col[:, None] * inv_freq
    cos = jnp.concatenate([jnp.cos(ang_r), jnp.cos(ang_r), jnp.cos(ang_c), jnp.cos(ang_c)], axis=-1)
    sin = jnp.concatenate([-jnp.sin(ang_r), jnp.sin(ang_r), -jnp.sin(ang_c), jnp.sin(ang_c)], axis=-1)
    return cos, sin


def kernel(x, c, ctx, c_ctx, ada_w, ada_b, w_in, na_rpb, sg_ln_g, sg_ln_b, sg_w, sg_b, q_norm_g, k_norm_g, w_out, ln_mix_g, ln_mix_b, router_w, router_bias, moe_w_gate, moe_w_up, moe_w_down, ln_ffn_g, ln_ffn_b):
    cos, sin = _rope_tables()
    cos_ctx = jnp.ones((CTX_LEN, HEAD_DIM), F32)
    sin_ctx = jnp.zeros((CTX_LEN, HEAD_DIM), F32)

    cvec = jnp.zeros((8, D_MODEL), F32).at[0].set(c[0]).at[1].set(c_ctx)
    mods = _ada(cvec, ada_w, ada_b)

    rw_t = router_w.T.astype(BF16)
    rb_col = router_bias.reshape(N_EXPERTS, 1).astype(F32)

    xl = x[0]
    xc = ctx[0]
    for layer in range(DEPTH):
        last = layer == DEPTH - 1
        sh1, sc1, g1, sh2, sc2, g2 = jnp.split(mods[layer, 0:1], 6, axis=-1)
        csh1, csc1, cg1, csh2, csc2, cg2 = jnp.split(mods[layer, 1:2], 6, axis=-1)
        w_in_b = w_in[layer].astype(BF16)
        w_out_b = w_out[layer].astype(BF16)
        sg_w_b = sg_w[layer].astype(BF16)
        q_gain = q_norm_g[layer].reshape(1, HEAD_DIM)
        k_gain = k_norm_g[layer].reshape(1, HEAD_DIM)
        lng = ln_mix_g[layer].reshape(1, D_MODEL)
        lnb = ln_mix_b[layer].reshape(1, D_MODEL)
        fng = ln_ffn_g[layer].reshape(1, D_MODEL)
        fnb = ln_ffn_b[layer].reshape(1, D_MODEL)

        proj, sgz = _proj_in(xl, sh1, sc1, w_in_b, cos, sin, q_gain, k_gain, tm=1024)
        proj_c, sgz_c = _proj_in(xc, csh1, csc1, w_in_b, cos_ctx, sin_ctx, q_gain, k_gain, tm=CTX_LEN)

        o_na = _na_attention(proj, proj_c, _na_bias_tables(na_rpb[layer]))
        o_sg = _spatial_gating(sgz, sg_ln_g[layer], sg_ln_b[layer], sg_w_b, sg_b[layer])
        q_t = proj[:, COL_GQ * HEAD_DIM:COL_GK * HEAD_DIM].T
        k_all = jnp.concatenate([proj[:, COL_GK * HEAD_DIM:COL_GV * HEAD_DIM],
                                 proj_c[:, COL_GK * HEAD_DIM:COL_GV * HEAD_DIM]], axis=0)
        v_t = jnp.concatenate([proj[:, COL_GV * HEAD_DIM:], proj_c[:, COL_GV * HEAD_DIM:]], axis=0).T
        n_tok = SEQ if last else SEQ + CTX_LEN
        n_tiles = _num_tiles(n_tok)
        o_gq_t, xs = _gqa_attention(q_t, k_all, v_t, n_tiles * MOE_TM)
        x_mid, h2 = _proj_out(o_na, o_sg, o_gq_t.T, w_out_b, xl, g1, lng, lnb, sh2, sc2)

        if last:
            e_all, gates = _router(h2, rw_t, rb_col)
        else:
            c_na = _ctx_attention(proj_c, NA_HEADS, COL_NA_Q, COL_NA_K, COL_NA_V, 1, False)
            c_sg = _spatial_gating(sgz_c, sg_ln_g[layer], sg_ln_b[layer], sg_w_b, sg_b[layer])
            c_gq = _ctx_attention(proj_c, GQA_Q_HEADS, COL_GQ, COL_GK, COL_GV,
                                  GQA_Q_HEADS // GQA_KV_HEADS, True)
            xc_mid, h2_c = _proj_out(c_na, c_sg, c_gq, w_out_b, xc, cg1, lng, lnb, csh2, csc2)
            e_lat, gates_lat = _router(h2, rw_t, rb_col)
            e_ctx, gates_ctx = _router(h2_c, rw_t, rb_col)
            e_all = jnp.concatenate([e_lat, e_ctx], axis=1)
            gates = jnp.concatenate([gates_lat, gates_ctx], axis=1)

        dest, meta = _positions(e_all)
        meta_flat = meta.reshape(-1)
        xs = _dispatch(dest[:, :SEQ].reshape(-1), h2, xs)
        if not last:
            xs = _dispatch(dest[:, SEQ:].reshape(-1), h2_c, xs)
        y = _moe_ffn(meta_flat, xs, moe_w_gate, moe_w_up, moe_w_down, layer, n_tiles)
        gates_t = gates.T
        xl = _combine(dest[:, :SEQ].reshape(-1), y, gates_t[:SEQ], x_mid, g2, fng, fnb)
        if not last:
            xc = _combine(dest[:, SEQ:].reshape(-1), y, gates_t[SEQ:], xc_mid, cg2, fng, fnb)
    return xl[None]
```

```python
import functools

import jax
import jax.numpy as jnp
import numpy as np
from jax import lax
from jax.experimental import pallas as pl
from jax.experimental.pallas import tpu as pltpu

F32 = jnp.float32
BF16 = jnp.bfloat16
I32 = jnp.int32

D_MODEL = 2048
SEQ = 8192
DEPTH = 2
GRID_W = 64
ROWS = SEQ // GRID_W
CTX_LEN = 256
HEAD_DIM = 128

NA_HEADS = 4
WIN_H = 8
WIN_W = 16
SG_GROUPS = 4
SG_CHUNK = 128
GQA_Q_HEADS = 8
GQA_KV_HEADS = 2
ROPE_THETA = 10000.0
ROPE_PAIRS = HEAD_DIM // 4

NA_WIDTH = NA_HEADS * HEAD_DIM
SG_WIDTH = SG_GROUPS * HEAD_DIM
GQA_WIDTH = GQA_Q_HEADS * HEAD_DIM
KV_WIDTH = GQA_KV_HEADS * HEAD_DIM
MIX_WIDTH = NA_WIDTH + SG_WIDTH + GQA_WIDTH
IN_WIDTH = 3 * NA_WIDTH + 2 * SG_WIDTH + GQA_WIDTH + 2 * KV_WIDTH

N_EXPERTS = 32
N_GROUPS = 4
EXPERTS_PER_GROUP = N_EXPERTS // N_GROUPS
TOP_K = 2
D_EXPERT = 1024

ALPHA = (2 * DEPTH) ** 0.25
LN_EPS = 1e-6
ATTN_SCALE = HEAD_DIM ** -0.5
LOG2E = 1.4426950408889634
NEG_BIG = -1e30
KEYS = SEQ + CTX_LEN

COL_NA_Q = 0
COL_NA_K = NA_HEADS
COL_NA_V = 2 * NA_HEADS
COL_SG = 3 * NA_HEADS
COL_GQ = COL_SG + 2 * SG_GROUPS
COL_GK = COL_GQ + GQA_Q_HEADS
COL_GV = COL_GK + GQA_KV_HEADS

PROJ_TN = 512
PROJ_NJ = IN_WIDTH // PROJ_TN
PROJ_ROWS = 256

NA_R = 4
NA_TQ = NA_R * GRID_W
NA_KR = NA_R + WIN_H - 1
NA_TK = NA_KR * GRID_W
NA_NB = ROWS // NA_R

MOE_TM = 768
MOE_SUB = 128
MOE_SUB_LOG2 = 7
MOE_TF = 512
MOE_NF = D_EXPERT // MOE_TF
META_TILE_EXPERT = 0
META_TILE_ROWS = 1
META_N_USED = 2
META_LANES = 128
POS_CHUNK = 256

ROW_TILE = 256


def _cparams(semantics, vmem_mib):
    return pltpu.CompilerParams(dimension_semantics=semantics,
                                vmem_limit_bytes=vmem_mib * 1024 * 1024)


def _layer_norm_rows(x):
    mu = jnp.mean(x, axis=-1, keepdims=True)
    xc = x - mu
    var = jnp.mean(xc * xc, axis=-1, keepdims=True)
    return xc * lax.rsqrt(var + LN_EPS)


def _dot(a, b):
    return jnp.dot(a, b, preferred_element_type=F32)


def _dot_nt(a, b):
    return lax.dot_general(a, b, (((1,), (1,)), ((), ())), preferred_element_type=F32)


ADA_TN = 1024


def _ada_kernel(c_ref, w_ref, b_ref, o_ref):
    cv = c_ref[...]
    act = (cv * jax.nn.sigmoid(cv)).astype(BF16)
    o_ref[...] = _dot(act, w_ref[...].astype(BF16)) + b_ref[...]


def _ada(cvec, ada_w, ada_b):
    n_out = 6 * D_MODEL
    return pl.pallas_call(
        _ada_kernel,
        grid=(DEPTH, n_out // ADA_TN),
        in_specs=[
            pl.BlockSpec((8, D_MODEL), lambda l, j: (0, 0)),
            pl.BlockSpec((None, D_MODEL, ADA_TN), lambda l, j: (l, 0, j)),
            pl.BlockSpec((None, 1, ADA_TN), lambda l, j: (l, 0, j)),
        ],
        out_specs=pl.BlockSpec((None, 8, ADA_TN), lambda l, j: (l, 0, j)),
        out_shape=jax.ShapeDtypeStruct((DEPTH, 8, n_out), F32),
        compiler_params=_cparams(("arbitrary", "arbitrary"), 40),
        name="ada_mod",
    )(cvec, ada_w, ada_b.reshape(DEPTH, 1, n_out))


def _rope_partner(y):
    lane = lax.broadcasted_iota(I32, y.shape, 1)
    first = (lane % (2 * ROPE_PAIRS)) < ROPE_PAIRS
    return jnp.where(first, pltpu.roll(y, HEAD_DIM - ROPE_PAIRS, 1), pltpu.roll(y, ROPE_PAIRS, 1))


def _norm_rope(zh, gain, cos, sin):
    ms = jnp.mean(zh * zh, axis=-1, keepdims=True)
    y = zh * lax.rsqrt(ms + LN_EPS) * gain
    return y * cos + _rope_partner(y) * sin


def _proj_in_kernel(x_ref, sh_ref, sc_ref, w_ref, cos_ref, sin_ref, qg_ref, kg_ref,
                    proj_ref, sgz_ref, h_scr):
    j = pl.program_id(1)
    tm = x_ref.shape[0]
    chunks = [slice(r, r + min(PROJ_ROWS, tm)) for r in range(0, tm, PROJ_ROWS)]

    def zed(rows):
        return _dot(h_scr[rows, :], w_ref[...])

    @pl.when(j == 0)
    def _():
        for rows in chunks:
            h = _layer_norm_rows(x_ref[rows, :]) * (1.0 + sc_ref[...]) + sh_ref[...]
            h_scr[rows, :] = h.astype(BF16)
            proj_ref[rows, :] = (zed(rows) * ATTN_SCALE).astype(BF16)

    @pl.when((j == 1) | (j == 2))
    def _():
        for rows in chunks:
            proj_ref[rows, :] = zed(rows).astype(BF16)

    @pl.when((j == 3) | (j == 4))
    def _():
        for rows in chunks:
            z = zed(rows)
            proj_ref[rows, :] = z.astype(BF16)
            sgz_ref[rows, :] = z

    @pl.when((j == 5) | (j == 6))
    def _():
        for rows in chunks:
            z = zed(rows)
            cos = cos_ref[rows, :]
            sin = sin_ref[rows, :]
            for hh in range(PROJ_TN // HEAD_DIM):
                cols = slice(hh * HEAD_DIM, (hh + 1) * HEAD_DIM)
                q = _norm_rope(z[:, cols], qg_ref[...], cos, sin)
                proj_ref[rows, cols] = (q * (ATTN_SCALE * LOG2E)).astype(BF16)

    @pl.when(j == 7)
    def _():
        for rows in chunks:
            z = zed(rows)
            cos = cos_ref[rows, :]
            sin = sin_ref[rows, :]
            for hh in range(GQA_KV_HEADS):
                cols = slice(hh * HEAD_DIM, (hh + 1) * HEAD_DIM)
                proj_ref[rows, cols] = _norm_rope(z[:, cols], kg_ref[...], cos, sin).astype(BF16)
            proj_ref[rows, KV_WIDTH:] = z[:, KV_WIDTH:].astype(BF16)


def _proj_in(x, sh, sc, w_bf16, cos, sin, q_gain, k_gain, tm):
    n = x.shape[0]
    return pl.pallas_call(
        _proj_in_kernel,
        grid=(n // tm, PROJ_NJ),
        in_specs=[
            pl.BlockSpec((tm, D_MODEL), lambda i, j: (i, 0)),
            pl.BlockSpec((1, D_MODEL), lambda i, j: (0, 0)),
            pl.BlockSpec((1, D_MODEL), lambda i, j: (0, 0)),
            pl.BlockSpec((D_MODEL, PROJ_TN), lambda i, j: (0, j)),
            pl.BlockSpec((tm, HEAD_DIM), lambda i, j: (i, 0)),
            pl.BlockSpec((tm, HEAD_DIM), lambda i, j: (i, 0)),
            pl.BlockSpec((1, HEAD_DIM), lambda i, j: (0, 0)),
            pl.BlockSpec((1, HEAD_DIM), lambda i, j: (0, 0)),
        ],
        out_specs=[
            pl.BlockSpec((tm, PROJ_TN), lambda i, j: (i, j)),
            pl.BlockSpec((tm, PROJ_TN), lambda i, j: (i, jnp.clip(j - 3, 0, 1))),
        ],
        out_shape=[
            jax.ShapeDtypeStruct((n, IN_WIDTH), BF16),
            jax.ShapeDtypeStruct((n, 2 * SG_WIDTH), F32),
        ],
        scratch_shapes=[pltpu.VMEM((tm, D_MODEL), BF16)],
        compiler_params=_cparams(("arbitrary", "arbitrary"), 48),
        name="proj_in",
    )(x, sh, sc, w_bf16, cos, sin, q_gain, k_gain)


def _na_kernel(q_ref, k_ref, v_ref, kc_ref, vc_ref, bias_ref, o_ref):
    b = pl.program_id(0)
    key_row0 = jnp.clip(b * NA_R - WIN_H // 2, 0, ROWS - NA_KR)
    start = pl.multiple_of(key_row0 * GRID_W, GRID_W)
    for h in range(NA_HEADS):
        cols = slice(h * HEAD_DIM, (h + 1) * HEAD_DIM)
        q = q_ref[:, cols]
        kw = k_ref[pl.ds(start, NA_TK), cols]
        vw = v_ref[pl.ds(start, NA_TK), cols]
        s_loc = _dot_nt(q, kw) + bias_ref[h]
        s_ctx = _dot_nt(q, kc_ref[:, cols])
        m = jnp.maximum(jnp.max(s_loc, axis=-1, keepdims=True), jnp.max(s_ctx, axis=-1, keepdims=True))
        p_loc = jnp.exp(s_loc - m)
        p_ctx = jnp.exp(s_ctx - m)
        denom = jnp.sum(p_loc, axis=-1, keepdims=True) + jnp.sum(p_ctx, axis=-1, keepdims=True)
        o = _dot(p_loc.astype(BF16), vw) + _dot(p_ctx.astype(BF16), vc_ref[:, cols])
        o_ref[:, cols] = (o / denom).astype(BF16)


def _na_bias_tables(rpb):
    tables = []
    for r0, k0 in ((0, 0), (2 * NA_R, 2 * NA_R - WIN_H // 2), (ROWS - NA_R, ROWS - NA_KR)):
        r = r0 + np.arange(NA_R)
        rs = np.clip(r - WIN_H // 2, 0, ROWS - WIN_H)
        kr = k0 + np.arange(NA_KR)
        ok_r = (kr[None, :] >= rs[:, None]) & (kr[None, :] < rs[:, None] + WIN_H)
        drow = kr[None, :] - r[:, None] + (WIN_H - 1)
        c = np.arange(GRID_W)
        cs = np.clip(c - WIN_W // 2, 0, GRID_W - WIN_W)
        kc = np.arange(GRID_W)
        ok_c = (kc[None, :] >= cs[:, None]) & (kc[None, :] < cs[:, None] + WIN_W)
        dcol = kc[None, :] - c[:, None] + (WIN_W - 1)
        ok = ok_r[:, None, :, None] & ok_c[None, :, None, :]
        pick_r = (np.clip(drow, 0, 2 * WIN_H - 2)[..., None] == np.arange(2 * WIN_H - 1)).astype(np.float32)
        pick_c = (np.clip(dcol, 0, 2 * WIN_W - 2)[..., None] == np.arange(2 * WIN_W - 1)).astype(np.float32)
        bias = jnp.einsum('rka,lhab,cwb->lhrckw', pick_r, rpb.astype(F32), pick_c,
                          precision=lax.Precision.HIGHEST)
        tables.append(jnp.where(ok[None, None], bias, NEG_BIG).reshape(DEPTH, NA_HEADS, NA_TQ, NA_TK))
    return jnp.stack(tables, axis=1)


def _na_attention(proj, proj_ctx, bias_tab, layer):
    def bias_idx(b):
        return (layer, jnp.where(b == 0, 0, jnp.where(b == NA_NB - 1, 2, 1)), 0, 0, 0)

    return pl.pallas_call(
        _na_kernel,
        grid=(NA_NB,),
        in_specs=[
            pl.BlockSpec((NA_TQ, NA_WIDTH), lambda b: (b, 0)),
            pl.BlockSpec((SEQ, NA_WIDTH), lambda b: (0, 1)),
            pl.BlockSpec((SEQ, NA_WIDTH), lambda b: (0, 2)),
            pl.BlockSpec((CTX_LEN, NA_WIDTH), lambda b: (0, 1)),
            pl.BlockSpec((CTX_LEN, NA_WIDTH), lambda b: (0, 2)),
            pl.BlockSpec((None, None, NA_HEADS, NA_TQ, NA_TK), bias_idx),
        ],
        out_specs=pl.BlockSpec((NA_TQ, NA_WIDTH), lambda b: (b, 0)),
        out_shape=jax.ShapeDtypeStruct((SEQ, NA_WIDTH), BF16),
        compiler_params=_cparams(("arbitrary",), 48),
        name="na_attn",
    )(proj, proj, proj, proj_ctx, proj_ctx, bias_tab)


def _ctx_attn_kernel(q_ref, k_ref, v_ref, o_ref, *, log2_domain):
    s = _dot_nt(q_ref[...], k_ref[...])
    m = jnp.max(s, axis=-1, keepdims=True)
    p = jnp.exp2(s - m) if log2_domain else jnp.exp(s - m)
    denom = jnp.sum(p, axis=-1, keepdims=True)
    o_ref[...] = (_dot(p.astype(BF16), v_ref[...]) / denom).astype(BF16)


def _ctx_attention(proj_ctx, n_heads, q_col, k_col, v_col, q_per_kv, log2_domain):
    return pl.pallas_call(
        functools.partial(_ctx_attn_kernel, log2_domain=log2_domain),
        grid=(n_heads,),
        in_specs=[
            pl.BlockSpec((CTX_LEN, HEAD_DIM), lambda h: (0, q_col + h)),
            pl.BlockSpec((CTX_LEN, HEAD_DIM), lambda h: (0, k_col + h // q_per_kv)),
            pl.BlockSpec((CTX_LEN, HEAD_DIM), lambda h: (0, v_col + h // q_per_kv)),
        ],
        out_specs=pl.BlockSpec((CTX_LEN, HEAD_DIM), lambda h: (0, h)),
        out_shape=jax.ShapeDtypeStruct((CTX_LEN, n_heads * HEAD_DIM), BF16),
        compiler_params=_cparams(("arbitrary",), 16),
        name="ctx_attn",
    )(proj_ctx, proj_ctx, proj_ctx)


SG_ROWS = 256


def _sg_kernel(zu_ref, zv_ref, g_ref, b_ref, w_ref, bs_ref, o_ref):
    for c in range(SG_ROWS // SG_CHUNK):
        rows = slice(c * SG_CHUNK, (c + 1) * SG_CHUNK)
        for g in range(SG_GROUPS):
            cols = slice(g * HEAD_DIM, (g + 1) * HEAD_DIM)
            v = jax.nn.gelu(zv_ref[rows, cols])
            vn = _layer_norm_rows(v) * g_ref[:, cols] + b_ref[:, cols]
            t = _dot(w_ref[g], vn.astype(BF16)) + bs_ref[:, g:g + 1]
            o_ref[rows, cols] = (jax.nn.gelu(zu_ref[rows, cols]) * t).astype(BF16)


def _spatial_gating(sgz, ln_g, ln_b, w_bf16, b_s):
    n = sgz.shape[0]
    return pl.pallas_call(
        _sg_kernel,
        grid=(n // SG_ROWS,),
        in_specs=[
            pl.BlockSpec((SG_ROWS, SG_WIDTH), lambda i: (i, 0)),
            pl.BlockSpec((SG_ROWS, SG_WIDTH), lambda i: (i, 1)),
            pl.BlockSpec((1, SG_WIDTH), lambda i: (0, 0)),
            pl.BlockSpec((1, SG_WIDTH), lambda i: (0, 0)),
            pl.BlockSpec((SG_GROUPS, SG_CHUNK, SG_CHUNK), lambda i: (0, 0, 0)),
            pl.BlockSpec((SG_CHUNK, SG_GROUPS), lambda i: (0, 0)),
        ],
        out_specs=pl.BlockSpec((SG_ROWS, SG_WIDTH), lambda i: (i, 0)),
        out_shape=jax.ShapeDtypeStruct((n, SG_WIDTH), BF16),
        compiler_params=_cparams(("arbitrary",), 16),
        name="spatial_gating",
    )(sgz, sgz, ln_g.reshape(1, SG_WIDTH), ln_b.reshape(1, SG_WIDTH), w_bf16, b_s.T)


GQA_TQ = 512
GQA_TK = 512


def _gqa_kernel(qt_ref, k_ref, vt_ref, ot_ref, slots_x_ref, slots_y_ref, sa_ref, sb_ref, zero_ref, zero_sem):
    qt = qt_ref[...]
    n_chunks = SEQ // GQA_TK

    step = pl.program_id(0) * pl.num_programs(1) + pl.program_id(1)
    n_steps = pl.num_programs(0) * pl.num_programs(1)
    n_zero_blocks = slots_x_ref.shape[0] // MOE_TM

    def zero_copy(dst_ref, c):
        off = pl.multiple_of(c * MOE_TM, MOE_TM)
        return pltpu.make_async_copy(zero_ref, dst_ref.at[pl.ds(off, MOE_TM), :], zero_sem)

    @pl.when(step == 0)
    def _():
        zero_ref[...] = jnp.zeros_like(zero_ref)

    @pl.when(step < n_zero_blocks)
    def _():
        zero_copy(slots_x_ref, step).start()

    @pl.when((step >= n_zero_blocks) & (step < 2 * n_zero_blocks))
    def _():
        zero_copy(slots_y_ref, step - n_zero_blocks).start()

    def scores(t, dst_ref):
        off = pl.multiple_of(t * GQA_TK, GQA_TK)
        s = _dot(k_ref[pl.ds(off, GQA_TK), :], qt)
        dst_ref[...] = s
        return jnp.max(s, axis=0, keepdims=True)

    def consume(t, src_ref, m_chunk, m, l, acc):
        off = pl.multiple_of(t * GQA_TK, GQA_TK)
        m_new = jnp.maximum(m, m_chunk)
        alpha = jnp.exp2(m - m_new)
        p = jnp.exp2(src_ref[...] - m_new)
        l = alpha * l + jnp.sum(p, axis=0, keepdims=True)
        acc = alpha * acc + _dot(vt_ref[:, pl.ds(off, GQA_TK)], p.astype(BF16))
        return m_new, l, acc

    s = _dot(k_ref[pl.ds(SEQ, CTX_LEN), :], qt)
    m = jnp.max(s, axis=0, keepdims=True)
    p = jnp.exp2(s - m)
    l = jnp.sum(p, axis=0, keepdims=True)
    acc = _dot(vt_ref[:, pl.ds(SEQ, CTX_LEN)], p.astype(BF16))
    mc_a = scores(0, sa_ref)

    def body(u, carry):
        m, l, acc, mc_a = carry
        mc_b = scores(2 * u + 1, sb_ref)
        m, l, acc = consume(2 * u, sa_ref, mc_a, m, l, acc)
        mc_a = scores(2 * u + 2, sa_ref)
        m, l, acc = consume(2 * u + 1, sb_ref, mc_b, m, l, acc)
        return m, l, acc, mc_a

    m, l, acc, mc_a = lax.fori_loop(0, n_chunks // 2 - 1, body, (m, l, acc, mc_a))
    mc_b = scores(n_chunks - 1, sb_ref)
    m, l, acc = consume(n_chunks - 2, sa_ref, mc_a, m, l, acc)
    m, l, acc = consume(n_chunks - 1, sb_ref, mc_b, m, l, acc)
    ot_ref[...] = (acc / l).astype(BF16)

    @pl.when(step == n_steps - 1)
    def _():
        def wait(c, carry):
            zero_copy(slots_x_ref, c).wait()
            zero_copy(slots_y_ref, c).wait()
            return carry
        lax.fori_loop(0, n_zero_blocks, wait, 0)


def _gqa_attention(q_t, k_all, v_t, n_slot_rows):
    grp = GQA_Q_HEADS // GQA_KV_HEADS
    grid = (GQA_Q_HEADS, SEQ // GQA_TQ)
    assert n_slot_rows % MOE_TM == 0 and 2 * (n_slot_rows // MOE_TM) <= grid[0] * grid[1]
    return pl.pallas_call(
        _gqa_kernel,
        grid=grid,
        in_specs=[
            pl.BlockSpec((HEAD_DIM, GQA_TQ), lambda h, i: (h, i)),
            pl.BlockSpec((KEYS, HEAD_DIM), lambda h, i: (0, h // grp)),
            pl.BlockSpec((HEAD_DIM, KEYS), lambda h, i: (h // grp, 0)),
        ],
        out_specs=[pl.BlockSpec((HEAD_DIM, GQA_TQ), lambda h, i: (h, i)),
                   pl.BlockSpec(memory_space=pl.ANY),
                   pl.BlockSpec(memory_space=pl.ANY)],
        out_shape=[jax.ShapeDtypeStruct((GQA_WIDTH, SEQ), BF16),
                   jax.ShapeDtypeStruct((n_slot_rows, D_MODEL), F32),
                   jax.ShapeDtypeStruct((n_slot_rows, D_MODEL), F32)],
        scratch_shapes=[pltpu.VMEM((GQA_TK, GQA_TQ), F32), pltpu.VMEM((GQA_TK, GQA_TQ), F32),
                        pltpu.VMEM((MOE_TM, D_MODEL), F32), pltpu.SemaphoreType.DMA(())],
        compiler_params=_cparams(("arbitrary", "arbitrary"), 32),
        name="gqa_attn",
    )(q_t, k_all, v_t)


def _proj_out_kernel(ona_ref, osg_ref, ogq_ref, w_ref, x_ref, g1_ref, lng_ref, lnb_ref,
                     sh2_ref, sc2_ref, rwt_ref, rb_ref, xmid_ref, h2_ref, e_ref, gate_ref):
    y = (_dot(ona_ref[...], w_ref[0:NA_WIDTH, :])
         + _dot(osg_ref[...], w_ref[NA_WIDTH:NA_WIDTH + SG_WIDTH, :])
         + _dot(ogq_ref[...], w_ref[NA_WIDTH + SG_WIDTH:, :]))
    z = ALPHA * x_ref[...] + g1_ref[...] * y
    xm = _layer_norm_rows(z) * lng_ref[...] + lnb_ref[...]
    xmid_ref[...] = xm
    h2 = _layer_norm_rows(xm) * (1.0 + sc2_ref[...]) + sh2_ref[...]
    h2_ref[...] = h2
    e_ref[...], gate_ref[...] = _route(h2, rwt_ref[...], rb_ref[...])


def _proj_out(o_na, o_sg, o_gq, w_bf16, x, g1, ln_g, ln_b, sh2, sc2, rw_t_bf16, rb_col):
    n = x.shape[0]
    tm = ROW_TILE
    vec = pl.BlockSpec((1, D_MODEL), lambda i: (0, 0))
    return pl.pallas_call(
        _proj_out_kernel,
        grid=(n // tm,),
        in_specs=[
            pl.BlockSpec((tm, NA_WIDTH), lambda i: (i, 0)),
            pl.BlockSpec((tm, SG_WIDTH), lambda i: (i, 0)),
            pl.BlockSpec((tm, GQA_WIDTH), lambda i: (i, 0)),
            pl.BlockSpec((MIX_WIDTH, D_MODEL), lambda i: (0, 0)),
            pl.BlockSpec((tm, D_MODEL), lambda i: (i, 0)),
            vec, vec, vec, vec, vec,
            pl.BlockSpec((N_EXPERTS, D_MODEL), lambda i: (0, 0)),
            pl.BlockSpec((N_EXPERTS, 1), lambda i: (0, 0)),
        ],
        out_specs=[pl.BlockSpec((tm, D_MODEL), lambda i: (i, 0)),
                   pl.BlockSpec((tm, D_MODEL), lambda i: (i, 0)),
                   pl.BlockSpec((TOP_K, tm), lambda i: (0, i)),
                   pl.BlockSpec((TOP_K, tm), lambda i: (0, i))],
        out_shape=[jax.ShapeDtypeStruct((n, D_MODEL), F32),
                   jax.ShapeDtypeStruct((n, D_MODEL), F32),
                   jax.ShapeDtypeStruct((TOP_K, n), I32),
                   jax.ShapeDtypeStruct((TOP_K, n), F32)],
        compiler_params=_cparams(("arbitrary",), 48),
        name="proj_out",
    )(o_na, o_sg, o_gq, w_bf16, x, g1, ln_g, ln_b, sh2, sc2, rw_t_bf16, rb_col)


def _route(h, rw_t, rb_col):
    tm = h.shape[0]
    logits = _dot_nt(rw_t, h.astype(BF16))
    scores = jax.nn.sigmoid(logits)
    sel = scores + rb_col
    sub = lax.broadcasted_iota(I32, (EXPERTS_PER_GROUP, tm), 0)
    best = None
    for g in range(N_GROUPS):
        rows = slice(g * EXPERTS_PER_GROUP, (g + 1) * EXPERTS_PER_GROUP)
        v = sel[rows, :]
        sc = scores[rows, :]
        m1 = jnp.max(v, axis=0, keepdims=True)
        i1 = jnp.min(jnp.where(v == m1, sub, EXPERTS_PER_GROUP), axis=0, keepdims=True)
        v2 = jnp.where(sub == i1, -jnp.inf, v)
        m2 = jnp.max(v2, axis=0, keepdims=True)
        i2 = jnp.min(jnp.where(v2 == m2, sub, EXPERTS_PER_GROUP), axis=0, keepdims=True)
        s1 = jnp.sum(jnp.where(sub == i1, sc, 0.0), axis=0, keepdims=True)
        s2 = jnp.sum(jnp.where(sub == i2, sc, 0.0), axis=0, keepdims=True)
        cand = (m1 + m2, i1 + g * EXPERTS_PER_GROUP, i2 + g * EXPERTS_PER_GROUP, s1, s2)
        if best is None:
            best = cand
        else:
            better = cand[0] > best[0]
            best = tuple(jnp.where(better, cn, bs) for cn, bs in zip(cand, best))
    _, e1, e2, s1, s2 = best
    tot = s1 + s2
    return jnp.concatenate([e1, e2], axis=0), jnp.concatenate([s1 / tot, s2 / tot], axis=0)


def _num_tiles(n_tok):
    return (n_tok * TOP_K + N_EXPERTS * (MOE_TM - 1) + MOE_TM - 1) // MOE_TM


def _positions_kernel(e_ref, dest_ref, meta_ref, rank_scr, *, n_tok):
    ch = POS_CHUNK
    n_chunks = n_tok // ch
    iota_e = lax.broadcasted_iota(I32, (N_EXPERTS, ch), 0)
    upper = (lax.broadcasted_iota(I32, (ch, ch), 0) <= lax.broadcasted_iota(I32, (ch, ch), 1)).astype(BF16)

    def one_hots(off):
        e0 = e_ref[0:1, pl.ds(off, ch)]
        e1 = e_ref[1:2, pl.ds(off, ch)]
        return (iota_e == e0).astype(F32), (iota_e == e1).astype(F32)

    def rank_body(c, carry):
        off = pl.multiple_of(c * ch, ch)
        oh0, oh1 = one_hots(off)
        both = oh0 + oh1
        incl = _dot(both.astype(BF16), upper)
        before = incl - both + carry
        rank_scr[0:1, pl.ds(off, ch)] = jnp.sum(oh0 * before, axis=0, keepdims=True)
        rank_scr[1:2, pl.ds(off, ch)] = jnp.sum(oh1 * (before + oh0), axis=0, keepdims=True)
        return carry + jnp.sum(both, axis=1, keepdims=True)

    counts = lax.fori_loop(0, n_chunks, rank_body, jnp.zeros((N_EXPERTS, 1), F32))
    tiles = jnp.zeros((N_EXPERTS, 1), F32)
    for k in range(n_tok * TOP_K // MOE_TM + 1):
        tiles = tiles + (counts > float(k * MOE_TM)).astype(F32)
    tiles_b = jnp.broadcast_to(tiles, (N_EXPERTS, META_LANES)).astype(BF16)
    lower = (lax.broadcasted_iota(I32, (N_EXPERTS, N_EXPERTS), 1)
             < lax.broadcasted_iota(I32, (N_EXPERTS, N_EXPERTS), 0)).astype(BF16)
    tile_start = _dot(lower, tiles_b)
    tile_end = tile_start + tiles
    pad_start = tile_start[:, 0:1] * float(MOE_TM)

    def dest_body(c, carry):
        off = pl.multiple_of(c * ch, ch)
        oh0, oh1 = one_hots(off)
        d0 = rank_scr[0:1, pl.ds(off, ch)] + jnp.sum(oh0 * pad_start, axis=0, keepdims=True)
        d1 = rank_scr[1:2, pl.ds(off, ch)] + jnp.sum(oh1 * pad_start, axis=0, keepdims=True)
        dest_ref[0:1, pl.ds(off, ch)] = d0.astype(I32)
        dest_ref[1:2, pl.ds(off, ch)] = d1.astype(I32)
        return carry

    lax.fori_loop(0, n_chunks, dest_body, 0)

    tile_id = lax.broadcasted_iota(I32, (N_EXPERTS, META_LANES), 1).astype(F32)
    tile_expert = jnp.sum((tile_end <= tile_id).astype(F32), axis=0, keepdims=True)
    tile_expert = jnp.minimum(tile_expert, float(N_EXPERTS - 1))
    inside = (tile_start <= tile_id) & (tile_id < tile_end)
    rows_here = jnp.clip(counts - float(MOE_TM) * (tile_id - tile_start), 0.0, float(MOE_TM))
    tile_rows = jnp.sum(jnp.where(inside, rows_here, 0.0), axis=0, keepdims=True)
    n_used = jnp.sum(tiles, axis=0, keepdims=True)
    row = lax.broadcasted_iota(I32, (8, META_LANES), 0)
    shape = (8, META_LANES)
    meta = jnp.where(row == META_TILE_EXPERT, jnp.broadcast_to(tile_expert, shape),
                     jnp.where(row == META_TILE_ROWS, jnp.broadcast_to(tile_rows, shape),
                               jnp.where(row == META_N_USED, jnp.broadcast_to(n_used, shape), 0.0)))
    meta_ref[...] = meta.astype(I32)


def _positions(e_all):
    n_tok = e_all.shape[1]
    return pl.pallas_call(
        functools.partial(_positions_kernel, n_tok=n_tok),
        out_shape=[jax.ShapeDtypeStruct((TOP_K, n_tok), I32),
                   jax.ShapeDtypeStruct((8, 128), I32)],
        scratch_shapes=[pltpu.VMEM((TOP_K, n_tok), F32)],
        compiler_params=pltpu.CompilerParams(vmem_limit_bytes=16 * 1024 * 1024),
        name="moe_positions",
    )(e_all)


def _dispatch_kernel(dest_ref, h_ref, xs_in_ref, xs_ref, sem, *, n_tok):
    del xs_in_ref
    tm = h_ref.shape[0]
    base = pl.program_id(0) * tm

    def row_copy(r, k):
        d = dest_ref[k * n_tok + base + r]
        return pltpu.make_async_copy(h_ref.at[pl.ds(r, 1), :], xs_ref.at[pl.ds(d, 1), :], sem)

    def start(r, carry):
        for k in range(TOP_K):
            row_copy(r, k).start()
        return carry

    def wait(r, carry):
        for k in range(TOP_K):
            row_copy(r, k).wait()
        return carry

    lax.fori_loop(0, tm, start, 0)
    lax.fori_loop(0, tm, wait, 0)


def _dispatch(dest_flat, h2, xs):
    n = h2.shape[0]
    tm = ROW_TILE
    return pl.pallas_call(
        functools.partial(_dispatch_kernel, n_tok=n),
        grid_spec=pltpu.PrefetchScalarGridSpec(
            num_scalar_prefetch=1,
            grid=(n // tm,),
            in_specs=[pl.BlockSpec((tm, D_MODEL), lambda i, d: (i, 0)),
                      pl.BlockSpec(memory_space=pl.ANY)],
            out_specs=pl.BlockSpec(memory_space=pl.ANY),
            scratch_shapes=[pltpu.SemaphoreType.DMA(())],
        ),
        out_shape=jax.ShapeDtypeStruct(xs.shape, xs.dtype),
        input_output_aliases={2: 0},
        compiler_params=_cparams(("arbitrary",), 16),
        name="moe_dispatch",
    )(dest_flat, h2, xs)


def _moe_kernel(meta_ref, x_ref, wg_ref, wu_ref, wd_ref, y_in_ref, y_ref, xb_scr):
    del y_in_ref
    i = pl.program_id(0)
    f = pl.program_id(1)
    used = i < meta_ref[META_N_USED * META_LANES]
    rows = meta_ref[META_TILE_ROWS * META_LANES + i]
    n_sub = (rows + (MOE_SUB - 1)) >> MOE_SUB_LOG2

    @pl.when(used & (f == 0))
    def _():
        y_ref[...] = jnp.zeros_like(y_ref)

    for occupied in range(1, MOE_TM // MOE_SUB + 1):
        m_rows = occupied * MOE_SUB

        @pl.when(used & (n_sub == occupied))
        def _(m_rows=m_rows):
            @pl.when(f == 0)
            def _():
                xb_scr[0:m_rows, :] = x_ref[0:m_rows, :].astype(BF16)

            xb = xb_scr[0:m_rows, :]
            gate = _dot(xb, wg_ref[...].astype(BF16))
            up = _dot(xb, wu_ref[...].astype(BF16))
            hidden = (gate * jax.nn.sigmoid(gate) * up).astype(BF16)
            y_ref[0:m_rows, :] += _dot(hidden, wd_ref[...].astype(BF16))


def _moe_ffn(meta_flat, xs, w_gate, w_up, w_down, y_zero, layer, n_tiles):
    n_used_at = META_N_USED * META_LANES

    def last_used(i, m):
        return jnp.minimum(i, m[n_used_at] - 1)

    def f_eff(i, f, m):
        return jnp.where(i < m[n_used_at], f, MOE_NF - 1)

    return pl.pallas_call(
        _moe_kernel,
        grid_spec=pltpu.PrefetchScalarGridSpec(
            num_scalar_prefetch=1,
            grid=(n_tiles, MOE_NF),
            in_specs=[
                pl.BlockSpec((MOE_TM, D_MODEL), lambda i, f, m: (last_used(i, m), 0)),
                pl.BlockSpec((None, None, D_MODEL, MOE_TF),
                             lambda i, f, m: (layer, m[last_used(i, m)], 0, f_eff(i, f, m))),
                pl.BlockSpec((None, None, D_MODEL, MOE_TF),
                             lambda i, f, m: (layer, m[last_used(i, m)], 0, f_eff(i, f, m))),
                pl.BlockSpec((None, None, MOE_TF, D_MODEL),
                             lambda i, f, m: (layer, m[last_used(i, m)], f_eff(i, f, m), 0)),
                pl.BlockSpec(memory_space=pl.ANY),
            ],
            out_specs=pl.BlockSpec((MOE_TM, D_MODEL), lambda i, f, m: (last_used(i, m), 0)),
            scratch_shapes=[pltpu.VMEM((MOE_TM, D_MODEL), BF16)],
        ),
        out_shape=jax.ShapeDtypeStruct((n_tiles * MOE_TM, D_MODEL), F32),
        input_output_aliases={5: 0},
        compiler_params=_cparams(("arbitrary", "arbitrary"), 56),
        name="moe_ffn",
    )(meta_flat, xs, w_gate, w_up, w_down, y_zero)


def _combine_kernel(dest_ref, y_ref, gt_ref, x_ref, g2_ref, lng_ref, lnb_ref, o_ref,
                    buf_ref, sem, *, n_tok):
    tm = x_ref.shape[0]
    i = pl.program_id(0)

    def row_copy(tile, slot, r, k):
        d = dest_ref[k * n_tok + tile * tm + r]
        return pltpu.make_async_copy(y_ref.at[pl.ds(d, 1), :],
                                     buf_ref.at[slot, k, pl.ds(r, 1), :], sem.at[slot])

    def start_tile(tile, slot):
        def start(r, carry):
            for k in range(TOP_K):
                row_copy(tile, slot, r, k).start()
            return carry
        lax.fori_loop(0, tm, start, 0)

    def wait_tile(tile, slot):
        def wait(r, carry):
            for k in range(TOP_K):
                row_copy(tile, slot, r, k).wait()
            return carry
        lax.fori_loop(0, tm, wait, 0)

    @pl.when(i == 0)
    def _():
        start_tile(i, 0)

    for slot in range(2):
        @pl.when((i + 1 < pl.num_programs(0)) & ((i + 1) % 2 == slot))
        def _(slot=slot):
            start_tile(i + 1, slot)

    for slot in range(2):
        @pl.when(i % 2 == slot)
        def _(slot=slot):
            wait_tile(i, slot)
            gt = gt_ref[...]
            moe = buf_ref[slot, 0] * gt[:, 0:1] + buf_ref[slot, 1] * gt[:, 1:2]
            z = ALPHA * x_ref[...] + g2_ref[...] * moe
            o_ref[...] = _layer_norm_rows(z) * lng_ref[...] + lnb_ref[...]


def _combine(dest_flat, y, gates_t, x_mid, g2, ln_g, ln_b):
    n = x_mid.shape[0]
    tm = ROW_TILE
    vec = pl.BlockSpec((1, D_MODEL), lambda i, d: (0, 0))
    return pl.pallas_call(
        functools.partial(_combine_kernel, n_tok=n),
        grid_spec=pltpu.PrefetchScalarGridSpec(
            num_scalar_prefetch=1,
            grid=(n // tm,),
            in_specs=[pl.BlockSpec(memory_space=pl.ANY),
                      pl.BlockSpec((tm, TOP_K), lambda i, d: (i, 0)),
                      pl.BlockSpec((tm, D_MODEL), lambda i, d: (i, 0)),
                      vec, vec, vec],
            out_specs=pl.BlockSpec((tm, D_MODEL), lambda i, d: (i, 0)),
            scratch_shapes=[pltpu.VMEM((2, TOP_K, tm, D_MODEL), F32),
                            pltpu.SemaphoreType.DMA((2,))],
        ),
        out_shape=jax.ShapeDtypeStruct((n, D_MODEL), F32),
        compiler_params=_cparams(("arbitrary",), 32),
        name="moe_combine",
    )(dest_flat, y, gates_t, x_mid, g2, ln_g, ln_b)


def _rope_tables():
    t = jnp.arange(SEQ, dtype=jnp.int32)
    row = (t // GRID_W).astype(F32)
    col = (t % GRID_W).astype(F32)
    inv_freq = ROPE_THETA ** (-jnp.arange(ROPE_PAIRS, dtype=F32) / ROPE_PAIRS)
    ang_r = row[:, None] * inv_freq
    ang_c = col[:, None] * inv_freq
    cos = jnp.concatenate([jnp.cos(ang_r), jnp.cos(ang_r), jnp.cos(ang_c), jnp.cos(ang_c)], axis=-1)
    sin = jnp.concatenate([-jnp.sin(ang_r), jnp.sin(ang_r), -jnp.sin(ang_c), jnp.sin(ang_c)], axis=-1)
    return cos, sin


def kernel(x, c, ctx, c_ctx, ada_w, ada_b, w_in, na_rpb, sg_ln_g, sg_ln_b, sg_w, sg_b, q_norm_g, k_norm_g, w_out, ln_mix_g, ln_mix_b, router_w, router_bias, moe_w_gate, moe_w_up, moe_w_down, ln_ffn_g, ln_ffn_b):
    cos, sin = _rope_tables()
    cos_ctx = jnp.ones((CTX_LEN, HEAD_DIM), F32)
    sin_ctx = jnp.zeros((CTX_LEN, HEAD_DIM), F32)

    cvec = jnp.zeros((8, D_MODEL), F32).at[0].set(c[0]).at[1].set(c_ctx)
    mods = _ada(cvec, ada_w, ada_b)

    na_bias = _na_bias_tables(na_rpb)
    rw_t = router_w.T.astype(BF16)
    rb_col = router_bias.reshape(N_EXPERTS, 1).astype(F32)

    xl = x[0]
    xc = ctx[0]
    for layer in range(DEPTH):
        last = layer == DEPTH - 1
        sh1, sc1, g1, sh2, sc2, g2 = jnp.split(mods[layer, 0:1], 6, axis=-1)
        csh1, csc1, cg1, csh2, csc2, cg2 = jnp.split(mods[layer, 1:2], 6, axis=-1)
        w_in_b = w_in[layer].astype(BF16)
        w_out_b = w_out[layer].astype(BF16)
        sg_w_b = sg_w[layer].astype(BF16)
        q_gain = q_norm_g[layer].reshape(1, HEAD_DIM)
        k_gain = k_norm_g[layer].reshape(1, HEAD_DIM)
        lng = ln_mix_g[layer].reshape(1, D_MODEL)
        lnb = ln_mix_b[layer].reshape(1, D_MODEL)
        fng = ln_ffn_g[layer].reshape(1, D_MODEL)
        fnb = ln_ffn_b[layer].reshape(1, D_MODEL)

        proj, sgz = _proj_in(xl, sh1, sc1, w_in_b, cos, sin, q_gain, k_gain, tm=1024)
        proj_c, sgz_c = _proj_in(xc, csh1, csc1, w_in_b, cos_ctx, sin_ctx, q_gain, k_gain, tm=CTX_LEN)

        o_na = _na_attention(proj, proj_c, na_bias, layer)
        o_sg = _spatial_gating(sgz, sg_ln_g[layer], sg_ln_b[layer], sg_w_b, sg_b[layer])
        q_t = proj[:, COL_GQ * HEAD_DIM:COL_GK * HEAD_DIM].T
        k_all = jnp.concatenate([proj[:, COL_GK * HEAD_DIM:COL_GV * HEAD_DIM],
                                 proj_c[:, COL_GK * HEAD_DIM:COL_GV * HEAD_DIM]], axis=0)
        v_t = jnp.concatenate([proj[:, COL_GV * HEAD_DIM:], proj_c[:, COL_GV * HEAD_DIM:]], axis=0).T
        n_tok = SEQ if last else SEQ + CTX_LEN
        n_tiles = _num_tiles(n_tok)
        o_gq_t, xs, y_zero = _gqa_attention(q_t, k_all, v_t, n_tiles * MOE_TM)
        x_mid, h2, e_all, gates = _proj_out(o_na, o_sg, o_gq_t.T, w_out_b, xl, g1, lng, lnb, sh2, sc2,
                                            rw_t, rb_col)

        if not last:
            c_na = _ctx_attention(proj_c, NA_HEADS, COL_NA_Q, COL_NA_K, COL_NA_V, 1, False)
            c_sg = _spatial_gating(sgz_c, sg_ln_g[layer], sg_ln_b[layer], sg_w_b, sg_b[layer])
            c_gq = _ctx_attention(proj_c, GQA_Q_HEADS, COL_GQ, COL_GK, COL_GV,
                                  GQA_Q_HEADS // GQA_KV_HEADS, True)
            xc_mid, h2_c, e_ctx, gates_ctx = _proj_out(c_na, c_sg, c_gq, w_out_b, xc, cg1, lng, lnb,
                                                       csh2, csc2, rw_t, rb_col)
            e_all = jnp.concatenate([e_all, e_ctx], axis=1)
            gates = jnp.concatenate([gates, gates_ctx], axis=1)

        dest, meta = _positions(e_all)
        meta_flat = meta.reshape(-1)
        xs = _dispatch(dest[:, :SEQ].reshape(-1), h2, xs)
        if not last:
            xs = _dispatch(dest[:, SEQ:].reshape(-1), h2_c, xs)
        y = _moe_ffn(meta_flat, xs, moe_w_gate, moe_w_up, moe_w_down, y_zero, layer, n_tiles)
        gates_t = gates.T
        xl = _combine(dest[:, :SEQ].reshape(-1), y, gates_t[:SEQ], x_mid, g2, fng, fnb)
        if not last:
            xc = _combine(dest[:, SEQ:].reshape(-1), y, gates_t[SEQ:], xc_mid, cg2, fng, fnb)
    return xl[None]
```

```python
import functools

import jax
import jax.numpy as jnp
import numpy as np
from jax import lax
from jax.experimental import pallas as pl
from jax.experimental.pallas import tpu as pltpu

F32 = jnp.float32
BF16 = jnp.bfloat16
I32 = jnp.int32

D_MODEL = 2048
SEQ = 8192
DEPTH = 2
GRID_W = 64
ROWS = SEQ // GRID_W
CTX_LEN = 256
HEAD_DIM = 128

NA_HEADS = 4
WIN_H = 8
WIN_W = 16
SG_GROUPS = 4
SG_CHUNK = 128
GQA_Q_HEADS = 8
GQA_KV_HEADS = 2
ROPE_THETA = 10000.0
ROPE_PAIRS = HEAD_DIM // 4

NA_WIDTH = NA_HEADS * HEAD_DIM
SG_WIDTH = SG_GROUPS * HEAD_DIM
GQA_WIDTH = GQA_Q_HEADS * HEAD_DIM
KV_WIDTH = GQA_KV_HEADS * HEAD_DIM
MIX_WIDTH = NA_WIDTH + SG_WIDTH + GQA_WIDTH
IN_WIDTH = 3 * NA_WIDTH + 2 * SG_WIDTH + GQA_WIDTH + 2 * KV_WIDTH

N_EXPERTS = 32
N_GROUPS = 4
EXPERTS_PER_GROUP = N_EXPERTS // N_GROUPS
TOP_K = 2
D_EXPERT = 1024

ALPHA = (2 * DEPTH) ** 0.25
LN_EPS = 1e-6
ATTN_SCALE = HEAD_DIM ** -0.5
LOG2E = 1.4426950408889634
NEG_BIG = -1e30
KEYS = SEQ + CTX_LEN

COL_NA_Q = 0
COL_NA_K = NA_HEADS
COL_NA_V = 2 * NA_HEADS
COL_SG = 3 * NA_HEADS
COL_GQ = COL_SG + 2 * SG_GROUPS
COL_GK = COL_GQ + GQA_Q_HEADS
COL_GV = COL_GK + GQA_KV_HEADS

PROJ_TN = 512
PROJ_NJ = IN_WIDTH // PROJ_TN
PROJ_ROWS = 256

NA_R = 4
NA_TQ = NA_R * GRID_W
NA_KR = NA_R + WIN_H - 1
NA_TK = NA_KR * GRID_W
NA_NB = ROWS // NA_R

MOE_TM = 768
MOE_SUB = 128
MOE_SUB_LOG2 = 7
MOE_TF = 512
MOE_NF = D_EXPERT // MOE_TF
META_TILE_EXPERT = 0
META_TILE_ROWS = 1
META_N_USED = 2
META_LANES = 128
POS_CHUNK = 256

ROW_TILE = 256


def _cparams(semantics, vmem_mib):
    return pltpu.CompilerParams(dimension_semantics=semantics,
                                vmem_limit_bytes=vmem_mib * 1024 * 1024)


def _layer_norm_rows(x):
    mu = jnp.mean(x, axis=-1, keepdims=True)
    xc = x - mu
    var = jnp.mean(xc * xc, axis=-1, keepdims=True)
    return xc * lax.rsqrt(var + LN_EPS)


def _dot(a, b):
    return jnp.dot(a, b, preferred_element_type=F32)


def _dot_nt(a, b):
    return lax.dot_general(a, b, (((1,), (1,)), ((), ())), preferred_element_type=F32)


ADA_TN = 1024


def _ada_kernel(c_ref, w_ref, b_ref, o_ref):
    cv = c_ref[...]
    act = (cv * jax.nn.sigmoid(cv)).astype(BF16)
    o_ref[...] = _dot(act, w_ref[...].astype(BF16)) + b_ref[...]


def _ada(cvec, ada_w, ada_b):
    n_out = 6 * D_MODEL
    return pl.pallas_call(
        _ada_kernel,
        grid=(DEPTH, n_out // ADA_TN),
        in_specs=[
            pl.BlockSpec((8, D_MODEL), lambda l, j: (0, 0)),
            pl.BlockSpec((None, D_MODEL, ADA_TN), lambda l, j: (l, 0, j)),
            pl.BlockSpec((None, 1, ADA_TN), lambda l, j: (l, 0, j)),
        ],
        out_specs=pl.BlockSpec((None, 8, ADA_TN), lambda l, j: (l, 0, j)),
        out_shape=jax.ShapeDtypeStruct((DEPTH, 8, n_out), F32),
        compiler_params=_cparams(("arbitrary", "arbitrary"), 40),
        name="ada_mod",
    )(cvec, ada_w, ada_b.reshape(DEPTH, 1, n_out))


def _rope_partner(y):
    lane = lax.broadcasted_iota(I32, y.shape, 1)
    first = (lane % (2 * ROPE_PAIRS)) < ROPE_PAIRS
    return jnp.where(first, pltpu.roll(y, HEAD_DIM - ROPE_PAIRS, 1), pltpu.roll(y, ROPE_PAIRS, 1))


def _norm_rope(zh, gain, cos, sin):
    ms = jnp.mean(zh * zh, axis=-1, keepdims=True)
    y = zh * lax.rsqrt(ms + LN_EPS) * gain
    return y * cos + _rope_partner(y) * sin


def _proj_in_kernel(x_ref, sh_ref, sc_ref, w_ref, cos_ref, sin_ref, qg_ref, kg_ref,
                    proj_ref, sgz_ref, h_scr):
    j = pl.program_id(1)
    tm = x_ref.shape[0]
    chunks = [slice(r, r + min(PROJ_ROWS, tm)) for r in range(0, tm, PROJ_ROWS)]

    def zed(rows):
        return _dot(h_scr[rows, :], w_ref[...])

    @pl.when(j == 0)
    def _():
        for rows in chunks:
            h = _layer_norm_rows(x_ref[rows, :]) * (1.0 + sc_ref[...]) + sh_ref[...]
            h_scr[rows, :] = h.astype(BF16)
            proj_ref[rows, :] = (zed(rows) * ATTN_SCALE).astype(BF16)

    @pl.when((j == 1) | (j == 2))
    def _():
        for rows in chunks:
            proj_ref[rows, :] = zed(rows).astype(BF16)

    @pl.when((j == 3) | (j == 4))
    def _():
        for rows in chunks:
            z = zed(rows)
            proj_ref[rows, :] = z.astype(BF16)
            sgz_ref[rows, :] = z

    @pl.when((j == 5) | (j == 6))
    def _():
        for rows in chunks:
            z = zed(rows)
            cos = cos_ref[rows, :]
            sin = sin_ref[rows, :]
            for hh in range(PROJ_TN // HEAD_DIM):
                cols = slice(hh * HEAD_DIM, (hh + 1) * HEAD_DIM)
                q = _norm_rope(z[:, cols], qg_ref[...], cos, sin)
                proj_ref[rows, cols] = (q * (ATTN_SCALE * LOG2E)).astype(BF16)

    @pl.when(j == 7)
    def _():
        for rows in chunks:
            z = zed(rows)
            cos = cos_ref[rows, :]
            sin = sin_ref[rows, :]
            for hh in range(GQA_KV_HEADS):
                cols = slice(hh * HEAD_DIM, (hh + 1) * HEAD_DIM)
                proj_ref[rows, cols] = _norm_rope(z[:, cols], kg_ref[...], cos, sin).astype(BF16)
            proj_ref[rows, KV_WIDTH:] = z[:, KV_WIDTH:].astype(BF16)


def _proj_in(x, sh, sc, w_bf16, cos, sin, q_gain, k_gain, tm):
    n = x.shape[0]
    return pl.pallas_call(
        _proj_in_kernel,
        grid=(n // tm, PROJ_NJ),
        in_specs=[
            pl.BlockSpec((tm, D_MODEL), lambda i, j: (i, 0)),
            pl.BlockSpec((1, D_MODEL), lambda i, j: (0, 0)),
            pl.BlockSpec((1, D_MODEL), lambda i, j: (0, 0)),
            pl.BlockSpec((D_MODEL, PROJ_TN), lambda i, j: (0, j)),
            pl.BlockSpec((tm, HEAD_DIM), lambda i, j: (i, 0)),
            pl.BlockSpec((tm, HEAD_DIM), lambda i, j: (i, 0)),
            pl.BlockSpec((1, HEAD_DIM), lambda i, j: (0, 0)),
            pl.BlockSpec((1, HEAD_DIM), lambda i, j: (0, 0)),
        ],
        out_specs=[
            pl.BlockSpec((tm, PROJ_TN), lambda i, j: (i, j)),
            pl.BlockSpec((tm, PROJ_TN), lambda i, j: (i, jnp.clip(j - 3, 0, 1))),
        ],
        out_shape=[
            jax.ShapeDtypeStruct((n, IN_WIDTH), BF16),
            jax.ShapeDtypeStruct((n, 2 * SG_WIDTH), F32),
        ],
        scratch_shapes=[pltpu.VMEM((tm, D_MODEL), BF16)],
        compiler_params=_cparams(("arbitrary", "arbitrary"), 48),
        name="proj_in",
    )(x, sh, sc, w_bf16, cos, sin, q_gain, k_gain)


def _na_kernel(q_ref, k_ref, v_ref, kc_ref, vc_ref, bias_ref, o_ref):
    b = pl.program_id(0)
    key_row0 = jnp.clip(b * NA_R - WIN_H // 2, 0, ROWS - NA_KR)
    start = pl.multiple_of(key_row0 * GRID_W, GRID_W)
    for h in range(NA_HEADS):
        cols = slice(h * HEAD_DIM, (h + 1) * HEAD_DIM)
        q = q_ref[:, cols]
        kw = k_ref[pl.ds(start, NA_TK), cols]
        vw = v_ref[pl.ds(start, NA_TK), cols]
        s_loc = _dot_nt(q, kw) + bias_ref[h]
        s_ctx = _dot_nt(q, kc_ref[:, cols])
        m = jnp.maximum(jnp.max(s_loc, axis=-1, keepdims=True), jnp.max(s_ctx, axis=-1, keepdims=True))
        p_loc = jnp.exp(s_loc - m)
        p_ctx = jnp.exp(s_ctx - m)
        denom = jnp.sum(p_loc, axis=-1, keepdims=True) + jnp.sum(p_ctx, axis=-1, keepdims=True)
        o = _dot(p_loc.astype(BF16), vw) + _dot(p_ctx.astype(BF16), vc_ref[:, cols])
        o_ref[:, cols] = (o / denom).astype(BF16)


def _na_bias_tables(rpb):
    tables = []
    for r0, k0 in ((0, 0), (2 * NA_R, 2 * NA_R - WIN_H // 2), (ROWS - NA_R, ROWS - NA_KR)):
        r = r0 + np.arange(NA_R)
        rs = np.clip(r - WIN_H // 2, 0, ROWS - WIN_H)
        kr = k0 + np.arange(NA_KR)
        ok_r = (kr[None, :] >= rs[:, None]) & (kr[None, :] < rs[:, None] + WIN_H)
        drow = kr[None, :] - r[:, None] + (WIN_H - 1)
        c = np.arange(GRID_W)
        cs = np.clip(c - WIN_W // 2, 0, GRID_W - WIN_W)
        kc = np.arange(GRID_W)
        ok_c = (kc[None, :] >= cs[:, None]) & (kc[None, :] < cs[:, None] + WIN_W)
        dcol = kc[None, :] - c[:, None] + (WIN_W - 1)
        ok = ok_r[:, None, :, None] & ok_c[None, :, None, :]
        pick_r = (np.clip(drow, 0, 2 * WIN_H - 2)[..., None] == np.arange(2 * WIN_H - 1)).astype(np.float32)
        rows_picked = jnp.einsum('rka,lhab->lhrkb', pick_r, rpb.astype(F32), precision=lax.Precision.HIGHEST)
        lead = GRID_W - WIN_W
        padded = jnp.pad(rows_picked, ((0, 0),) * 4 + ((lead, lead),))
        bias = jnp.stack([padded[..., lead + WIN_W - 1 - cq:lead + WIN_W - 1 - cq + GRID_W]
                          for cq in range(GRID_W)], axis=3)
        tables.append(jnp.where(ok[None, None], bias, NEG_BIG).reshape(DEPTH, NA_HEADS, NA_TQ, NA_TK))
    return jnp.stack(tables, axis=1)


def _na_attention(proj, proj_ctx, bias_tab, layer):
    def bias_idx(b):
        return (layer, jnp.where(b == 0, 0, jnp.where(b == NA_NB - 1, 2, 1)), 0, 0, 0)

    return pl.pallas_call(
        _na_kernel,
        grid=(NA_NB,),
        in_specs=[
            pl.BlockSpec((NA_TQ, NA_WIDTH), lambda b: (b, 0)),
            pl.BlockSpec((SEQ, NA_WIDTH), lambda b: (0, 1)),
            pl.BlockSpec((SEQ, NA_WIDTH), lambda b: (0, 2)),
            pl.BlockSpec((CTX_LEN, NA_WIDTH), lambda b: (0, 1)),
            pl.BlockSpec((CTX_LEN, NA_WIDTH), lambda b: (0, 2)),
            pl.BlockSpec((None, None, NA_HEADS, NA_TQ, NA_TK), bias_idx),
        ],
        out_specs=pl.BlockSpec((NA_TQ, NA_WIDTH), lambda b: (b, 0)),
        out_shape=jax.ShapeDtypeStruct((SEQ, NA_WIDTH), BF16),
        compiler_params=_cparams(("arbitrary",), 48),
        name="na_attn",
    )(proj, proj, proj, proj_ctx, proj_ctx, bias_tab)


def _ctx_attn_kernel(q_ref, k_ref, v_ref, o_ref, *, log2_domain):
    s = _dot_nt(q_ref[...], k_ref[...])
    m = jnp.max(s, axis=-1, keepdims=True)
    p = jnp.exp2(s - m) if log2_domain else jnp.exp(s - m)
    denom = jnp.sum(p, axis=-1, keepdims=True)
    o_ref[...] = (_dot(p.astype(BF16), v_ref[...]) / denom).astype(BF16)


def _ctx_attention(proj_ctx, n_heads, q_col, k_col, v_col, q_per_kv, log2_domain):
    return pl.pallas_call(
        functools.partial(_ctx_attn_kernel, log2_domain=log2_domain),
        grid=(n_heads,),
        in_specs=[
            pl.BlockSpec((CTX_LEN, HEAD_DIM), lambda h: (0, q_col + h)),
            pl.BlockSpec((CTX_LEN, HEAD_DIM), lambda h: (0, k_col + h // q_per_kv)),
            pl.BlockSpec((CTX_LEN, HEAD_DIM), lambda h: (0, v_col + h // q_per_kv)),
        ],
        out_specs=pl.BlockSpec((CTX_LEN, HEAD_DIM), lambda h: (0, h)),
        out_shape=jax.ShapeDtypeStruct((CTX_LEN, n_heads * HEAD_DIM), BF16),
        compiler_params=_cparams(("arbitrary",), 16),
        name="ctx_attn",
    )(proj_ctx, proj_ctx, proj_ctx)


SG_ROWS = 256


def _sg_kernel(zu_ref, zv_ref, g_ref, b_ref, w_ref, bs_ref, o_ref):
    for c in range(SG_ROWS // SG_CHUNK):
        rows = slice(c * SG_CHUNK, (c + 1) * SG_CHUNK)
        for g in range(SG_GROUPS):
            cols = slice(g * HEAD_DIM, (g + 1) * HEAD_DIM)
            v = jax.nn.gelu(zv_ref[rows, cols])
            vn = _layer_norm_rows(v) * g_ref[:, cols] + b_ref[:, cols]
            t = _dot(w_ref[g], vn.astype(BF16)) + bs_ref[:, g:g + 1]
            o_ref[rows, cols] = (jax.nn.gelu(zu_ref[rows, cols]) * t).astype(BF16)


def _spatial_gating(sgz, ln_g, ln_b, w_bf16, b_s):
    n = sgz.shape[0]
    return pl.pallas_call(
        _sg_kernel,
        grid=(n // SG_ROWS,),
        in_specs=[
            pl.BlockSpec((SG_ROWS, SG_WIDTH), lambda i: (i, 0)),
            pl.BlockSpec((SG_ROWS, SG_WIDTH), lambda i: (i, 1)),
            pl.BlockSpec((1, SG_WIDTH), lambda i: (0, 0)),
            pl.BlockSpec((1, SG_WIDTH), lambda i: (0, 0)),
            pl.BlockSpec((SG_GROUPS, SG_CHUNK, SG_CHUNK), lambda i: (0, 0, 0)),
            pl.BlockSpec((SG_CHUNK, SG_GROUPS), lambda i: (0, 0)),
        ],
        out_specs=pl.BlockSpec((SG_ROWS, SG_WIDTH), lambda i: (i, 0)),
        out_shape=jax.ShapeDtypeStruct((n, SG_WIDTH), BF16),
        compiler_params=_cparams(("arbitrary",), 16),
        name="spatial_gating",
    )(sgz, sgz, ln_g.reshape(1, SG_WIDTH), ln_b.reshape(1, SG_WIDTH), w_bf16, b_s.T)


GQA_TQ = 512
GQA_TK = 512


def _gqa_kernel(qt_ref, k_ref, vt_ref, ot_ref, slots_x_ref, slots_y_ref, sa_ref, sb_ref, zero_ref, zero_sem):
    qt = qt_ref[...]
    n_chunks = SEQ // GQA_TK

    step = pl.program_id(0) * pl.num_programs(1) + pl.program_id(1)
    n_steps = pl.num_programs(0) * pl.num_programs(1)
    n_zero_blocks = slots_x_ref.shape[0] // MOE_TM

    def zero_copy(dst_ref, c):
        off = pl.multiple_of(c * MOE_TM, MOE_TM)
        return pltpu.make_async_copy(zero_ref, dst_ref.at[pl.ds(off, MOE_TM), :], zero_sem)

    @pl.when(step == 0)
    def _():
        zero_ref[...] = jnp.zeros_like(zero_ref)

    @pl.when(step < n_zero_blocks)
    def _():
        zero_copy(slots_x_ref, step).start()

    @pl.when((step >= n_zero_blocks) & (step < 2 * n_zero_blocks))
    def _():
        zero_copy(slots_y_ref, step - n_zero_blocks).start()

    def scores(t, dst_ref):
        off = pl.multiple_of(t * GQA_TK, GQA_TK)
        s = _dot(k_ref[pl.ds(off, GQA_TK), :], qt)
        dst_ref[...] = s
        return jnp.max(s, axis=0, keepdims=True)

    def consume(t, src_ref, m_chunk, m, l, acc):
        off = pl.multiple_of(t * GQA_TK, GQA_TK)
        m_new = jnp.maximum(m, m_chunk)
        alpha = jnp.exp2(m - m_new)
        p = jnp.exp2(src_ref[...] - m_new)
        l = alpha * l + jnp.sum(p, axis=0, keepdims=True)
        acc = alpha * acc + _dot(vt_ref[:, pl.ds(off, GQA_TK)], p.astype(BF16))
        return m_new, l, acc

    s = _dot(k_ref[pl.ds(SEQ, CTX_LEN), :], qt)
    m = jnp.max(s, axis=0, keepdims=True)
    p = jnp.exp2(s - m)
    l = jnp.sum(p, axis=0, keepdims=True)
    acc = _dot(vt_ref[:, pl.ds(SEQ, CTX_LEN)], p.astype(BF16))
    mc_a = scores(0, sa_ref)

    def body(u, carry):
        m, l, acc, mc_a = carry
        mc_b = scores(2 * u + 1, sb_ref)
        m, l, acc = consume(2 * u, sa_ref, mc_a, m, l, acc)
        mc_a = scores(2 * u + 2, sa_ref)
        m, l, acc = consume(2 * u + 1, sb_ref, mc_b, m, l, acc)
        return m, l, acc, mc_a

    m, l, acc, mc_a = lax.fori_loop(0, n_chunks // 2 - 1, body, (m, l, acc, mc_a))
    mc_b = scores(n_chunks - 1, sb_ref)
    m, l, acc = consume(n_chunks - 2, sa_ref, mc_a, m, l, acc)
    m, l, acc = consume(n_chunks - 1, sb_ref, mc_b, m, l, acc)
    ot_ref[...] = (acc / l).astype(BF16)

    @pl.when(step == n_steps - 1)
    def _():
        def wait(c, carry):
            zero_copy(slots_x_ref, c).wait()
            zero_copy(slots_y_ref, c).wait()
            return carry
        lax.fori_loop(0, n_zero_blocks, wait, 0)


def _gqa_attention(q_t, k_all, v_t, n_slot_rows):
    grp = GQA_Q_HEADS // GQA_KV_HEADS
    grid = (GQA_Q_HEADS, SEQ // GQA_TQ)
    assert n_slot_rows % MOE_TM == 0 and 2 * (n_slot_rows // MOE_TM) <= grid[0] * grid[1]
    return pl.pallas_call(
        _gqa_kernel,
        grid=grid,
        in_specs=[
            pl.BlockSpec((HEAD_DIM, GQA_TQ), lambda h, i: (h, i)),
            pl.BlockSpec((KEYS, HEAD_DIM), lambda h, i: (0, h // grp)),
            pl.BlockSpec((HEAD_DIM, KEYS), lambda h, i: (h // grp, 0)),
        ],
        out_specs=[pl.BlockSpec((HEAD_DIM, GQA_TQ), lambda h, i: (h, i)),
                   pl.BlockSpec(memory_space=pl.ANY),
                   pl.BlockSpec(memory_space=pl.ANY)],
        out_shape=[jax.ShapeDtypeStruct((GQA_WIDTH, SEQ), BF16),
                   jax.ShapeDtypeStruct((n_slot_rows, D_MODEL), F32),
                   jax.ShapeDtypeStruct((n_slot_rows, D_MODEL), F32)],
        scratch_shapes=[pltpu.VMEM((GQA_TK, GQA_TQ), F32), pltpu.VMEM((GQA_TK, GQA_TQ), F32),
                        pltpu.VMEM((MOE_TM, D_MODEL), F32), pltpu.SemaphoreType.DMA(())],
        compiler_params=_cparams(("arbitrary", "arbitrary"), 32),
        name="gqa_attn",
    )(q_t, k_all, v_t)


def _proj_out_kernel(ona_ref, osg_ref, ogq_ref, w_ref, x_ref, g1_ref, lng_ref, lnb_ref,
                     sh2_ref, sc2_ref, rwt_ref, rb_ref, xmid_ref, h2_ref, e_ref, gate_ref, y_scr):
    i = pl.program_id(0)

    @pl.when(i == 0)
    def _():
        y_scr[1] = jnp.zeros_like(y_scr[1])

    for parity in range(2):
        @pl.when(i % 2 == parity)
        def _(parity=parity):
            y_scr[parity] = (_dot(ona_ref[...], w_ref[0:NA_WIDTH, :])
                             + _dot(osg_ref[...], w_ref[NA_WIDTH:NA_WIDTH + SG_WIDTH, :])
                             + _dot(ogq_ref[...], w_ref[NA_WIDTH + SG_WIDTH:, :]))
            z = ALPHA * x_ref[...] + g1_ref[...] * y_scr[1 - parity]
            xm = _layer_norm_rows(z) * lng_ref[...] + lnb_ref[...]
            xmid_ref[...] = xm
            h2 = _layer_norm_rows(xm) * (1.0 + sc2_ref[...]) + sh2_ref[...]
            h2_ref[...] = h2
            e_ref[...], gate_ref[...] = _route(h2, rwt_ref[...], rb_ref[...])


def _proj_out(o_na, o_sg, o_gq, w_bf16, x, g1, ln_g, ln_b, sh2, sc2, rw_t_bf16, rb_col):
    n = x.shape[0]
    tm = ROW_TILE
    n_tiles = n // tm
    vec = pl.BlockSpec((1, D_MODEL), lambda i: (0, 0))

    def mm_tile(i):
        return jnp.minimum(i, n_tiles - 1)

    def ep_tile(i):
        return jnp.maximum(i - 1, 0)

    return pl.pallas_call(
        _proj_out_kernel,
        grid=(n_tiles + 1,),
        in_specs=[
            pl.BlockSpec((tm, NA_WIDTH), lambda i: (mm_tile(i), 0)),
            pl.BlockSpec((tm, SG_WIDTH), lambda i: (mm_tile(i), 0)),
            pl.BlockSpec((tm, GQA_WIDTH), lambda i: (mm_tile(i), 0)),
            pl.BlockSpec((MIX_WIDTH, D_MODEL), lambda i: (0, 0)),
            pl.BlockSpec((tm, D_MODEL), lambda i: (ep_tile(i), 0)),
            vec, vec, vec, vec, vec,
            pl.BlockSpec((N_EXPERTS, D_MODEL), lambda i: (0, 0)),
            pl.BlockSpec((N_EXPERTS, 1), lambda i: (0, 0)),
        ],
        out_specs=[pl.BlockSpec((tm, D_MODEL), lambda i: (ep_tile(i), 0)),
                   pl.BlockSpec((tm, D_MODEL), lambda i: (ep_tile(i), 0)),
                   pl.BlockSpec((TOP_K, tm), lambda i: (0, ep_tile(i))),
                   pl.BlockSpec((TOP_K, tm), lambda i: (0, ep_tile(i)))],
        out_shape=[jax.ShapeDtypeStruct((n, D_MODEL), F32),
                   jax.ShapeDtypeStruct((n, D_MODEL), F32),
                   jax.ShapeDtypeStruct((TOP_K, n), I32),
                   jax.ShapeDtypeStruct((TOP_K, n), F32)],
        scratch_shapes=[pltpu.VMEM((2, tm, D_MODEL), F32)],
        compiler_params=_cparams(("arbitrary",), 48),
        name="proj_out",
    )(o_na, o_sg, o_gq, w_bf16, x, g1, ln_g, ln_b, sh2, sc2, rw_t_bf16, rb_col)


def _route(h, rw_t, rb_col):
    tm = h.shape[0]
    logits = _dot_nt(rw_t, h.astype(BF16))
    scores = jax.nn.sigmoid(logits)
    sel = scores + rb_col
    sub = lax.broadcasted_iota(I32, (EXPERTS_PER_GROUP, tm), 0)
    best = None
    for g in range(N_GROUPS):
        rows = slice(g * EXPERTS_PER_GROUP, (g + 1) * EXPERTS_PER_GROUP)
        v = sel[rows, :]
        sc = scores[rows, :]
        m1 = jnp.max(v, axis=0, keepdims=True)
        i1 = jnp.min(jnp.where(v == m1, sub, EXPERTS_PER_GROUP), axis=0, keepdims=True)
        v2 = jnp.where(sub == i1, -jnp.inf, v)
        m2 = jnp.max(v2, axis=0, keepdims=True)
        i2 = jnp.min(jnp.where(v2 == m2, sub, EXPERTS_PER_GROUP), axis=0, keepdims=True)
        s1 = jnp.sum(jnp.where(sub == i1, sc, 0.0), axis=0, keepdims=True)
        s2 = jnp.sum(jnp.where(sub == i2, sc, 0.0), axis=0, keepdims=True)
        cand = (m1 + m2, i1 + g * EXPERTS_PER_GROUP, i2 + g * EXPERTS_PER_GROUP, s1, s2)
        if best is None:
            best = cand
        else:
            better = cand[0] > best[0]
            best = tuple(jnp.where(better, cn, bs) for cn, bs in zip(cand, best))
    _, e1, e2, s1, s2 = best
    tot = s1 + s2
    return jnp.concatenate([e1, e2], axis=0), jnp.concatenate([s1 / tot, s2 / tot], axis=0)


def _num_tiles(n_tok):
    return (n_tok * TOP_K + N_EXPERTS * (MOE_TM - 1) + MOE_TM - 1) // MOE_TM


def _positions_kernel(e_ref, dest_ref, meta_ref, rank_scr, *, n_tok):
    ch = POS_CHUNK
    n_chunks = n_tok // ch
    iota_e = lax.broadcasted_iota(I32, (N_EXPERTS, ch), 0)
    upper = (lax.broadcasted_iota(I32, (ch, ch), 0) <= lax.broadcasted_iota(I32, (ch, ch), 1)).astype(BF16)

    def one_hots(off):
        e0 = e_ref[0:1, pl.ds(off, ch)]
        e1 = e_ref[1:2, pl.ds(off, ch)]
        return (iota_e == e0).astype(F32), (iota_e == e1).astype(F32)

    def rank_body(c, carry):
        off = pl.multiple_of(c * ch, ch)
        oh0, oh1 = one_hots(off)
        both = oh0 + oh1
        incl = _dot(both.astype(BF16), upper)
        before = incl - both + carry
        rank_scr[0:1, pl.ds(off, ch)] = jnp.sum(oh0 * before, axis=0, keepdims=True)
        rank_scr[1:2, pl.ds(off, ch)] = jnp.sum(oh1 * (before + oh0), axis=0, keepdims=True)
        return carry + jnp.sum(both, axis=1, keepdims=True)

    counts = lax.fori_loop(0, n_chunks, rank_body, jnp.zeros((N_EXPERTS, 1), F32))
    tiles = jnp.zeros((N_EXPERTS, 1), F32)
    for k in range(n_tok * TOP_K // MOE_TM + 1):
        tiles = tiles + (counts > float(k * MOE_TM)).astype(F32)
    tiles_b = jnp.broadcast_to(tiles, (N_EXPERTS, META_LANES)).astype(BF16)
    lower = (lax.broadcasted_iota(I32, (N_EXPERTS, N_EXPERTS), 1)
             < lax.broadcasted_iota(I32, (N_EXPERTS, N_EXPERTS), 0)).astype(BF16)
    tile_start = _dot(lower, tiles_b)
    tile_end = tile_start + tiles
    pad_start = tile_start[:, 0:1] * float(MOE_TM)

    def dest_body(c, carry):
        off = pl.multiple_of(c * ch, ch)
        oh0, oh1 = one_hots(off)
        d0 = rank_scr[0:1, pl.ds(off, ch)] + jnp.sum(oh0 * pad_start, axis=0, keepdims=True)
        d1 = rank_scr[1:2, pl.ds(off, ch)] + jnp.sum(oh1 * pad_start, axis=0, keepdims=True)
        dest_ref[0:1, pl.ds(off, ch)] = d0.astype(I32)
        dest_ref[1:2, pl.ds(off, ch)] = d1.astype(I32)
        return carry

    lax.fori_loop(0, n_chunks, dest_body, 0)

    tile_id = lax.broadcasted_iota(I32, (N_EXPERTS, META_LANES), 1).astype(F32)
    tile_expert = jnp.sum((tile_end <= tile_id).astype(F32), axis=0, keepdims=True)
    tile_expert = jnp.minimum(tile_expert, float(N_EXPERTS - 1))
    inside = (tile_start <= tile_id) & (tile_id < tile_end)
    rows_here = jnp.clip(counts - float(MOE_TM) * (tile_id - tile_start), 0.0, float(MOE_TM))
    tile_rows = jnp.sum(jnp.where(inside, rows_here, 0.0), axis=0, keepdims=True)
    n_used = jnp.sum(tiles, axis=0, keepdims=True)
    row = lax.broadcasted_iota(I32, (8, META_LANES), 0)
    shape = (8, META_LANES)
    meta = jnp.where(row == META_TILE_EXPERT, jnp.broadcast_to(tile_expert, shape),
                     jnp.where(row == META_TILE_ROWS, jnp.broadcast_to(tile_rows, shape),
                               jnp.where(row == META_N_USED, jnp.broadcast_to(n_used, shape), 0.0)))
    meta_ref[...] = meta.astype(I32)


def _positions(e_all):
    n_tok = e_all.shape[1]
    return pl.pallas_call(
        functools.partial(_positions_kernel, n_tok=n_tok),
        out_shape=[jax.ShapeDtypeStruct((TOP_K, n_tok), I32),
                   jax.ShapeDtypeStruct((8, 128), I32)],
        scratch_shapes=[pltpu.VMEM((TOP_K, n_tok), F32)],
        compiler_params=pltpu.CompilerParams(vmem_limit_bytes=16 * 1024 * 1024),
        name="moe_positions",
    )(e_all)


def _dispatch_kernel(dest_ref, h_ref, xs_in_ref, xs_ref, sem, *, n_tok):
    del xs_in_ref
    tm = h_ref.shape[0]
    base = pl.program_id(0) * tm

    def row_copy(r, k):
        d = dest_ref[k * n_tok + base + r]
        return pltpu.make_async_copy(h_ref.at[pl.ds(r, 1), :], xs_ref.at[pl.ds(d, 1), :], sem)

    def start(r, carry):
        for k in range(TOP_K):
            row_copy(r, k).start(priority=k)
        return carry

    def wait(r, carry):
        for k in range(TOP_K):
            row_copy(r, k).wait()
        return carry

    lax.fori_loop(0, tm, start, 0)
    lax.fori_loop(0, tm, wait, 0)


def _dispatch(dest_flat, h2, xs):
    n = h2.shape[0]
    tm = ROW_TILE
    return pl.pallas_call(
        functools.partial(_dispatch_kernel, n_tok=n),
        grid_spec=pltpu.PrefetchScalarGridSpec(
            num_scalar_prefetch=1,
            grid=(n // tm,),
            in_specs=[pl.BlockSpec((tm, D_MODEL), lambda i, d: (i, 0)),
                      pl.BlockSpec(memory_space=pl.ANY)],
            out_specs=pl.BlockSpec(memory_space=pl.ANY),
            scratch_shapes=[pltpu.SemaphoreType.DMA(())],
        ),
        out_shape=jax.ShapeDtypeStruct(xs.shape, xs.dtype),
        input_output_aliases={2: 0},
        compiler_params=_cparams(("arbitrary",), 16),
        name="moe_dispatch",
    )(dest_flat, h2, xs)


def _moe_kernel(meta_ref, x_ref, wg_ref, wu_ref, wd_ref, y_in_ref, y_ref, xb_scr):
    del y_in_ref
    i = pl.program_id(0)
    f = pl.program_id(1)
    used = i < meta_ref[META_N_USED * META_LANES]
    rows = meta_ref[META_TILE_ROWS * META_LANES + i]
    n_sub = (rows + (MOE_SUB - 1)) >> MOE_SUB_LOG2

    @pl.when(used & (f == 0))
    def _():
        y_ref[...] = jnp.zeros_like(y_ref)

    for occupied in range(1, MOE_TM // MOE_SUB + 1):
        m_rows = occupied * MOE_SUB

        @pl.when(used & (n_sub == occupied))
        def _(m_rows=m_rows):
            @pl.when(f == 0)
            def _():
                xb_scr[0:m_rows, :] = x_ref[0:m_rows, :].astype(BF16)

            xb = xb_scr[0:m_rows, :]
            gate = _dot(xb, wg_ref[...].astype(BF16))
            up = _dot(xb, wu_ref[...].astype(BF16))
            hidden = (gate * jax.nn.sigmoid(gate) * up).astype(BF16)
            y_ref[0:m_rows, :] += _dot(hidden, wd_ref[...].astype(BF16))


def _moe_ffn(meta_flat, xs, w_gate, w_up, w_down, y_zero, layer, n_tiles):
    n_used_at = META_N_USED * META_LANES

    def last_used(i, m):
        return jnp.minimum(i, m[n_used_at] - 1)

    def f_eff(i, f, m):
        return jnp.where(i < m[n_used_at], f, MOE_NF - 1)

    return pl.pallas_call(
        _moe_kernel,
        grid_spec=pltpu.PrefetchScalarGridSpec(
            num_scalar_prefetch=1,
            grid=(n_tiles, MOE_NF),
            in_specs=[
                pl.BlockSpec((MOE_TM, D_MODEL), lambda i, f, m: (last_used(i, m), 0)),
                pl.BlockSpec((None, None, D_MODEL, MOE_TF),
                             lambda i, f, m: (layer, m[last_used(i, m)], 0, f_eff(i, f, m))),
                pl.BlockSpec((None, None, D_MODEL, MOE_TF),
                             lambda i, f, m: (layer, m[last_used(i, m)], 0, f_eff(i, f, m))),
                pl.BlockSpec((None, None, MOE_TF, D_MODEL),
                             lambda i, f, m: (layer, m[last_used(i, m)], f_eff(i, f, m), 0)),
                pl.BlockSpec(memory_space=pl.ANY),
            ],
            out_specs=pl.BlockSpec((MOE_TM, D_MODEL), lambda i, f, m: (last_used(i, m), 0)),
            scratch_shapes=[pltpu.VMEM((MOE_TM, D_MODEL), BF16)],
        ),
        out_shape=jax.ShapeDtypeStruct((n_tiles * MOE_TM, D_MODEL), F32),
        input_output_aliases={5: 0},
        compiler_params=_cparams(("arbitrary", "arbitrary"), 56),
        name="moe_ffn",
    )(meta_flat, xs, w_gate, w_up, w_down, y_zero)


def _combine_kernel(dest_ref, y_ref, gt_ref, x_ref, g2_ref, lng_ref, lnb_ref, o_ref,
                    buf_ref, sem, *, n_tok):
    tm = x_ref.shape[0]
    i = pl.program_id(0)

    def row_copy(tile, slot, r, k):
        d = dest_ref[k * n_tok + tile * tm + r]
        return pltpu.make_async_copy(y_ref.at[pl.ds(d, 1), :],
                                     buf_ref.at[slot, k, pl.ds(r, 1), :], sem.at[slot])

    def start_tile(tile, slot):
        def start(r, carry):
            for k in range(TOP_K):
                row_copy(tile, slot, r, k).start(priority=k)
            return carry
        lax.fori_loop(0, tm, start, 0)

    def wait_tile(tile, slot):
        def wait(r, carry):
            for k in range(TOP_K):
                row_copy(tile, slot, r, k).wait()
            return carry
        lax.fori_loop(0, tm, wait, 0)

    @pl.when(i == 0)
    def _():
        start_tile(i, 0)

    for slot in range(2):
        @pl.when((i + 1 < pl.num_programs(0)) & ((i + 1) % 2 == slot))
        def _(slot=slot):
            start_tile(i + 1, slot)

    for slot in range(2):
        @pl.when(i % 2 == slot)
        def _(slot=slot):
            wait_tile(i, slot)
            gt = gt_ref[...]
            moe = buf_ref[slot, 0] * gt[:, 0:1] + buf_ref[slot, 1] * gt[:, 1:2]
            z = ALPHA * x_ref[...] + g2_ref[...] * moe
            o_ref[...] = _layer_norm_rows(z) * lng_ref[...] + lnb_ref[...]


def _combine(dest_flat, y, gates_t, x_mid, g2, ln_g, ln_b):
    n = x_mid.shape[0]
    tm = ROW_TILE
    vec = pl.BlockSpec((1, D_MODEL), lambda i, d: (0, 0))
    return pl.pallas_call(
        functools.partial(_combine_kernel, n_tok=n),
        grid_spec=pltpu.PrefetchScalarGridSpec(
            num_scalar_prefetch=1,
            grid=(n // tm,),
            in_specs=[pl.BlockSpec(memory_space=pl.ANY),
                      pl.BlockSpec((tm, TOP_K), lambda i, d: (i, 0)),
                      pl.BlockSpec((tm, D_MODEL), lambda i, d: (i, 0)),
                      vec, vec, vec],
            out_specs=pl.BlockSpec((tm, D_MODEL), lambda i, d: (i, 0)),
            scratch_shapes=[pltpu.VMEM((2, TOP_K, tm, D_MODEL), F32),
                            pltpu.SemaphoreType.DMA((2,))],
        ),
        out_shape=jax.ShapeDtypeStruct((n, D_MODEL), F32),
        compiler_params=_cparams(("arbitrary",), 32),
        name="moe_combine",
    )(dest_flat, y, gates_t, x_mid, g2, ln_g, ln_b)


def _rope_tables():
    t = jnp.arange(SEQ, dtype=jnp.int32)
    row = (t // GRID_W).astype(F32)
    col = (t % GRID_W).astype(F32)
    inv_freq = ROPE_THETA ** (-jnp.arange(ROPE_PAIRS, dtype=F32) / ROPE_PAIRS)
    ang_r = row[:, None] * inv_freq
    ang_c = col[:, None] * inv_freq
    cos = jnp.concatenate([jnp.cos(ang_r), jnp.cos(ang_r), jnp.cos(ang_c), jnp.cos(ang_c)], axis=-1)
    sin = jnp.concatenate([-jnp.sin(ang_r), jnp.sin(ang_r), -jnp.sin(ang_c), jnp.sin(ang_c)], axis=-1)
    return cos, sin


def kernel(x, c, ctx, c_ctx, ada_w, ada_b, w_in, na_rpb, sg_ln_g, sg_ln_b, sg_w, sg_b, q_norm_g, k_norm_g, w_out, ln_mix_g, ln_mix_b, router_w, router_bias, moe_w_gate, moe_w_up, moe_w_down, ln_ffn_g, ln_ffn_b):
    cos, sin = _rope_tables()
    cos_ctx = jnp.ones((CTX_LEN, HEAD_DIM), F32)
    sin_ctx = jnp.zeros((CTX_LEN, HEAD_DIM), F32)

    cvec = jnp.zeros((8, D_MODEL), F32).at[0].set(c[0]).at[1].set(c_ctx)
    mods = _ada(cvec, ada_w, ada_b)

    na_bias = _na_bias_tables(na_rpb)
    rw_t = router_w.T.astype(BF16)
    rb_col = router_bias.reshape(N_EXPERTS, 1).astype(F32)

    xl = x[0]
    xc = ctx[0]
    for layer in range(DEPTH):
        last = layer == DEPTH - 1
        sh1, sc1, g1, sh2, sc2, g2 = jnp.split(mods[layer, 0:1], 6, axis=-1)
        csh1, csc1, cg1, csh2, csc2, cg2 = jnp.split(mods[layer, 1:2], 6, axis=-1)
        w_in_b = w_in[layer].astype(BF16)
        w_out_b = w_out[layer].astype(BF16)
        sg_w_b = sg_w[layer].astype(BF16)
        q_gain = q_norm_g[layer].reshape(1, HEAD_DIM)
        k_gain = k_norm_g[layer].reshape(1, HEAD_DIM)
        lng = ln_mix_g[layer].reshape(1, D_MODEL)
        lnb = ln_mix_b[layer].reshape(1, D_MODEL)
        fng = ln_ffn_g[layer].reshape(1, D_MODEL)
        fnb = ln_ffn_b[layer].reshape(1, D_MODEL)

        proj, sgz = _proj_in(xl, sh1, sc1, w_in_b, cos, sin, q_gain, k_gain, tm=1024)
        proj_c, sgz_c = _proj_in(xc, csh1, csc1, w_in_b, cos_ctx, sin_ctx, q_gain, k_gain, tm=CTX_LEN)

        o_na = _na_attention(proj, proj_c, na_bias, layer)
        o_sg = _spatial_gating(sgz, sg_ln_g[layer], sg_ln_b[layer], sg_w_b, sg_b[layer])
        q_t = proj[:, COL_GQ * HEAD_DIM:COL_GK * HEAD_DIM].T
        k_all = jnp.concatenate([proj[:, COL_GK * HEAD_DIM:COL_GV * HEAD_DIM],
                                 proj_c[:, COL_GK * HEAD_DIM:COL_GV * HEAD_DIM]], axis=0)
        v_t = jnp.concatenate([proj[:, COL_GV * HEAD_DIM:], proj_c[:, COL_GV * HEAD_DIM:]], axis=0).T
        n_tok = SEQ if last else SEQ + CTX_LEN
        n_tiles = _num_tiles(n_tok)
        o_gq_t, xs, y_zero = _gqa_attention(q_t, k_all, v_t, n_tiles * MOE_TM)
        x_mid, h2, e_all, gates = _proj_out(o_na, o_sg, o_gq_t.T, w_out_b, xl, g1, lng, lnb, sh2, sc2,
                                            rw_t, rb_col)

        if not last:
            c_na = _ctx_attention(proj_c, NA_HEADS, COL_NA_Q, COL_NA_K, COL_NA_V, 1, False)
            c_sg = _spatial_gating(sgz_c, sg_ln_g[layer], sg_ln_b[layer], sg_w_b, sg_b[layer])
            c_gq = _ctx_attention(proj_c, GQA_Q_HEADS, COL_GQ, COL_GK, COL_GV,
                                  GQA_Q_HEADS // GQA_KV_HEADS, True)
            xc_mid, h2_c, e_ctx, gates_ctx = _proj_out(c_na, c_sg, c_gq, w_out_b, xc, cg1, lng, lnb,
                                                       csh2, csc2, rw_t, rb_col)
            e_all = jnp.concatenate([e_all, e_ctx], axis=1)
            gates = jnp.concatenate([gates, gates_ctx], axis=1)

        dest, meta = _positions(e_all)
        meta_flat = meta.reshape(-1)
        xs = _dispatch(dest[:, :SEQ].reshape(-1), h2, xs)
        if not last:
            xs = _dispatch(dest[:, SEQ:].reshape(-1), h2_c, xs)
        y = _moe_ffn(meta_flat, xs, moe_w_gate, moe_w_up, moe_w_down, y_zero, layer, n_tiles)
        gates_t = gates.T
        xl = _combine(dest[:, :SEQ].reshape(-1), y, gates_t[:SEQ], x_mid, g2, fng, fnb)
        if not last:
            xc = _combine(dest[:, SEQ:].reshape(-1), y, gates_t[SEQ:], xc_mid, cg2, fng, fnb)
    return xl[None]
```

```python
import functools

import jax
import jax.numpy as jnp
import numpy as np
from jax import lax
from jax.experimental import pallas as pl
from jax.experimental.pallas import tpu as pltpu

F32 = jnp.float32
BF16 = jnp.bfloat16
I32 = jnp.int32

D_MODEL = 2048
SEQ = 8192
DEPTH = 2
GRID_W = 64
ROWS = SEQ // GRID_W
CTX_LEN = 256
HEAD_DIM = 128

NA_HEADS = 4
WIN_H = 8
WIN_W = 16
SG_GROUPS = 4
SG_CHUNK = 128
GQA_Q_HEADS = 8
GQA_KV_HEADS = 2
ROPE_THETA = 10000.0
ROPE_PAIRS = HEAD_DIM // 4

NA_WIDTH = NA_HEADS * HEAD_DIM
SG_WIDTH = SG_GROUPS * HEAD_DIM
GQA_WIDTH = GQA_Q_HEADS * HEAD_DIM
KV_WIDTH = GQA_KV_HEADS * HEAD_DIM
MIX_WIDTH = NA_WIDTH + SG_WIDTH + GQA_WIDTH
IN_WIDTH = 3 * NA_WIDTH + 2 * SG_WIDTH + GQA_WIDTH + 2 * KV_WIDTH

N_EXPERTS = 32
N_GROUPS = 4
EXPERTS_PER_GROUP = N_EXPERTS // N_GROUPS
TOP_K = 2
D_EXPERT = 1024

ALPHA = (2 * DEPTH) ** 0.25
LN_EPS = 1e-6
ATTN_SCALE = HEAD_DIM ** -0.5
LOG2E = 1.4426950408889634
NEG_BIG = -1e30
KEYS = SEQ + CTX_LEN

COL_NA_Q = 0
COL_NA_K = NA_HEADS
COL_NA_V = 2 * NA_HEADS
COL_SG = 3 * NA_HEADS
COL_GQ = COL_SG + 2 * SG_GROUPS
COL_GK = COL_GQ + GQA_Q_HEADS
COL_GV = COL_GK + GQA_KV_HEADS

PROJ_TN = 512
PROJ_NJ = IN_WIDTH // PROJ_TN
PROJ_ROWS = 256

NA_R = 4
NA_TQ = NA_R * GRID_W
NA_KR = NA_R + WIN_H - 1
NA_TK = NA_KR * GRID_W
NA_NB = ROWS // NA_R

MOE_TM = 768
MOE_SUB = 128
MOE_SUB_LOG2 = 7
MOE_TF = 512
MOE_NF = D_EXPERT // MOE_TF
META_TILE_EXPERT = 0
META_TILE_ROWS = 1
META_N_USED = 2
META_LANES = 128
POS_CHUNK = 256

ROW_TILE = 256


def _cparams(semantics, vmem_mib):
    return pltpu.CompilerParams(dimension_semantics=semantics,
                                vmem_limit_bytes=vmem_mib * 1024 * 1024)


def _layer_norm_rows(x):
    mu = jnp.mean(x, axis=-1, keepdims=True)
    xc = x - mu
    var = jnp.mean(xc * xc, axis=-1, keepdims=True)
    return xc * lax.rsqrt(var + LN_EPS)


def _dot(a, b):
    return jnp.dot(a, b, preferred_element_type=F32)


def _dot_nt(a, b):
    return lax.dot_general(a, b, (((1,), (1,)), ((), ())), preferred_element_type=F32)


ADA_TN = 1024


def _ada_kernel(c_ref, w_ref, b_ref, o_ref):
    cv = c_ref[...]
    act = (cv * jax.nn.sigmoid(cv)).astype(BF16)
    o_ref[...] = _dot(act, w_ref[...].astype(BF16)) + b_ref[...]


def _ada(cvec, ada_w, ada_b):
    n_out = 6 * D_MODEL
    return pl.pallas_call(
        _ada_kernel,
        grid=(DEPTH, n_out // ADA_TN),
        in_specs=[
            pl.BlockSpec((8, D_MODEL), lambda l, j: (0, 0)),
            pl.BlockSpec((None, D_MODEL, ADA_TN), lambda l, j: (l, 0, j)),
            pl.BlockSpec((None, 1, ADA_TN), lambda l, j: (l, 0, j)),
        ],
        out_specs=pl.BlockSpec((None, 8, ADA_TN), lambda l, j: (l, 0, j)),
        out_shape=jax.ShapeDtypeStruct((DEPTH, 8, n_out), F32),
        compiler_params=_cparams(("arbitrary", "arbitrary"), 40),
        name="ada_mod",
    )(cvec, ada_w, ada_b.reshape(DEPTH, 1, n_out))


def _rope_partner(y):
    lane = lax.broadcasted_iota(I32, y.shape, 1)
    first = (lane % (2 * ROPE_PAIRS)) < ROPE_PAIRS
    return jnp.where(first, pltpu.roll(y, HEAD_DIM - ROPE_PAIRS, 1), pltpu.roll(y, ROPE_PAIRS, 1))


def _norm_rope(zh, gain, cos, sin):
    ms = jnp.mean(zh * zh, axis=-1, keepdims=True)
    y = zh * lax.rsqrt(ms + LN_EPS) * gain
    return y * cos + _rope_partner(y) * sin


def _proj_in_kernel(x_ref, sh_ref, sc_ref, w_ref, cos_ref, sin_ref, qg_ref, kg_ref,
                    proj_ref, sgz_ref, h_scr):
    j = pl.program_id(1)
    tm = x_ref.shape[0]
    chunks = [slice(r, r + min(PROJ_ROWS, tm)) for r in range(0, tm, PROJ_ROWS)]

    def zed(rows):
        return _dot(h_scr[rows, :], w_ref[...])

    @pl.when(j == 0)
    def _():
        for rows in chunks:
            h = _layer_norm_rows(x_ref[rows, :]) * (1.0 + sc_ref[...]) + sh_ref[...]
            h_scr[rows, :] = h.astype(BF16)
            proj_ref[rows, :] = (zed(rows) * ATTN_SCALE).astype(BF16)

    @pl.when((j == 1) | (j == 2))
    def _():
        for rows in chunks:
            proj_ref[rows, :] = zed(rows).astype(BF16)

    @pl.when((j == 3) | (j == 4))
    def _():
        for rows in chunks:
            z = zed(rows)
            proj_ref[rows, :] = z.astype(BF16)
            sgz_ref[rows, :] = z

    @pl.when((j == 5) | (j == 6))
    def _():
        for rows in chunks:
            z = zed(rows)
            cos = cos_ref[rows, :]
            sin = sin_ref[rows, :]
            for hh in range(PROJ_TN // HEAD_DIM):
                cols = slice(hh * HEAD_DIM, (hh + 1) * HEAD_DIM)
                q = _norm_rope(z[:, cols], qg_ref[...], cos, sin)
                proj_ref[rows, cols] = (q * (ATTN_SCALE * LOG2E)).astype(BF16)

    @pl.when(j == 7)
    def _():
        for rows in chunks:
            z = zed(rows)
            cos = cos_ref[rows, :]
            sin = sin_ref[rows, :]
            for hh in range(GQA_KV_HEADS):
                cols = slice(hh * HEAD_DIM, (hh + 1) * HEAD_DIM)
                proj_ref[rows, cols] = _norm_rope(z[:, cols], kg_ref[...], cos, sin).astype(BF16)
            proj_ref[rows, KV_WIDTH:] = z[:, KV_WIDTH:].astype(BF16)


def _proj_in(x, sh, sc, w_bf16, cos, sin, q_gain, k_gain, tm):
    n = x.shape[0]
    return pl.pallas_call(
        _proj_in_kernel,
        grid=(n // tm, PROJ_NJ),
        in_specs=[
            pl.BlockSpec((tm, D_MODEL), lambda i, j: (i, 0)),
            pl.BlockSpec((1, D_MODEL), lambda i, j: (0, 0)),
            pl.BlockSpec((1, D_MODEL), lambda i, j: (0, 0)),
            pl.BlockSpec((D_MODEL, PROJ_TN), lambda i, j: (0, j)),
            pl.BlockSpec((tm, HEAD_DIM), lambda i, j: (i, 0)),
            pl.BlockSpec((tm, HEAD_DIM), lambda i, j: (i, 0)),
            pl.BlockSpec((1, HEAD_DIM), lambda i, j: (0, 0)),
            pl.BlockSpec((1, HEAD_DIM), lambda i, j: (0, 0)),
        ],
        out_specs=[
            pl.BlockSpec((tm, PROJ_TN), lambda i, j: (i, j)),
            pl.BlockSpec((tm, PROJ_TN), lambda i, j: (i, jnp.clip(j - 3, 0, 1))),
        ],
        out_shape=[
            jax.ShapeDtypeStruct((n, IN_WIDTH), BF16),
            jax.ShapeDtypeStruct((n, 2 * SG_WIDTH), F32),
        ],
        scratch_shapes=[pltpu.VMEM((tm, D_MODEL), BF16)],
        compiler_params=_cparams(("arbitrary", "arbitrary"), 48),
        name="proj_in",
    )(x, sh, sc, w_bf16, cos, sin, q_gain, k_gain)


def _na_kernel(q_ref, k_ref, v_ref, kc_ref, vc_ref, bias_ref, o_ref):
    b = pl.program_id(0)
    key_row0 = jnp.clip(b * NA_R - WIN_H // 2, 0, ROWS - NA_KR)
    start = pl.multiple_of(key_row0 * GRID_W, GRID_W)
    for h in range(NA_HEADS):
        cols = slice(h * HEAD_DIM, (h + 1) * HEAD_DIM)
        q = q_ref[:, cols]
        kw = k_ref[pl.ds(start, NA_TK), cols]
        vw = v_ref[pl.ds(start, NA_TK), cols]
        s_loc = _dot_nt(q, kw) + bias_ref[h]
        s_ctx = _dot_nt(q, kc_ref[:, cols])
        m = jnp.maximum(jnp.max(s_loc, axis=-1, keepdims=True), jnp.max(s_ctx, axis=-1, keepdims=True))
        p_loc = jnp.exp(s_loc - m)
        p_ctx = jnp.exp(s_ctx - m)
        denom = jnp.sum(p_loc, axis=-1, keepdims=True) + jnp.sum(p_ctx, axis=-1, keepdims=True)
        o = _dot(p_loc.astype(BF16), vw) + _dot(p_ctx.astype(BF16), vc_ref[:, cols])
        o_ref[:, cols] = (o / denom).astype(BF16)


NA_BIAS_LANE0 = 64
NA_BIAS_ROLL = HEAD_DIM - NA_BIAS_LANE0 - (WIN_W - 1)


def _na_bias_kernel(rows_ref, o_ref):
    c = lax.broadcasted_iota(I32, (GRID_W, GRID_W), 0)
    kc = lax.broadcasted_iota(I32, (GRID_W, GRID_W), 1)
    cs = jnp.clip(c - WIN_W // 2, 0, GRID_W - WIN_W)
    in_cols = (kc >= cs) & (kc < cs + WIN_W)
    for ri in range(NA_R):
        for kri in range(NA_KR):
            row = rows_ref[ri * NA_KR + kri:ri * NA_KR + kri + 1, :]
            spread = pltpu.roll(jnp.broadcast_to(row, (GRID_W, HEAD_DIM)), NA_BIAS_ROLL, 1,
                                stride=1, stride_axis=0)
            o_ref[ri * GRID_W:(ri + 1) * GRID_W, kri * GRID_W:(kri + 1) * GRID_W] = jnp.where(
                in_cols, spread[:, :GRID_W], NEG_BIG)


def _na_bias_tables(rpb):
    picked = []
    for r0, k0 in ((0, 0), (2 * NA_R, 2 * NA_R - WIN_H // 2), (ROWS - NA_R, ROWS - NA_KR)):
        r = r0 + np.arange(NA_R)
        rs = np.clip(r - WIN_H // 2, 0, ROWS - WIN_H)
        kr = k0 + np.arange(NA_KR)
        ok_r = (kr[None, :] >= rs[:, None]) & (kr[None, :] < rs[:, None] + WIN_H)
        drow = kr[None, :] - r[:, None] + (WIN_H - 1)
        pick_r = (np.clip(drow, 0, 2 * WIN_H - 2)[..., None] == np.arange(2 * WIN_H - 1)).astype(np.float32)
        rows = jnp.einsum('rka,lhab->lhrkb', pick_r, rpb.astype(F32), precision=lax.Precision.HIGHEST)
        rows = jnp.pad(rows, ((0, 0),) * 4 + ((NA_BIAS_LANE0, HEAD_DIM - NA_BIAS_LANE0 - (2 * WIN_W - 1)),))
        picked.append(jnp.where(ok_r[None, None, :, :, None], rows, NEG_BIG))
    rows_all = jnp.stack(picked, axis=1).reshape(DEPTH * 3 * NA_HEADS, NA_R * NA_KR, HEAD_DIM)
    tables = pl.pallas_call(
        _na_bias_kernel,
        grid=(DEPTH * 3 * NA_HEADS,),
        in_specs=[pl.BlockSpec((None, NA_R * NA_KR, HEAD_DIM), lambda t: (t, 0, 0))],
        out_specs=pl.BlockSpec((None, NA_TQ, NA_TK), lambda t: (t, 0, 0)),
        out_shape=jax.ShapeDtypeStruct((DEPTH * 3 * NA_HEADS, NA_TQ, NA_TK), F32),
        compiler_params=_cparams(("arbitrary",), 16),
        name="na_bias",
    )(rows_all)
    return tables.reshape(DEPTH, 3, NA_HEADS, NA_TQ, NA_TK)


def _na_attention(proj, proj_ctx, bias_tab, layer):
    def bias_idx(b):
        return (layer, jnp.where(b == 0, 0, jnp.where(b == NA_NB - 1, 2, 1)), 0, 0, 0)

    return pl.pallas_call(
        _na_kernel,
        grid=(NA_NB,),
        in_specs=[
            pl.BlockSpec((NA_TQ, NA_WIDTH), lambda b: (b, 0)),
            pl.BlockSpec((SEQ, NA_WIDTH), lambda b: (0, 1)),
            pl.BlockSpec((SEQ, NA_WIDTH), lambda b: (0, 2)),
            pl.BlockSpec((CTX_LEN, NA_WIDTH), lambda b: (0, 1)),
            pl.BlockSpec((CTX_LEN, NA_WIDTH), lambda b: (0, 2)),
            pl.BlockSpec((None, None, NA_HEADS, NA_TQ, NA_TK), bias_idx),
        ],
        out_specs=pl.BlockSpec((NA_TQ, NA_WIDTH), lambda b: (b, 0)),
        out_shape=jax.ShapeDtypeStruct((SEQ, NA_WIDTH), BF16),
        compiler_params=_cparams(("arbitrary",), 48),
        name="na_attn",
    )(proj, proj, proj, proj_ctx, proj_ctx, bias_tab)


def _ctx_attn_kernel(q_ref, k_ref, v_ref, o_ref, *, log2_domain):
    s = _dot_nt(q_ref[...], k_ref[...])
    m = jnp.max(s, axis=-1, keepdims=True)
    p = jnp.exp2(s - m) if log2_domain else jnp.exp(s - m)
    denom = jnp.sum(p, axis=-1, keepdims=True)
    o_ref[...] = (_dot(p.astype(BF16), v_ref[...]) / denom).astype(BF16)


def _ctx_attention(proj_ctx, n_heads, q_col, k_col, v_col, q_per_kv, log2_domain):
    return pl.pallas_call(
        functools.partial(_ctx_attn_kernel, log2_domain=log2_domain),
        grid=(n_heads,),
        in_specs=[
            pl.BlockSpec((CTX_LEN, HEAD_DIM), lambda h: (0, q_col + h)),
            pl.BlockSpec((CTX_LEN, HEAD_DIM), lambda h: (0, k_col + h // q_per_kv)),
            pl.BlockSpec((CTX_LEN, HEAD_DIM), lambda h: (0, v_col + h // q_per_kv)),
        ],
        out_specs=pl.BlockSpec((CTX_LEN, HEAD_DIM), lambda h: (0, h)),
        out_shape=jax.ShapeDtypeStruct((CTX_LEN, n_heads * HEAD_DIM), BF16),
        compiler_params=_cparams(("arbitrary",), 16),
        name="ctx_attn",
    )(proj_ctx, proj_ctx, proj_ctx)


SG_ROWS = 256


def _sg_kernel(zu_ref, zv_ref, g_ref, b_ref, w_ref, bs_ref, o_ref):
    for c in range(SG_ROWS // SG_CHUNK):
        rows = slice(c * SG_CHUNK, (c + 1) * SG_CHUNK)
        for g in range(SG_GROUPS):
            cols = slice(g * HEAD_DIM, (g + 1) * HEAD_DIM)
            v = jax.nn.gelu(zv_ref[rows, cols])
            vn = _layer_norm_rows(v) * g_ref[:, cols] + b_ref[:, cols]
            t = _dot(w_ref[g], vn.astype(BF16)) + bs_ref[:, g:g + 1]
            o_ref[rows, cols] = (jax.nn.gelu(zu_ref[rows, cols]) * t).astype(BF16)


def _spatial_gating(sgz, ln_g, ln_b, w_bf16, b_s):
    n = sgz.shape[0]
    return pl.pallas_call(
        _sg_kernel,
        grid=(n // SG_ROWS,),
        in_specs=[
            pl.BlockSpec((SG_ROWS, SG_WIDTH), lambda i: (i, 0)),
            pl.BlockSpec((SG_ROWS, SG_WIDTH), lambda i: (i, 1)),
            pl.BlockSpec((1, SG_WIDTH), lambda i: (0, 0)),
            pl.BlockSpec((1, SG_WIDTH), lambda i: (0, 0)),
            pl.BlockSpec((SG_GROUPS, SG_CHUNK, SG_CHUNK), lambda i: (0, 0, 0)),
            pl.BlockSpec((SG_CHUNK, SG_GROUPS), lambda i: (0, 0)),
        ],
        out_specs=pl.BlockSpec((SG_ROWS, SG_WIDTH), lambda i: (i, 0)),
        out_shape=jax.ShapeDtypeStruct((n, SG_WIDTH), BF16),
        compiler_params=_cparams(("arbitrary",), 16),
        name="spatial_gating",
    )(sgz, sgz, ln_g.reshape(1, SG_WIDTH), ln_b.reshape(1, SG_WIDTH), w_bf16, b_s.T)


GQA_TQ = 512
GQA_TK = 512


def _gqa_kernel(qt_ref, k_ref, vt_ref, ot_ref, slots_x_ref, slots_y_ref, sa_ref, sb_ref, sc_ref,
                zero_ref, zero_sem):
    qt = qt_ref[...]
    n_chunks = SEQ // GQA_TK

    step = pl.program_id(0) * pl.num_programs(1) + pl.program_id(1)
    n_steps = pl.num_programs(0) * pl.num_programs(1)
    n_zero_blocks = slots_x_ref.shape[0] // MOE_TM

    def zero_copy(dst_ref, c):
        off = pl.multiple_of(c * MOE_TM, MOE_TM)
        return pltpu.make_async_copy(zero_ref, dst_ref.at[pl.ds(off, MOE_TM), :], zero_sem)

    @pl.when(step == 0)
    def _():
        zero_ref[...] = jnp.zeros_like(zero_ref)

    @pl.when(step < n_zero_blocks)
    def _():
        zero_copy(slots_x_ref, step).start()

    @pl.when((step >= n_zero_blocks) & (step < 2 * n_zero_blocks))
    def _():
        zero_copy(slots_y_ref, step - n_zero_blocks).start()

    def scores(t, dst_ref):
        off = pl.multiple_of(t * GQA_TK, GQA_TK)
        s = _dot(k_ref[pl.ds(off, GQA_TK), :], qt)
        dst_ref[...] = s
        return jnp.max(s, axis=0, keepdims=True)

    def consume(t, src_ref, m_chunk, m, l, acc):
        off = pl.multiple_of(t * GQA_TK, GQA_TK)
        m_new = jnp.maximum(m, m_chunk)
        alpha = jnp.exp2(m - m_new)
        p = jnp.exp2(src_ref[...] - m_new)
        l = alpha * l + jnp.sum(p, axis=0, keepdims=True)
        acc = alpha * acc + _dot(vt_ref[:, pl.ds(off, GQA_TK)], p.astype(BF16))
        return m_new, l, acc

    m = jnp.full((1, GQA_TQ), -jnp.inf, F32)
    l = jnp.zeros((1, GQA_TQ), F32)
    acc = jnp.zeros((HEAD_DIM, GQA_TQ), F32)
    mc_a = scores(0, sa_ref)

    def body(u, carry):
        m, l, acc, mc_a = carry
        mc_b = scores(2 * u + 1, sb_ref)
        m, l, acc = consume(2 * u, sa_ref, mc_a, m, l, acc)
        mc_a = scores(2 * u + 2, sa_ref)
        m, l, acc = consume(2 * u + 1, sb_ref, mc_b, m, l, acc)
        return m, l, acc, mc_a

    m, l, acc, mc_a = lax.fori_loop(0, n_chunks // 2 - 1, body, (m, l, acc, mc_a))
    mc_b = scores(n_chunks - 1, sb_ref)
    m, l, acc = consume(n_chunks - 2, sa_ref, mc_a, m, l, acc)
    s_ctx = _dot(k_ref[pl.ds(SEQ, CTX_LEN), :], qt)
    sc_ref[...] = s_ctx
    mc_ctx = jnp.max(s_ctx, axis=0, keepdims=True)
    m, l, acc = consume(n_chunks - 1, sb_ref, mc_b, m, l, acc)
    m_new = jnp.maximum(m, mc_ctx)
    alpha = jnp.exp2(m - m_new)
    p = jnp.exp2(sc_ref[...] - m_new)
    l = alpha * l + jnp.sum(p, axis=0, keepdims=True)
    acc = alpha * acc + _dot(vt_ref[:, pl.ds(SEQ, CTX_LEN)], p.astype(BF16))
    ot_ref[...] = (acc / l).astype(BF16)

    @pl.when(step == n_steps - 1)
    def _():
        def wait(c, carry):
            zero_copy(slots_x_ref, c).wait()
            zero_copy(slots_y_ref, c).wait()
            return carry
        lax.fori_loop(0, n_zero_blocks, wait, 0)


def _gqa_attention(q_t, k_all, v_t, n_slot_rows):
    grp = GQA_Q_HEADS // GQA_KV_HEADS
    grid = (GQA_Q_HEADS, SEQ // GQA_TQ)
    assert n_slot_rows % MOE_TM == 0 and 2 * (n_slot_rows // MOE_TM) <= grid[0] * grid[1]
    return pl.pallas_call(
        _gqa_kernel,
        grid=grid,
        in_specs=[
            pl.BlockSpec((HEAD_DIM, GQA_TQ), lambda h, i: (h, i)),
            pl.BlockSpec((KEYS, HEAD_DIM), lambda h, i: (0, h // grp)),
            pl.BlockSpec((HEAD_DIM, KEYS), lambda h, i: (h // grp, 0)),
        ],
        out_specs=[pl.BlockSpec((HEAD_DIM, GQA_TQ), lambda h, i: (h, i)),
                   pl.BlockSpec(memory_space=pl.ANY),
                   pl.BlockSpec(memory_space=pl.ANY)],
        out_shape=[jax.ShapeDtypeStruct((GQA_WIDTH, SEQ), BF16),
                   jax.ShapeDtypeStruct((n_slot_rows, D_MODEL), F32),
                   jax.ShapeDtypeStruct((n_slot_rows, D_MODEL), F32)],
        scratch_shapes=[pltpu.VMEM((GQA_TK, GQA_TQ), F32), pltpu.VMEM((GQA_TK, GQA_TQ), F32),
                        pltpu.VMEM((CTX_LEN, GQA_TQ), F32),
                        pltpu.VMEM((MOE_TM, D_MODEL), F32), pltpu.SemaphoreType.DMA(())],
        compiler_params=_cparams(("arbitrary", "arbitrary"), 32),
        name="gqa_attn",
    )(q_t, k_all, v_t)


def _proj_out_kernel(ona_ref, osg_ref, ogq_ref, w_ref, x_ref, g1_ref, lng_ref, lnb_ref,
                     sh2_ref, sc2_ref, rwt_ref, rb_ref, xmid_ref, h2_ref, e_ref, gate_ref, y_scr):
    i = pl.program_id(0)

    @pl.when(i == 0)
    def _():
        y_scr[1] = jnp.zeros_like(y_scr[1])

    for parity in range(2):
        @pl.when(i % 2 == parity)
        def _(parity=parity):
            y_scr[parity] = (_dot(ona_ref[...], w_ref[0:NA_WIDTH, :])
                             + _dot(osg_ref[...], w_ref[NA_WIDTH:NA_WIDTH + SG_WIDTH, :])
                             + _dot(ogq_ref[...], w_ref[NA_WIDTH + SG_WIDTH:, :]))
            z = ALPHA * x_ref[...] + g1_ref[...] * y_scr[1 - parity]
            xm = _layer_norm_rows(z) * lng_ref[...] + lnb_ref[...]
            xmid_ref[...] = xm
            h2 = _layer_norm_rows(xm) * (1.0 + sc2_ref[...]) + sh2_ref[...]
            h2_ref[...] = h2
            e_ref[...], gate_ref[...] = _route(h2, rwt_ref[...], rb_ref[...])


def _proj_out(o_na, o_sg, o_gq, w_bf16, x, g1, ln_g, ln_b, sh2, sc2, rw_t_bf16, rb_col):
    n = x.shape[0]
    tm = ROW_TILE
    n_tiles = n // tm
    vec = pl.BlockSpec((1, D_MODEL), lambda i: (0, 0))

    def mm_tile(i):
        return jnp.minimum(i, n_tiles - 1)

    def ep_tile(i):
        return jnp.maximum(i - 1, 0)

    return pl.pallas_call(
        _proj_out_kernel,
        grid=(n_tiles + 1,),
        in_specs=[
            pl.BlockSpec((tm, NA_WIDTH), lambda i: (mm_tile(i), 0)),
            pl.BlockSpec((tm, SG_WIDTH), lambda i: (mm_tile(i), 0)),
            pl.BlockSpec((tm, GQA_WIDTH), lambda i: (mm_tile(i), 0)),
            pl.BlockSpec((MIX_WIDTH, D_MODEL), lambda i: (0, 0)),
            pl.BlockSpec((tm, D_MODEL), lambda i: (ep_tile(i), 0)),
            vec, vec, vec, vec, vec,
            pl.BlockSpec((N_EXPERTS, D_MODEL), lambda i: (0, 0)),
            pl.BlockSpec((N_EXPERTS, 1), lambda i: (0, 0)),
        ],
        out_specs=[pl.BlockSpec((tm, D_MODEL), lambda i: (ep_tile(i), 0)),
                   pl.BlockSpec((tm, D_MODEL), lambda i: (ep_tile(i), 0)),
                   pl.BlockSpec((TOP_K, tm), lambda i: (0, ep_tile(i))),
                   pl.BlockSpec((TOP_K, tm), lambda i: (0, ep_tile(i)))],
        out_shape=[jax.ShapeDtypeStruct((n, D_MODEL), F32),
                   jax.ShapeDtypeStruct((n, D_MODEL), F32),
                   jax.ShapeDtypeStruct((TOP_K, n), I32),
                   jax.ShapeDtypeStruct((TOP_K, n), F32)],
        scratch_shapes=[pltpu.VMEM((2, tm, D_MODEL), F32)],
        compiler_params=_cparams(("arbitrary",), 48),
        name="proj_out",
    )(o_na, o_sg, o_gq, w_bf16, x, g1, ln_g, ln_b, sh2, sc2, rw_t_bf16, rb_col)


def _route(h, rw_t, rb_col):
    tm = h.shape[0]
    logits = _dot_nt(rw_t, h.astype(BF16))
    scores = jax.nn.sigmoid(logits)
    sel = scores + rb_col
    sub = lax.broadcasted_iota(I32, (EXPERTS_PER_GROUP, tm), 0)
    best = None
    for g in range(N_GROUPS):
        rows = slice(g * EXPERTS_PER_GROUP, (g + 1) * EXPERTS_PER_GROUP)
        v = sel[rows, :]
        sc = scores[rows, :]
        m1 = jnp.max(v, axis=0, keepdims=True)
        i1 = jnp.min(jnp.where(v == m1, sub, EXPERTS_PER_GROUP), axis=0, keepdims=True)
        v2 = jnp.where(sub == i1, -jnp.inf, v)
        m2 = jnp.max(v2, axis=0, keepdims=True)
        i2 = jnp.min(jnp.where(v2 == m2, sub, EXPERTS_PER_GROUP), axis=0, keepdims=True)
        s1 = jnp.sum(jnp.where(sub == i1, sc, 0.0), axis=0, keepdims=True)
        s2 = jnp.sum(jnp.where(sub == i2, sc, 0.0), axis=0, keepdims=True)
        cand = (m1 + m2, i1 + g * EXPERTS_PER_GROUP, i2 + g * EXPERTS_PER_GROUP, s1, s2)
        if best is None:
            best = cand
        else:
            better = cand[0] > best[0]
            best = tuple(jnp.where(better, cn, bs) for cn, bs in zip(cand, best))
    _, e1, e2, s1, s2 = best
    tot = s1 + s2
    return jnp.concatenate([e1, e2], axis=0), jnp.concatenate([s1 / tot, s2 / tot], axis=0)


def _num_tiles(n_tok):
    return (n_tok * TOP_K + N_EXPERTS * (MOE_TM - 1) + MOE_TM - 1) // MOE_TM


def _positions_kernel(e_ref, dest_ref, meta_ref, rank_scr, *, n_tok):
    ch = POS_CHUNK
    n_chunks = n_tok // ch
    iota_e = lax.broadcasted_iota(I32, (N_EXPERTS, ch), 0)
    upper = (lax.broadcasted_iota(I32, (ch, ch), 0) <= lax.broadcasted_iota(I32, (ch, ch), 1)).astype(BF16)

    def one_hots(off):
        e0 = e_ref[0:1, pl.ds(off, ch)]
        e1 = e_ref[1:2, pl.ds(off, ch)]
        return (iota_e == e0).astype(F32), (iota_e == e1).astype(F32)

    def rank_body(c, carry):
        off = pl.multiple_of(c * ch, ch)
        oh0, oh1 = one_hots(off)
        both = oh0 + oh1
        incl = _dot(both.astype(BF16), upper)
        before = incl - both + carry
        rank_scr[0:1, pl.ds(off, ch)] = jnp.sum(oh0 * before, axis=0, keepdims=True)
        rank_scr[1:2, pl.ds(off, ch)] = jnp.sum(oh1 * (before + oh0), axis=0, keepdims=True)
        return carry + jnp.sum(both, axis=1, keepdims=True)

    counts = lax.fori_loop(0, n_chunks, rank_body, jnp.zeros((N_EXPERTS, 1), F32))
    tiles = jnp.zeros((N_EXPERTS, 1), F32)
    for k in range(n_tok * TOP_K // MOE_TM + 1):
        tiles = tiles + (counts > float(k * MOE_TM)).astype(F32)
    tiles_b = jnp.broadcast_to(tiles, (N_EXPERTS, META_LANES)).astype(BF16)
    lower = (lax.broadcasted_iota(I32, (N_EXPERTS, N_EXPERTS), 1)
             < lax.broadcasted_iota(I32, (N_EXPERTS, N_EXPERTS), 0)).astype(BF16)
    tile_start = _dot(lower, tiles_b)
    tile_end = tile_start + tiles
    pad_start = tile_start[:, 0:1] * float(MOE_TM)

    def dest_body(c, carry):
        off = pl.multiple_of(c * ch, ch)
        oh0, oh1 = one_hots(off)
        d0 = rank_scr[0:1, pl.ds(off, ch)] + jnp.sum(oh0 * pad_start, axis=0, keepdims=True)
        d1 = rank_scr[1:2, pl.ds(off, ch)] + jnp.sum(oh1 * pad_start, axis=0, keepdims=True)
        dest_ref[0:1, pl.ds(off, ch)] = d0.astype(I32)
        dest_ref[1:2, pl.ds(off, ch)] = d1.astype(I32)
        return carry

    lax.fori_loop(0, n_chunks, dest_body, 0)

    tile_id = lax.broadcasted_iota(I32, (N_EXPERTS, META_LANES), 1).astype(F32)
    tile_expert = jnp.sum((tile_end <= tile_id).astype(F32), axis=0, keepdims=True)
    tile_expert = jnp.minimum(tile_expert, float(N_EXPERTS - 1))
    inside = (tile_start <= tile_id) & (tile_id < tile_end)
    rows_here = jnp.clip(counts - float(MOE_TM) * (tile_id - tile_start), 0.0, float(MOE_TM))
    tile_rows = jnp.sum(jnp.where(inside, rows_here, 0.0), axis=0, keepdims=True)
    n_used = jnp.sum(tiles, axis=0, keepdims=True)
    row = lax.broadcasted_iota(I32, (8, META_LANES), 0)
    shape = (8, META_LANES)
    meta = jnp.where(row == META_TILE_EXPERT, jnp.broadcast_to(tile_expert, shape),
                     jnp.where(row == META_TILE_ROWS, jnp.broadcast_to(tile_rows, shape),
                               jnp.where(row == META_N_USED, jnp.broadcast_to(n_used, shape), 0.0)))
    meta_ref[...] = meta.astype(I32)


def _positions(e_all):
    n_tok = e_all.shape[1]
    return pl.pallas_call(
        functools.partial(_positions_kernel, n_tok=n_tok),
        out_shape=[jax.ShapeDtypeStruct((TOP_K, n_tok), I32),
                   jax.ShapeDtypeStruct((8, 128), I32)],
        scratch_shapes=[pltpu.VMEM((TOP_K, n_tok), F32)],
        compiler_params=pltpu.CompilerParams(vmem_limit_bytes=16 * 1024 * 1024),
        name="moe_positions",
    )(e_all)


def _dispatch_kernel(dest_ref, h_ref, xs_in_ref, xs_ref, sem, *, n_tok):
    del xs_in_ref
    tm = h_ref.shape[0]
    base = pl.program_id(0) * tm

    def row_copy(r, k):
        d = dest_ref[k * n_tok + base + r]
        return pltpu.make_async_copy(h_ref.at[pl.ds(r, 1), :], xs_ref.at[pl.ds(d, 1), :], sem)

    def start(r, carry):
        for k in range(TOP_K):
            row_copy(r, k).start(priority=k)
        return carry

    def wait(r, carry):
        for k in range(TOP_K):
            row_copy(r, k).wait()
        return carry

    lax.fori_loop(0, tm, start, 0)
    lax.fori_loop(0, tm, wait, 0)


def _dispatch(dest_flat, h2, xs):
    n = h2.shape[0]
    tm = ROW_TILE
    return pl.pallas_call(
        functools.partial(_dispatch_kernel, n_tok=n),
        grid_spec=pltpu.PrefetchScalarGridSpec(
            num_scalar_prefetch=1,
            grid=(n // tm,),
            in_specs=[pl.BlockSpec((tm, D_MODEL), lambda i, d: (i, 0)),
                      pl.BlockSpec(memory_space=pl.ANY)],
            out_specs=pl.BlockSpec(memory_space=pl.ANY),
            scratch_shapes=[pltpu.SemaphoreType.DMA(())],
        ),
        out_shape=jax.ShapeDtypeStruct(xs.shape, xs.dtype),
        input_output_aliases={2: 0},
        compiler_params=_cparams(("arbitrary",), 16),
        name="moe_dispatch",
    )(dest_flat, h2, xs)


def _moe_kernel(meta_ref, x_ref, wg_ref, wu_ref, wd_ref, y_in_ref, y_ref, xb_scr):
    del y_in_ref
    i = pl.program_id(0)
    f = pl.program_id(1)
    used = i < meta_ref[META_N_USED * META_LANES]
    rows = meta_ref[META_TILE_ROWS * META_LANES + i]
    n_sub = (rows + (MOE_SUB - 1)) >> MOE_SUB_LOG2

    @pl.when(used & (f == 0))
    def _():
        y_ref[...] = jnp.zeros_like(y_ref)

    for occupied in range(1, MOE_TM // MOE_SUB + 1):
        m_rows = occupied * MOE_SUB

        @pl.when(used & (n_sub == occupied))
        def _(m_rows=m_rows):
            @pl.when(f == 0)
            def _():
                xb_scr[0:m_rows, :] = x_ref[0:m_rows, :].astype(BF16)

            xb = xb_scr[0:m_rows, :]
            gate = _dot(xb, wg_ref[...].astype(BF16))
            up = _dot(xb, wu_ref[...].astype(BF16))
            hidden = (gate * jax.nn.sigmoid(gate) * up).astype(BF16)
            y_ref[0:m_rows, :] += _dot(hidden, wd_ref[...].astype(BF16))


def _moe_ffn(meta_flat, xs, w_gate, w_up, w_down, y_zero, layer, n_tiles):
    n_used_at = META_N_USED * META_LANES

    def last_used(i, m):
        return jnp.minimum(i, m[n_used_at] - 1)

    def f_eff(i, f, m):
        return jnp.where(i < m[n_used_at], f, MOE_NF - 1)

    return pl.pallas_call(
        _moe_kernel,
        grid_spec=pltpu.PrefetchScalarGridSpec(
            num_scalar_prefetch=1,
            grid=(n_tiles, MOE_NF),
            in_specs=[
                pl.BlockSpec((MOE_TM, D_MODEL), lambda i, f, m: (last_used(i, m), 0)),
                pl.BlockSpec((None, None, D_MODEL, MOE_TF),
                             lambda i, f, m: (layer, m[last_used(i, m)], 0, f_eff(i, f, m))),
                pl.BlockSpec((None, None, D_MODEL, MOE_TF),
                             lambda i, f, m: (layer, m[last_used(i, m)], 0, f_eff(i, f, m))),
                pl.BlockSpec((None, None, MOE_TF, D_MODEL),
                             lambda i, f, m: (layer, m[last_used(i, m)], f_eff(i, f, m), 0)),
                pl.BlockSpec(memory_space=pl.ANY),
            ],
            out_specs=pl.BlockSpec((MOE_TM, D_MODEL), lambda i, f, m: (last_used(i, m), 0)),
            scratch_shapes=[pltpu.VMEM((MOE_TM, D_MODEL), BF16)],
        ),
        out_shape=jax.ShapeDtypeStruct((n_tiles * MOE_TM, D_MODEL), F32),
        input_output_aliases={5: 0},
        compiler_params=_cparams(("arbitrary", "arbitrary"), 56),
        name="moe_ffn",
    )(meta_flat, xs, w_gate, w_up, w_down, y_zero)


def _combine_kernel(dest_ref, y_ref, gt_ref, x_ref, g2_ref, lng_ref, lnb_ref, o_ref,
                    buf_ref, sem, *, n_tok):
    tm = x_ref.shape[0]
    i = pl.program_id(0)

    def row_copy(tile, slot, r, k):
        d = dest_ref[k * n_tok + tile * tm + r]
        return pltpu.make_async_copy(y_ref.at[pl.ds(d, 1), :],
                                     buf_ref.at[slot, k, pl.ds(r, 1), :], sem.at[slot])

    def start_tile(tile, slot):
        def start(r, carry):
            for k in range(TOP_K):
                row_copy(tile, slot, r, k).start(priority=k)
            return carry
        lax.fori_loop(0, tm, start, 0)

    def wait_tile(tile, slot):
        def wait(r, carry):
            for k in range(TOP_K):
                row_copy(tile, slot, r, k).wait()
            return carry
        lax.fori_loop(0, tm, wait, 0)

    @pl.when(i == 0)
    def _():
        start_tile(i, 0)

    for slot in range(2):
        @pl.when((i + 1 < pl.num_programs(0)) & ((i + 1) % 2 == slot))
        def _(slot=slot):
            start_tile(i + 1, slot)

    for slot in range(2):
        @pl.when(i % 2 == slot)
        def _(slot=slot):
            wait_tile(i, slot)
            gt = gt_ref[...]
            moe = buf_ref[slot, 0] * gt[:, 0:1] + buf_ref[slot, 1] * gt[:, 1:2]
            z = ALPHA * x_ref[...] + g2_ref[...] * moe
            o_ref[...] = _layer_norm_rows(z) * lng_ref[...] + lnb_ref[...]


def _combine(dest_flat, y, gates_t, x_mid, g2, ln_g, ln_b):
    n = x_mid.shape[0]
    tm = ROW_TILE
    vec = pl.BlockSpec((1, D_MODEL), lambda i, d: (0, 0))
    return pl.pallas_call(
        functools.partial(_combine_kernel, n_tok=n),
        grid_spec=pltpu.PrefetchScalarGridSpec(
            num_scalar_prefetch=1,
            grid=(n // tm,),
            in_specs=[pl.BlockSpec(memory_space=pl.ANY),
                      pl.BlockSpec((tm, TOP_K), lambda i, d: (i, 0)),
                      pl.BlockSpec((tm, D_MODEL), lambda i, d: (i, 0)),
                      vec, vec, vec],
            out_specs=pl.BlockSpec((tm, D_MODEL), lambda i, d: (i, 0)),
            scratch_shapes=[pltpu.VMEM((2, TOP_K, tm, D_MODEL), F32),
                            pltpu.SemaphoreType.DMA((2,))],
        ),
        out_shape=jax.ShapeDtypeStruct((n, D_MODEL), F32),
        compiler_params=_cparams(("arbitrary",), 32),
        name="moe_combine",
    )(dest_flat, y, gates_t, x_mid, g2, ln_g, ln_b)


def _rope_tables():
    t = jnp.arange(SEQ, dtype=jnp.int32)
    row = (t // GRID_W).astype(F32)
    col = (t % GRID_W).astype(F32)
    inv_freq = ROPE_THETA ** (-jnp.arange(ROPE_PAIRS, dtype=F32) / ROPE_PAIRS)
    ang_r = row[:, None] * inv_freq
    ang_c = col[:, None] * inv_freq
    cos = jnp.concatenate([jnp.cos(ang_r), jnp.cos(ang_r), jnp.cos(ang_c), jnp.cos(ang_c)], axis=-1)
    sin = jnp.concatenate([-jnp.sin(ang_r), jnp.sin(ang_r), -jnp.sin(ang_c), jnp.sin(ang_c)], axis=-1)
    return cos, sin


def kernel(x, c, ctx, c_ctx, ada_w, ada_b, w_in, na_rpb, sg_ln_g, sg_ln_b, sg_w, sg_b, q_norm_g, k_norm_g, w_out, ln_mix_g, ln_mix_b, router_w, router_bias, moe_w_gate, moe_w_up, moe_w_down, ln_ffn_g, ln_ffn_b):
    cos, sin = _rope_tables()
    cos_ctx = jnp.ones((CTX_LEN, HEAD_DIM), F32)
    sin_ctx = jnp.zeros((CTX_LEN, HEAD_DIM), F32)

    cvec = jnp.zeros((8, D_MODEL), F32).at[0].set(c[0]).at[1].set(c_ctx)
    mods = _ada(cvec, ada_w, ada_b)

    na_bias = _na_bias_tables(na_rpb)
    rw_t = router_w.T.astype(BF16)
    rb_col = router_bias.reshape(N_EXPERTS, 1).astype(F32)

    xl = x[0]
    xc = ctx[0]
    for layer in range(DEPTH):
        last = layer == DEPTH - 1
        sh1, sc1, g1, sh2, sc2, g2 = jnp.split(mods[layer, 0:1], 6, axis=-1)
        csh1, csc1, cg1, csh2, csc2, cg2 = jnp.split(mods[layer, 1:2], 6, axis=-1)
        w_in_b = w_in[layer].astype(BF16)
        w_out_b = w_out[layer].astype(BF16)
        sg_w_b = sg_w[layer].astype(BF16)
        q_gain = q_norm_g[layer].reshape(1, HEAD_DIM)
        k_gain = k_norm_g[layer].reshape(1, HEAD_DIM)
        lng = ln_mix_g[layer].reshape(1, D_MODEL)
        lnb = ln_mix_b[layer].reshape(1, D_MODEL)
        fng = ln_ffn_g[layer].reshape(1, D_MODEL)
        fnb = ln_ffn_b[layer].reshape(1, D_MODEL)

        proj, sgz = _proj_in(xl, sh1, sc1, w_in_b, cos, sin, q_gain, k_gain, tm=1024)
        proj_c, sgz_c = _proj_in(xc, csh1, csc1, w_in_b, cos_ctx, sin_ctx, q_gain, k_gain, tm=CTX_LEN)

        o_na = _na_attention(proj, proj_c, na_bias, layer)
        o_sg = _spatial_gating(sgz, sg_ln_g[layer], sg_ln_b[layer], sg_w_b, sg_b[layer])
        q_t = proj[:, COL_GQ * HEAD_DIM:COL_GK * HEAD_DIM].T
        k_all = jnp.concatenate([proj[:, COL_GK * HEAD_DIM:COL_GV * HEAD_DIM],
                                 proj_c[:, COL_GK * HEAD_DIM:COL_GV * HEAD_DIM]], axis=0)
        v_t = jnp.concatenate([proj[:, COL_GV * HEAD_DIM:], proj_c[:, COL_GV * HEAD_DIM:]], axis=0).T
        n_tok = SEQ if last else SEQ + CTX_LEN
        n_tiles = _num_tiles(n_tok)
        o_gq_t, xs, y_zero = _gqa_attention(q_t, k_all, v_t, n_tiles * MOE_TM)
        x_mid, h2, e_all, gates = _proj_out(o_na, o_sg, o_gq_t.T, w_out_b, xl, g1, lng, lnb, sh2, sc2,
                                            rw_t, rb_col)

        if not last:
            c_na = _ctx_attention(proj_c, NA_HEADS, COL_NA_Q, COL_NA_K, COL_NA_V, 1, False)
            c_sg = _spatial_gating(sgz_c, sg_ln_g[layer], sg_ln_b[layer], sg_w_b, sg_b[layer])
            c_gq = _ctx_attention(proj_c, GQA_Q_HEADS, COL_GQ, COL_GK, COL_GV,
                                  GQA_Q_HEADS // GQA_KV_HEADS, True)
            xc_mid, h2_c, e_ctx, gates_ctx = _proj_out(c_na, c_sg, c_gq, w_out_b, xc, cg1, lng, lnb,
                                                       csh2, csc2, rw_t, rb_col)
            e_all = jnp.concatenate([e_all, e_ctx], axis=1)
            gates = jnp.concatenate([gates, gates_ctx], axis=1)

        dest, meta = _positions(e_all)
        meta_flat = meta.reshape(-1)
        xs = _dispatch(dest[:, :SEQ].reshape(-1), h2, xs)
        if not last:
            xs = _dispatch(dest[:, SEQ:].reshape(-1), h2_c, xs)
        y = _moe_ffn(meta_flat, xs, moe_w_gate, moe_w_up, moe_w_down, y_zero, layer, n_tiles)
        gates_t = gates.T
        xl = _combine(dest[:, :SEQ].reshape(-1), y, gates_t[:SEQ], x_mid, g2, fng, fnb)
        if not last:
            xc = _combine(dest[:, SEQ:].reshape(-1), y, gates_t[SEQ:], xc_mid, cg2, fng, fnb)
    return xl[None]
```

```python
import functools

import jax
import jax.numpy as jnp
import numpy as np
from jax import lax
from jax.experimental import pallas as pl
from jax.experimental.pallas import tpu as pltpu

F32 = jnp.float32
BF16 = jnp.bfloat16
I32 = jnp.int32

D_MODEL = 2048
SEQ = 8192
DEPTH = 2
GRID_W = 64
ROWS = SEQ // GRID_W
CTX_LEN = 256
HEAD_DIM = 128

NA_HEADS = 4
WIN_H = 8
WIN_W = 16
SG_GROUPS = 4
SG_CHUNK = 128
GQA_Q_HEADS = 8
GQA_KV_HEADS = 2
ROPE_THETA = 10000.0
ROPE_PAIRS = HEAD_DIM // 4

NA_WIDTH = NA_HEADS * HEAD_DIM
SG_WIDTH = SG_GROUPS * HEAD_DIM
GQA_WIDTH = GQA_Q_HEADS * HEAD_DIM
KV_WIDTH = GQA_KV_HEADS * HEAD_DIM
MIX_WIDTH = NA_WIDTH + SG_WIDTH + GQA_WIDTH
IN_WIDTH = 3 * NA_WIDTH + 2 * SG_WIDTH + GQA_WIDTH + 2 * KV_WIDTH

N_EXPERTS = 32
N_GROUPS = 4
EXPERTS_PER_GROUP = N_EXPERTS // N_GROUPS
TOP_K = 2
D_EXPERT = 1024

ALPHA = (2 * DEPTH) ** 0.25
LN_EPS = 1e-6
ATTN_SCALE = HEAD_DIM ** -0.5
LOG2E = 1.4426950408889634
NEG_BIG = -1e30
KEYS = SEQ + CTX_LEN

COL_NA_Q = 0
COL_NA_K = NA_HEADS
COL_NA_V = 2 * NA_HEADS
COL_SG = 3 * NA_HEADS
COL_GQ = COL_SG + 2 * SG_GROUPS
COL_GK = COL_GQ + GQA_Q_HEADS
COL_GV = COL_GK + GQA_KV_HEADS

PROJ_TN = 512
PROJ_NJ = IN_WIDTH // PROJ_TN
PROJ_ROWS = 256

NA_R = 4
NA_TQ = NA_R * GRID_W
NA_KR = NA_R + WIN_H - 1
NA_TK = NA_KR * GRID_W
NA_NB = ROWS // NA_R

MOE_TM = 768
MOE_SUB = 128
MOE_SUB_LOG2 = 7
MOE_TF = 512
MOE_NF = D_EXPERT // MOE_TF
META_TILE_EXPERT = 0
META_TILE_ROWS = 1
META_N_USED = 2
META_LANES = 128
POS_CHUNK = 256

ROW_TILE = 256


def _cparams(semantics, vmem_mib):
    return pltpu.CompilerParams(dimension_semantics=semantics,
                                vmem_limit_bytes=vmem_mib * 1024 * 1024)


def _layer_norm_rows(x):
    mu = jnp.mean(x, axis=-1, keepdims=True)
    xc = x - mu
    var = jnp.mean(xc * xc, axis=-1, keepdims=True)
    return xc * lax.rsqrt(var + LN_EPS)


def _dot(a, b):
    return jnp.dot(a, b, preferred_element_type=F32)


def _dot_nt(a, b):
    return lax.dot_general(a, b, (((1,), (1,)), ((), ())), preferred_element_type=F32)


ADA_TN = 1024


def _ada_kernel(c_ref, w_ref, b_ref, o_ref):
    cv = c_ref[...]
    act = (cv * jax.nn.sigmoid(cv)).astype(BF16)
    o_ref[...] = _dot(act, w_ref[...].astype(BF16)) + b_ref[...]


def _ada(cvec, ada_w, ada_b):
    n_out = 6 * D_MODEL
    return pl.pallas_call(
        _ada_kernel,
        grid=(DEPTH, n_out // ADA_TN),
        in_specs=[
            pl.BlockSpec((8, D_MODEL), lambda l, j: (0, 0)),
            pl.BlockSpec((None, D_MODEL, ADA_TN), lambda l, j: (l, 0, j)),
            pl.BlockSpec((None, 1, ADA_TN), lambda l, j: (l, 0, j)),
        ],
        out_specs=pl.BlockSpec((None, 8, ADA_TN), lambda l, j: (l, 0, j)),
        out_shape=jax.ShapeDtypeStruct((DEPTH, 8, n_out), F32),
        compiler_params=_cparams(("arbitrary", "arbitrary"), 40),
        name="ada_mod",
    )(cvec, ada_w, ada_b.reshape(DEPTH, 1, n_out))


def _rope_partner(y):
    lane = lax.broadcasted_iota(I32, y.shape, 1)
    first = (lane % (2 * ROPE_PAIRS)) < ROPE_PAIRS
    return jnp.where(first, pltpu.roll(y, HEAD_DIM - ROPE_PAIRS, 1), pltpu.roll(y, ROPE_PAIRS, 1))


def _norm_rope(zh, gain, cos, sin):
    ms = jnp.mean(zh * zh, axis=-1, keepdims=True)
    y = zh * lax.rsqrt(ms + LN_EPS) * gain
    return y * cos + _rope_partner(y) * sin


def _proj_in_kernel(x_ref, sh_ref, sc_ref, w_ref, cos_ref, sin_ref, qg_ref, kg_ref,
                    proj_ref, sgz_ref, h_scr):
    j = pl.program_id(1)
    tm = x_ref.shape[0]
    chunks = [slice(r, r + min(PROJ_ROWS, tm)) for r in range(0, tm, PROJ_ROWS)]

    def zed(rows):
        return _dot(h_scr[rows, :], w_ref[...])

    @pl.when(j == 0)
    def _():
        for rows in chunks:
            h = _layer_norm_rows(x_ref[rows, :]) * (1.0 + sc_ref[...]) + sh_ref[...]
            h_scr[rows, :] = h.astype(BF16)
            proj_ref[rows, :] = (zed(rows) * ATTN_SCALE).astype(BF16)

    @pl.when((j == 1) | (j == 2))
    def _():
        for rows in chunks:
            proj_ref[rows, :] = zed(rows).astype(BF16)

    @pl.when((j == 3) | (j == 4))
    def _():
        for rows in chunks:
            z = zed(rows)
            proj_ref[rows, :] = z.astype(BF16)
            sgz_ref[rows, :] = z

    @pl.when((j == 5) | (j == 6))
    def _():
        for rows in chunks:
            z = zed(rows)
            cos = cos_ref[rows, :]
            sin = sin_ref[rows, :]
            for hh in range(PROJ_TN // HEAD_DIM):
                cols = slice(hh * HEAD_DIM, (hh + 1) * HEAD_DIM)
                q = _norm_rope(z[:, cols], qg_ref[...], cos, sin)
                proj_ref[rows, cols] = (q * (ATTN_SCALE * LOG2E)).astype(BF16)

    @pl.when(j == 7)
    def _():
        for rows in chunks:
            z = zed(rows)
            cos = cos_ref[rows, :]
            sin = sin_ref[rows, :]
            for hh in range(GQA_KV_HEADS):
                cols = slice(hh * HEAD_DIM, (hh + 1) * HEAD_DIM)
                proj_ref[rows, cols] = _norm_rope(z[:, cols], kg_ref[...], cos, sin).astype(BF16)
            proj_ref[rows, KV_WIDTH:] = z[:, KV_WIDTH:].astype(BF16)


def _proj_in(x, sh, sc, w_bf16, cos, sin, q_gain, k_gain, tm):
    n = x.shape[0]
    return pl.pallas_call(
        _proj_in_kernel,
        grid=(n // tm, PROJ_NJ),
        in_specs=[
            pl.BlockSpec((tm, D_MODEL), lambda i, j: (i, 0)),
            pl.BlockSpec((1, D_MODEL), lambda i, j: (0, 0)),
            pl.BlockSpec((1, D_MODEL), lambda i, j: (0, 0)),
            pl.BlockSpec((D_MODEL, PROJ_TN), lambda i, j: (0, j)),
            pl.BlockSpec((tm, HEAD_DIM), lambda i, j: (i, 0)),
            pl.BlockSpec((tm, HEAD_DIM), lambda i, j: (i, 0)),
            pl.BlockSpec((1, HEAD_DIM), lambda i, j: (0, 0)),
            pl.BlockSpec((1, HEAD_DIM), lambda i, j: (0, 0)),
        ],
        out_specs=[
            pl.BlockSpec((tm, PROJ_TN), lambda i, j: (i, j)),
            pl.BlockSpec((tm, PROJ_TN), lambda i, j: (i, jnp.clip(j - 3, 0, 1))),
        ],
        out_shape=[
            jax.ShapeDtypeStruct((n, IN_WIDTH), BF16),
            jax.ShapeDtypeStruct((n, 2 * SG_WIDTH), F32),
        ],
        scratch_shapes=[pltpu.VMEM((tm, D_MODEL), BF16)],
        compiler_params=_cparams(("arbitrary", "arbitrary"), 48),
        name="proj_in",
    )(x, sh, sc, w_bf16, cos, sin, q_gain, k_gain)


def _na_kernel(q_ref, k_ref, v_ref, kc_ref, vc_ref, bias_ref, o_ref):
    b = pl.program_id(0)
    key_row0 = jnp.clip(b * NA_R - WIN_H // 2, 0, ROWS - NA_KR)
    start = pl.multiple_of(key_row0 * GRID_W, GRID_W)
    for h in range(NA_HEADS):
        cols = slice(h * HEAD_DIM, (h + 1) * HEAD_DIM)
        q = q_ref[:, cols]
        kw = k_ref[pl.ds(start, NA_TK), cols]
        vw = v_ref[pl.ds(start, NA_TK), cols]
        s_loc = _dot_nt(q, kw) + bias_ref[h]
        s_ctx = _dot_nt(q, kc_ref[:, cols])
        m = jnp.maximum(jnp.max(s_loc, axis=-1, keepdims=True), jnp.max(s_ctx, axis=-1, keepdims=True))
        p_loc = jnp.exp(s_loc - m)
        p_ctx = jnp.exp(s_ctx - m)
        denom = jnp.sum(p_loc, axis=-1, keepdims=True) + jnp.sum(p_ctx, axis=-1, keepdims=True)
        o = _dot(p_loc.astype(BF16), vw) + _dot(p_ctx.astype(BF16), vc_ref[:, cols])
        o_ref[:, cols] = (o / denom).astype(BF16)


NA_BIAS_LANE0 = 64
NA_BIAS_ROLL = HEAD_DIM - NA_BIAS_LANE0 - (WIN_W - 1)


def _na_bias_kernel(rows_ref, o_ref):
    c = lax.broadcasted_iota(I32, (GRID_W, GRID_W), 0)
    kc = lax.broadcasted_iota(I32, (GRID_W, GRID_W), 1)
    cs = jnp.clip(c - WIN_W // 2, 0, GRID_W - WIN_W)
    in_cols = (kc >= cs) & (kc < cs + WIN_W)
    for ri in range(NA_R):
        for kri in range(NA_KR):
            row = rows_ref[ri * NA_KR + kri:ri * NA_KR + kri + 1, :]
            spread = pltpu.roll(jnp.broadcast_to(row, (GRID_W, HEAD_DIM)), NA_BIAS_ROLL, 1,
                                stride=1, stride_axis=0)
            o_ref[ri * GRID_W:(ri + 1) * GRID_W, kri * GRID_W:(kri + 1) * GRID_W] = jnp.where(
                in_cols, spread[:, :GRID_W], NEG_BIG)


def _na_bias_tables(rpb):
    picked = []
    for r0, k0 in ((0, 0), (2 * NA_R, 2 * NA_R - WIN_H // 2), (ROWS - NA_R, ROWS - NA_KR)):
        r = r0 + np.arange(NA_R)
        rs = np.clip(r - WIN_H // 2, 0, ROWS - WIN_H)
        kr = k0 + np.arange(NA_KR)
        ok_r = (kr[None, :] >= rs[:, None]) & (kr[None, :] < rs[:, None] + WIN_H)
        drow = kr[None, :] - r[:, None] + (WIN_H - 1)
        pick_r = (np.clip(drow, 0, 2 * WIN_H - 2)[..., None] == np.arange(2 * WIN_H - 1)).astype(np.float32)
        rows = jnp.einsum('rka,lhab->lhrkb', pick_r, rpb.astype(F32), precision=lax.Precision.HIGHEST)
        rows = jnp.pad(rows, ((0, 0),) * 4 + ((NA_BIAS_LANE0, HEAD_DIM - NA_BIAS_LANE0 - (2 * WIN_W - 1)),))
        picked.append(jnp.where(ok_r[None, None, :, :, None], rows, NEG_BIG))
    rows_all = jnp.stack(picked, axis=1).reshape(DEPTH * 3 * NA_HEADS, NA_R * NA_KR, HEAD_DIM)
    tables = pl.pallas_call(
        _na_bias_kernel,
        grid=(DEPTH * 3 * NA_HEADS,),
        in_specs=[pl.BlockSpec((None, NA_R * NA_KR, HEAD_DIM), lambda t: (t, 0, 0))],
        out_specs=pl.BlockSpec((None, NA_TQ, NA_TK), lambda t: (t, 0, 0)),
        out_shape=jax.ShapeDtypeStruct((DEPTH * 3 * NA_HEADS, NA_TQ, NA_TK), F32),
        compiler_params=_cparams(("arbitrary",), 16),
        name="na_bias",
    )(rows_all)
    return tables.reshape(DEPTH, 3, NA_HEADS, NA_TQ, NA_TK)


def _na_attention(proj, proj_ctx, bias_tab, layer):
    def bias_idx(b):
        return (layer, jnp.where(b == 0, 0, jnp.where(b == NA_NB - 1, 2, 1)), 0, 0, 0)

    return pl.pallas_call(
        _na_kernel,
        grid=(NA_NB,),
        in_specs=[
            pl.BlockSpec((NA_TQ, NA_WIDTH), lambda b: (b, 0)),
            pl.BlockSpec((SEQ, NA_WIDTH), lambda b: (0, 1)),
            pl.BlockSpec((SEQ, NA_WIDTH), lambda b: (0, 2)),
            pl.BlockSpec((CTX_LEN, NA_WIDTH), lambda b: (0, 1)),
            pl.BlockSpec((CTX_LEN, NA_WIDTH), lambda b: (0, 2)),
            pl.BlockSpec((None, None, NA_HEADS, NA_TQ, NA_TK), bias_idx),
        ],
        out_specs=pl.BlockSpec((NA_TQ, NA_WIDTH), lambda b: (b, 0)),
        out_shape=jax.ShapeDtypeStruct((SEQ, NA_WIDTH), BF16),
        compiler_params=_cparams(("arbitrary",), 48),
        name="na_attn",
    )(proj, proj, proj, proj_ctx, proj_ctx, bias_tab)


def _ctx_attn_kernel(q_ref, k_ref, v_ref, o_ref, *, log2_domain):
    s = _dot_nt(q_ref[...], k_ref[...])
    m = jnp.max(s, axis=-1, keepdims=True)
    p = jnp.exp2(s - m) if log2_domain else jnp.exp(s - m)
    denom = jnp.sum(p, axis=-1, keepdims=True)
    o_ref[...] = (_dot(p.astype(BF16), v_ref[...]) / denom).astype(BF16)


def _ctx_attention(proj_ctx, n_heads, q_col, k_col, v_col, q_per_kv, log2_domain):
    return pl.pallas_call(
        functools.partial(_ctx_attn_kernel, log2_domain=log2_domain),
        grid=(n_heads,),
        in_specs=[
            pl.BlockSpec((CTX_LEN, HEAD_DIM), lambda h: (0, q_col + h)),
            pl.BlockSpec((CTX_LEN, HEAD_DIM), lambda h: (0, k_col + h // q_per_kv)),
            pl.BlockSpec((CTX_LEN, HEAD_DIM), lambda h: (0, v_col + h // q_per_kv)),
        ],
        out_specs=pl.BlockSpec((CTX_LEN, HEAD_DIM), lambda h: (0, h)),
        out_shape=jax.ShapeDtypeStruct((CTX_LEN, n_heads * HEAD_DIM), BF16),
        compiler_params=_cparams(("arbitrary",), 16),
        name="ctx_attn",
    )(proj_ctx, proj_ctx, proj_ctx)


SG_ROWS = 256


def _sg_kernel(zu_ref, zv_ref, g_ref, b_ref, w_ref, bs_ref, o_ref):
    for c in range(SG_ROWS // SG_CHUNK):
        rows = slice(c * SG_CHUNK, (c + 1) * SG_CHUNK)
        for g in range(SG_GROUPS):
            cols = slice(g * HEAD_DIM, (g + 1) * HEAD_DIM)
            v = jax.nn.gelu(zv_ref[rows, cols])
            vn = _layer_norm_rows(v) * g_ref[:, cols] + b_ref[:, cols]
            t = _dot(w_ref[g], vn.astype(BF16)) + bs_ref[:, g:g + 1]
            o_ref[rows, cols] = (jax.nn.gelu(zu_ref[rows, cols]) * t).astype(BF16)


def _spatial_gating(sgz, ln_g, ln_b, w_bf16, b_s):
    n = sgz.shape[0]
    return pl.pallas_call(
        _sg_kernel,
        grid=(n // SG_ROWS,),
        in_specs=[
            pl.BlockSpec((SG_ROWS, SG_WIDTH), lambda i: (i, 0)),
            pl.BlockSpec((SG_ROWS, SG_WIDTH), lambda i: (i, 1)),
            pl.BlockSpec((1, SG_WIDTH), lambda i: (0, 0)),
            pl.BlockSpec((1, SG_WIDTH), lambda i: (0, 0)),
            pl.BlockSpec((SG_GROUPS, SG_CHUNK, SG_CHUNK), lambda i: (0, 0, 0)),
            pl.BlockSpec((SG_CHUNK, SG_GROUPS), lambda i: (0, 0)),
        ],
        out_specs=pl.BlockSpec((SG_ROWS, SG_WIDTH), lambda i: (i, 0)),
        out_shape=jax.ShapeDtypeStruct((n, SG_WIDTH), BF16),
        compiler_params=_cparams(("arbitrary",), 16),
        name="spatial_gating",
    )(sgz, sgz, ln_g.reshape(1, SG_WIDTH), ln_b.reshape(1, SG_WIDTH), w_bf16, b_s.T)


GQA_TQ = 512
GQA_TK = 1024


def _gqa_kernel(qt_ref, k_ref, vt_ref, ot_ref, slots_x_ref, slots_y_ref, sa_ref, sb_ref, sc_ref,
                zero_ref, zero_sem):
    qt = qt_ref[...]
    n_chunks = SEQ // GQA_TK

    step = pl.program_id(0) * pl.num_programs(1) + pl.program_id(1)
    n_steps = pl.num_programs(0) * pl.num_programs(1)
    n_zero_blocks = slots_x_ref.shape[0] // MOE_TM

    def zero_copy(dst_ref, c):
        off = pl.multiple_of(c * MOE_TM, MOE_TM)
        return pltpu.make_async_copy(zero_ref, dst_ref.at[pl.ds(off, MOE_TM), :], zero_sem)

    @pl.when(step == 0)
    def _():
        zero_ref[...] = jnp.zeros_like(zero_ref)

    @pl.when(step < n_zero_blocks)
    def _():
        zero_copy(slots_x_ref, step).start()

    @pl.when((step >= n_zero_blocks) & (step < 2 * n_zero_blocks))
    def _():
        zero_copy(slots_y_ref, step - n_zero_blocks).start()

    def scores(t, dst_ref):
        off = pl.multiple_of(t * GQA_TK, GQA_TK)
        s = _dot(k_ref[pl.ds(off, GQA_TK), :], qt)
        dst_ref[...] = s
        return jnp.max(s, axis=0, keepdims=True)

    def consume(t, src_ref, m_chunk, m, l, acc):
        off = pl.multiple_of(t * GQA_TK, GQA_TK)
        m_new = jnp.maximum(m, m_chunk)
        alpha = jnp.exp2(m - m_new)
        p = jnp.exp2(src_ref[...] - m_new)
        l = alpha * l + jnp.sum(p, axis=0, keepdims=True)
        acc = alpha * acc + _dot(vt_ref[:, pl.ds(off, GQA_TK)], p.astype(BF16))
        return m_new, l, acc

    m = jnp.full((1, GQA_TQ), -jnp.inf, F32)
    l = jnp.zeros((1, GQA_TQ), F32)
    acc = jnp.zeros((HEAD_DIM, GQA_TQ), F32)
    mc_a = scores(0, sa_ref)

    def body(u, carry):
        m, l, acc, mc_a = carry
        mc_b = scores(2 * u + 1, sb_ref)
        m, l, acc = consume(2 * u, sa_ref, mc_a, m, l, acc)
        mc_a = scores(2 * u + 2, sa_ref)
        m, l, acc = consume(2 * u + 1, sb_ref, mc_b, m, l, acc)
        return m, l, acc, mc_a

    m, l, acc, mc_a = lax.fori_loop(0, n_chunks // 2 - 1, body, (m, l, acc, mc_a))
    mc_b = scores(n_chunks - 1, sb_ref)
    m, l, acc = consume(n_chunks - 2, sa_ref, mc_a, m, l, acc)
    s_ctx = _dot(k_ref[pl.ds(SEQ, CTX_LEN), :], qt)
    sc_ref[...] = s_ctx
    mc_ctx = jnp.max(s_ctx, axis=0, keepdims=True)
    m, l, acc = consume(n_chunks - 1, sb_ref, mc_b, m, l, acc)
    m_new = jnp.maximum(m, mc_ctx)
    alpha = jnp.exp2(m - m_new)
    p = jnp.exp2(sc_ref[...] - m_new)
    l = alpha * l + jnp.sum(p, axis=0, keepdims=True)
    acc = alpha * acc + _dot(vt_ref[:, pl.ds(SEQ, CTX_LEN)], p.astype(BF16))
    ot_ref[...] = (acc / l).astype(BF16)

    @pl.when(step == n_steps - 1)
    def _():
        def wait(c, carry):
            zero_copy(slots_x_ref, c).wait()
            zero_copy(slots_y_ref, c).wait()
            return carry
        lax.fori_loop(0, n_zero_blocks, wait, 0)


def _gqa_attention(q_t, k_all, v_t, n_slot_rows):
    grp = GQA_Q_HEADS // GQA_KV_HEADS
    grid = (GQA_Q_HEADS, SEQ // GQA_TQ)
    assert n_slot_rows % MOE_TM == 0 and 2 * (n_slot_rows // MOE_TM) <= grid[0] * grid[1]
    return pl.pallas_call(
        _gqa_kernel,
        grid=grid,
        in_specs=[
            pl.BlockSpec((HEAD_DIM, GQA_TQ), lambda h, i: (h, i)),
            pl.BlockSpec((KEYS, HEAD_DIM), lambda h, i: (0, h // grp)),
            pl.BlockSpec((HEAD_DIM, KEYS), lambda h, i: (h // grp, 0)),
        ],
        out_specs=[pl.BlockSpec((HEAD_DIM, GQA_TQ), lambda h, i: (h, i)),
                   pl.BlockSpec(memory_space=pl.ANY),
                   pl.BlockSpec(memory_space=pl.ANY)],
        out_shape=[jax.ShapeDtypeStruct((GQA_WIDTH, SEQ), BF16),
                   jax.ShapeDtypeStruct((n_slot_rows, D_MODEL), F32),
                   jax.ShapeDtypeStruct((n_slot_rows, D_MODEL), F32)],
        scratch_shapes=[pltpu.VMEM((GQA_TK, GQA_TQ), F32), pltpu.VMEM((GQA_TK, GQA_TQ), F32),
                        pltpu.VMEM((CTX_LEN, GQA_TQ), F32),
                        pltpu.VMEM((MOE_TM, D_MODEL), F32), pltpu.SemaphoreType.DMA(())],
        compiler_params=_cparams(("arbitrary", "arbitrary"), 32),
        name="gqa_attn",
    )(q_t, k_all, v_t)


def _proj_out_kernel(ona_ref, osg_ref, ogq_ref, w_ref, x_ref, g1_ref, lng_ref, lnb_ref,
                     sh2_ref, sc2_ref, rwt_ref, rb_ref, xmid_ref, h2_ref, e_ref, gate_ref, y_scr):
    i = pl.program_id(0)

    @pl.when(i == 0)
    def _():
        y_scr[1] = jnp.zeros_like(y_scr[1])

    for parity in range(2):
        @pl.when(i % 2 == parity)
        def _(parity=parity):
            y_scr[parity] = (_dot(ona_ref[...], w_ref[0:NA_WIDTH, :])
                             + _dot(osg_ref[...], w_ref[NA_WIDTH:NA_WIDTH + SG_WIDTH, :])
                             + _dot(ogq_ref[...], w_ref[NA_WIDTH + SG_WIDTH:, :]))
            z = ALPHA * x_ref[...] + g1_ref[...] * y_scr[1 - parity]
            xm = _layer_norm_rows(z) * lng_ref[...] + lnb_ref[...]
            xmid_ref[...] = xm
            h2 = _layer_norm_rows(xm) * (1.0 + sc2_ref[...]) + sh2_ref[...]
            h2_ref[...] = h2
            e_ref[...], gate_ref[...] = _route(h2, rwt_ref[...], rb_ref[...])


def _proj_out(o_na, o_sg, o_gq, w_bf16, x, g1, ln_g, ln_b, sh2, sc2, rw_t_bf16, rb_col):
    n = x.shape[0]
    tm = ROW_TILE
    n_tiles = n // tm
    vec = pl.BlockSpec((1, D_MODEL), lambda i: (0, 0))

    def mm_tile(i):
        return jnp.minimum(i, n_tiles - 1)

    def ep_tile(i):
        return jnp.maximum(i - 1, 0)

    return pl.pallas_call(
        _proj_out_kernel,
        grid=(n_tiles + 1,),
        in_specs=[
            pl.BlockSpec((tm, NA_WIDTH), lambda i: (mm_tile(i), 0)),
            pl.BlockSpec((tm, SG_WIDTH), lambda i: (mm_tile(i), 0)),
            pl.BlockSpec((tm, GQA_WIDTH), lambda i: (mm_tile(i), 0)),
            pl.BlockSpec((MIX_WIDTH, D_MODEL), lambda i: (0, 0)),
            pl.BlockSpec((tm, D_MODEL), lambda i: (ep_tile(i), 0)),
            vec, vec, vec, vec, vec,
            pl.BlockSpec((N_EXPERTS, D_MODEL), lambda i: (0, 0)),
            pl.BlockSpec((N_EXPERTS, 1), lambda i: (0, 0)),
        ],
        out_specs=[pl.BlockSpec((tm, D_MODEL), lambda i: (ep_tile(i), 0)),
                   pl.BlockSpec((tm, D_MODEL), lambda i: (ep_tile(i), 0)),
                   pl.BlockSpec((TOP_K, tm), lambda i: (0, ep_tile(i))),
                   pl.BlockSpec((TOP_K, tm), lambda i: (0, ep_tile(i)))],
        out_shape=[jax.ShapeDtypeStruct((n, D_MODEL), F32),
                   jax.ShapeDtypeStruct((n, D_MODEL), F32),
                   jax.ShapeDtypeStruct((TOP_K, n), I32),
                   jax.ShapeDtypeStruct((TOP_K, n), F32)],
        scratch_shapes=[pltpu.VMEM((2, tm, D_MODEL), F32)],
        compiler_params=_cparams(("arbitrary",), 48),
        name="proj_out",
    )(o_na, o_sg, o_gq, w_bf16, x, g1, ln_g, ln_b, sh2, sc2, rw_t_bf16, rb_col)


def _route(h, rw_t, rb_col):
    tm = h.shape[0]
    logits = _dot_nt(rw_t, h.astype(BF16))
    scores = jax.nn.sigmoid(logits)
    sel = scores + rb_col
    sub = lax.broadcasted_iota(I32, (EXPERTS_PER_GROUP, tm), 0)
    best = None
    for g in range(N_GROUPS):
        rows = slice(g * EXPERTS_PER_GROUP, (g + 1) * EXPERTS_PER_GROUP)
        v = sel[rows, :]
        sc = scores[rows, :]
        m1 = jnp.max(v, axis=0, keepdims=True)
        i1 = jnp.min(jnp.where(v == m1, sub, EXPERTS_PER_GROUP), axis=0, keepdims=True)
        v2 = jnp.where(sub == i1, -jnp.inf, v)
        m2 = jnp.max(v2, axis=0, keepdims=True)
        i2 = jnp.min(jnp.where(v2 == m2, sub, EXPERTS_PER_GROUP), axis=0, keepdims=True)
        s1 = jnp.sum(jnp.where(sub == i1, sc, 0.0), axis=0, keepdims=True)
        s2 = jnp.sum(jnp.where(sub == i2, sc, 0.0), axis=0, keepdims=True)
        cand = (m1 + m2, i1 + g * EXPERTS_PER_GROUP, i2 + g * EXPERTS_PER_GROUP, s1, s2)
        if best is None:
            best = cand
        else:
            better = cand[0] > best[0]
            best = tuple(jnp.where(better, cn, bs) for cn, bs in zip(cand, best))
    _, e1, e2, s1, s2 = best
    tot = s1 + s2
    return jnp.concatenate([e1, e2], axis=0), jnp.concatenate([s1 / tot, s2 / tot], axis=0)


def _num_tiles(n_tok):
    return (n_tok * TOP_K + N_EXPERTS * (MOE_TM - 1) + MOE_TM - 1) // MOE_TM


def _positions_kernel(e_ref, dest_ref, meta_ref, rank_scr, *, n_tok):
    ch = POS_CHUNK
    n_chunks = n_tok // ch
    iota_e = lax.broadcasted_iota(I32, (N_EXPERTS, ch), 0)
    upper = (lax.broadcasted_iota(I32, (ch, ch), 0) <= lax.broadcasted_iota(I32, (ch, ch), 1)).astype(BF16)

    def one_hots(off):
        e0 = e_ref[0:1, pl.ds(off, ch)]
        e1 = e_ref[1:2, pl.ds(off, ch)]
        return (iota_e == e0).astype(F32), (iota_e == e1).astype(F32)

    def rank_body(c, carry):
        off = pl.multiple_of(c * ch, ch)
        oh0, oh1 = one_hots(off)
        both = oh0 + oh1
        incl = _dot(both.astype(BF16), upper)
        before = incl - both + carry
        rank_scr[0:1, pl.ds(off, ch)] = jnp.sum(oh0 * before, axis=0, keepdims=True)
        rank_scr[1:2, pl.ds(off, ch)] = jnp.sum(oh1 * (before + oh0), axis=0, keepdims=True)
        return carry + jnp.sum(both, axis=1, keepdims=True)

    counts = lax.fori_loop(0, n_chunks, rank_body, jnp.zeros((N_EXPERTS, 1), F32))
    tiles = jnp.zeros((N_EXPERTS, 1), F32)
    for k in range(n_tok * TOP_K // MOE_TM + 1):
        tiles = tiles + (counts > float(k * MOE_TM)).astype(F32)
    tiles_b = jnp.broadcast_to(tiles, (N_EXPERTS, META_LANES)).astype(BF16)
    lower = (lax.broadcasted_iota(I32, (N_EXPERTS, N_EXPERTS), 1)
             < lax.broadcasted_iota(I32, (N_EXPERTS, N_EXPERTS), 0)).astype(BF16)
    tile_start = _dot(lower, tiles_b)
    tile_end = tile_start + tiles
    pad_start = tile_start[:, 0:1] * float(MOE_TM)

    def dest_body(c, carry):
        off = pl.multiple_of(c * ch, ch)
        oh0, oh1 = one_hots(off)
        d0 = rank_scr[0:1, pl.ds(off, ch)] + jnp.sum(oh0 * pad_start, axis=0, keepdims=True)
        d1 = rank_scr[1:2, pl.ds(off, ch)] + jnp.sum(oh1 * pad_start, axis=0, keepdims=True)
        dest_ref[0:1, pl.ds(off, ch)] = d0.astype(I32)
        dest_ref[1:2, pl.ds(off, ch)] = d1.astype(I32)
        return carry

    lax.fori_loop(0, n_chunks, dest_body, 0)

    tile_id = lax.broadcasted_iota(I32, (N_EXPERTS, META_LANES), 1).astype(F32)
    tile_expert = jnp.sum((tile_end <= tile_id).astype(F32), axis=0, keepdims=True)
    tile_expert = jnp.minimum(tile_expert, float(N_EXPERTS - 1))
    inside = (tile_start <= tile_id) & (tile_id < tile_end)
    rows_here = jnp.clip(counts - float(MOE_TM) * (tile_id - tile_start), 0.0, float(MOE_TM))
    tile_rows = jnp.sum(jnp.where(inside, rows_here, 0.0), axis=0, keepdims=True)
    n_used = jnp.sum(tiles, axis=0, keepdims=True)
    row = lax.broadcasted_iota(I32, (8, META_LANES), 0)
    shape = (8, META_LANES)
    meta = jnp.where(row == META_TILE_EXPERT, jnp.broadcast_to(tile_expert, shape),
                     jnp.where(row == META_TILE_ROWS, jnp.broadcast_to(tile_rows, shape),
                               jnp.where(row == META_N_USED, jnp.broadcast_to(n_used, shape), 0.0)))
    meta_ref[...] = meta.astype(I32)


def _positions(e_all):
    n_tok = e_all.shape[1]
    return pl.pallas_call(
        functools.partial(_positions_kernel, n_tok=n_tok),
        out_shape=[jax.ShapeDtypeStruct((TOP_K, n_tok), I32),
                   jax.ShapeDtypeStruct((8, 128), I32)],
        scratch_shapes=[pltpu.VMEM((TOP_K, n_tok), F32)],
        compiler_params=pltpu.CompilerParams(vmem_limit_bytes=16 * 1024 * 1024),
        name="moe_positions",
    )(e_all)


def _dispatch_kernel(dest_ref, h_ref, xs_in_ref, xs_ref, sem, *, n_tok):
    del xs_in_ref
    tm = h_ref.shape[0]
    base = pl.program_id(0) * tm

    def row_copy(r, k):
        d = dest_ref[k * n_tok + base + r]
        return pltpu.make_async_copy(h_ref.at[pl.ds(r, 1), :], xs_ref.at[pl.ds(d, 1), :], sem)

    def start(r, carry):
        for k in range(TOP_K):
            row_copy(r, k).start(priority=k)
        return carry

    def wait(r, carry):
        for k in range(TOP_K):
            row_copy(r, k).wait()
        return carry

    lax.fori_loop(0, tm, start, 0)
    lax.fori_loop(0, tm, wait, 0)


def _dispatch(dest_flat, h2, xs):
    n = h2.shape[0]
    tm = ROW_TILE
    return pl.pallas_call(
        functools.partial(_dispatch_kernel, n_tok=n),
        grid_spec=pltpu.PrefetchScalarGridSpec(
            num_scalar_prefetch=1,
            grid=(n // tm,),
            in_specs=[pl.BlockSpec((tm, D_MODEL), lambda i, d: (i, 0)),
                      pl.BlockSpec(memory_space=pl.ANY)],
            out_specs=pl.BlockSpec(memory_space=pl.ANY),
            scratch_shapes=[pltpu.SemaphoreType.DMA(())],
        ),
        out_shape=jax.ShapeDtypeStruct(xs.shape, xs.dtype),
        input_output_aliases={2: 0},
        compiler_params=_cparams(("arbitrary",), 16),
        name="moe_dispatch",
    )(dest_flat, h2, xs)


def _moe_kernel(meta_ref, x_ref, wg_ref, wu_ref, wd_ref, y_in_ref, y_ref, xb_scr):
    del y_in_ref
    i = pl.program_id(0)
    f = pl.program_id(1)
    used = i < meta_ref[META_N_USED * META_LANES]
    rows = meta_ref[META_TILE_ROWS * META_LANES + i]
    n_sub = (rows + (MOE_SUB - 1)) >> MOE_SUB_LOG2

    @pl.when(used & (f == 0))
    def _():
        y_ref[...] = jnp.zeros_like(y_ref)

    for occupied in range(1, MOE_TM // MOE_SUB + 1):
        m_rows = occupied * MOE_SUB

        @pl.when(used & (n_sub == occupied))
        def _(m_rows=m_rows):
            @pl.when(f == 0)
            def _():
                xb_scr[0:m_rows, :] = x_ref[0:m_rows, :].astype(BF16)

            xb = xb_scr[0:m_rows, :]
            gate = _dot(xb, wg_ref[...].astype(BF16))
            up = _dot(xb, wu_ref[...].astype(BF16))
            hidden = (gate * jax.nn.sigmoid(gate) * up).astype(BF16)
            y_ref[0:m_rows, :] += _dot(hidden, wd_ref[...].astype(BF16))


def _moe_ffn(meta_flat, xs, w_gate, w_up, w_down, y_zero, layer, n_tiles):
    n_used_at = META_N_USED * META_LANES

    def last_used(i, m):
        return jnp.minimum(i, m[n_used_at] - 1)

    def f_eff(i, f, m):
        return jnp.where(i < m[n_used_at], f, MOE_NF - 1)

    return pl.pallas_call(
        _moe_kernel,
        grid_spec=pltpu.PrefetchScalarGridSpec(
            num_scalar_prefetch=1,
            grid=(n_tiles, MOE_NF),
            in_specs=[
                pl.BlockSpec((MOE_TM, D_MODEL), lambda i, f, m: (last_used(i, m), 0)),
                pl.BlockSpec((None, None, D_MODEL, MOE_TF),
                             lambda i, f, m: (layer, m[last_used(i, m)], 0, f_eff(i, f, m))),
                pl.BlockSpec((None, None, D_MODEL, MOE_TF),
                             lambda i, f, m: (layer, m[last_used(i, m)], 0, f_eff(i, f, m))),
                pl.BlockSpec((None, None, MOE_TF, D_MODEL),
                             lambda i, f, m: (layer, m[last_used(i, m)], f_eff(i, f, m), 0)),
                pl.BlockSpec(memory_space=pl.ANY),
            ],
            out_specs=pl.BlockSpec((MOE_TM, D_MODEL), lambda i, f, m: (last_used(i, m), 0)),
            scratch_shapes=[pltpu.VMEM((MOE_TM, D_MODEL), BF16)],
        ),
        out_shape=jax.ShapeDtypeStruct((n_tiles * MOE_TM, D_MODEL), F32),
        input_output_aliases={5: 0},
        compiler_params=_cparams(("arbitrary", "arbitrary"), 56),
        name="moe_ffn",
    )(meta_flat, xs, w_gate, w_up, w_down, y_zero)


def _combine_kernel(dest_ref, y_ref, gt_ref, x_ref, g2_ref, lng_ref, lnb_ref, o_ref,
                    buf_ref, sem, *, n_tok):
    tm = x_ref.shape[0]
    i = pl.program_id(0)

    def row_copy(tile, slot, r, k):
        d = dest_ref[k * n_tok + tile * tm + r]
        return pltpu.make_async_copy(y_ref.at[pl.ds(d, 1), :],
                                     buf_ref.at[slot, k, pl.ds(r, 1), :], sem.at[slot])

    def start_tile(tile, slot):
        def start(r, carry):
            for k in range(TOP_K):
                row_copy(tile, slot, r, k).start(priority=k)
            return carry
        lax.fori_loop(0, tm, start, 0)

    def wait_tile(tile, slot):
        def wait(r, carry):
            for k in range(TOP_K):
                row_copy(tile, slot, r, k).wait()
            return carry
        lax.fori_loop(0, tm, wait, 0)

    @pl.when(i == 0)
    def _():
        start_tile(i, 0)

    for slot in range(2):
        @pl.when((i + 1 < pl.num_programs(0)) & ((i + 1) % 2 == slot))
        def _(slot=slot):
            start_tile(i + 1, slot)

    for slot in range(2):
        @pl.when(i % 2 == slot)
        def _(slot=slot):
            wait_tile(i, slot)
            gt = gt_ref[...]
            moe = buf_ref[slot, 0] * gt[:, 0:1] + buf_ref[slot, 1] * gt[:, 1:2]
            z = ALPHA * x_ref[...] + g2_ref[...] * moe
            o_ref[...] = _layer_norm_rows(z) * lng_ref[...] + lnb_ref[...]


def _combine(dest_flat, y, gates_t, x_mid, g2, ln_g, ln_b):
    n = x_mid.shape[0]
    tm = ROW_TILE
    vec = pl.BlockSpec((1, D_MODEL), lambda i, d: (0, 0))
    return pl.pallas_call(
        functools.partial(_combine_kernel, n_tok=n),
        grid_spec=pltpu.PrefetchScalarGridSpec(
            num_scalar_prefetch=1,
            grid=(n // tm,),
            in_specs=[pl.BlockSpec(memory_space=pl.ANY),
                      pl.BlockSpec((tm, TOP_K), lambda i, d: (i, 0)),
                      pl.BlockSpec((tm, D_MODEL), lambda i, d: (i, 0)),
                      vec, vec, vec],
            out_specs=pl.BlockSpec((tm, D_MODEL), lambda i, d: (i, 0)),
            scratch_shapes=[pltpu.VMEM((2, TOP_K, tm, D_MODEL), F32),
                            pltpu.SemaphoreType.DMA((2,))],
        ),
        out_shape=jax.ShapeDtypeStruct((n, D_MODEL), F32),
        compiler_params=_cparams(("arbitrary",), 32),
        name="moe_combine",
    )(dest_flat, y, gates_t, x_mid, g2, ln_g, ln_b)


def _rope_tables():
    t = jnp.arange(SEQ, dtype=jnp.int32)
    row = (t // GRID_W).astype(F32)
    col = (t % GRID_W).astype(F32)
    inv_freq = ROPE_THETA ** (-jnp.arange(ROPE_PAIRS, dtype=F32) / ROPE_PAIRS)
    ang_r = row[:, None] * inv_freq
    ang_c = col[:, None] * inv_freq
    cos = jnp.concatenate([jnp.cos(ang_r), jnp.cos(ang_r), jnp.cos(ang_c), jnp.cos(ang_c)], axis=-1)
    sin = jnp.concatenate([-jnp.sin(ang_r), jnp.sin(ang_r), -jnp.sin(ang_c), jnp.sin(ang_c)], axis=-1)
    return cos, sin


def kernel(x, c, ctx, c_ctx, ada_w, ada_b, w_in, na_rpb, sg_ln_g, sg_ln_b, sg_w, sg_b, q_norm_g, k_norm_g, w_out, ln_mix_g, ln_mix_b, router_w, router_bias, moe_w_gate, moe_w_up, moe_w_down, ln_ffn_g, ln_ffn_b):
    cos, sin = _rope_tables()
    cos_ctx = jnp.ones((CTX_LEN, HEAD_DIM), F32)
    sin_ctx = jnp.zeros((CTX_LEN, HEAD_DIM), F32)

    cvec = jnp.zeros((8, D_MODEL), F32).at[0].set(c[0]).at[1].set(c_ctx)
    mods = _ada(cvec, ada_w, ada_b)

    na_bias = _na_bias_tables(na_rpb)
    rw_t = router_w.T.astype(BF16)
    rb_col = router_bias.reshape(N_EXPERTS, 1).astype(F32)

    xl = x[0]
    xc = ctx[0]
    for layer in range(DEPTH):
        last = layer == DEPTH - 1
        sh1, sc1, g1, sh2, sc2, g2 = jnp.split(mods[layer, 0:1], 6, axis=-1)
        csh1, csc1, cg1, csh2, csc2, cg2 = jnp.split(mods[layer, 1:2], 6, axis=-1)
        w_in_b = w_in[layer].astype(BF16)
        w_out_b = w_out[layer].astype(BF16)
        sg_w_b = sg_w[layer].astype(BF16)
        q_gain = q_norm_g[layer].reshape(1, HEAD_DIM)
        k_gain = k_norm_g[layer].reshape(1, HEAD_DIM)
        lng = ln_mix_g[layer].reshape(1, D_MODEL)
        lnb = ln_mix_b[layer].reshape(1, D_MODEL)
        fng = ln_ffn_g[layer].reshape(1, D_MODEL)
        fnb = ln_ffn_b[layer].reshape(1, D_MODEL)

        proj, sgz = _proj_in(xl, sh1, sc1, w_in_b, cos, sin, q_gain, k_gain, tm=1024)
        proj_c, sgz_c = _proj_in(xc, csh1, csc1, w_in_b, cos_ctx, sin_ctx, q_gain, k_gain, tm=CTX_LEN)

        o_na = _na_attention(proj, proj_c, na_bias, layer)
        o_sg = _spatial_gating(sgz, sg_ln_g[layer], sg_ln_b[layer], sg_w_b, sg_b[layer])
        q_t = proj[:, COL_GQ * HEAD_DIM:COL_GK * HEAD_DIM].T
        k_all = jnp.concatenate([proj[:, COL_GK * HEAD_DIM:COL_GV * HEAD_DIM],
                                 proj_c[:, COL_GK * HEAD_DIM:COL_GV * HEAD_DIM]], axis=0)
        v_t = jnp.concatenate([proj[:, COL_GV * HEAD_DIM:], proj_c[:, COL_GV * HEAD_DIM:]], axis=0).T
        n_tok = SEQ if last else SEQ + CTX_LEN
        n_tiles = _num_tiles(n_tok)
        o_gq_t, xs, y_zero = _gqa_attention(q_t, k_all, v_t, n_tiles * MOE_TM)
        x_mid, h2, e_all, gates = _proj_out(o_na, o_sg, o_gq_t.T, w_out_b, xl, g1, lng, lnb, sh2, sc2,
                                            rw_t, rb_col)

        if not last:
            c_na = _ctx_attention(proj_c, NA_HEADS, COL_NA_Q, COL_NA_K, COL_NA_V, 1, False)
            c_sg = _spatial_gating(sgz_c, sg_ln_g[layer], sg_ln_b[layer], sg_w_b, sg_b[layer])
            c_gq = _ctx_attention(proj_c, GQA_Q_HEADS, COL_GQ, COL_GK, COL_GV,
                                  GQA_Q_HEADS // GQA_KV_HEADS, True)
            xc_mid, h2_c, e_ctx, gates_ctx = _proj_out(c_na, c_sg, c_gq, w_out_b, xc, cg1, lng, lnb,
                                                       csh2, csc2, rw_t, rb_col)
            e_all = jnp.concatenate([e_all, e_ctx], axis=1)
            gates = jnp.concatenate([gates, gates_ctx], axis=1)

        dest, meta = _positions(e_all)
        meta_flat = meta.reshape(-1)
        xs = _dispatch(dest[:, :SEQ].reshape(-1), h2, xs)
        if not last:
            xs = _dispatch(dest[:, SEQ:].reshape(-1), h2_c, xs)
        y = _moe_ffn(meta_flat, xs, moe_w_gate, moe_w_up, moe_w_down, y_zero, layer, n_tiles)
        gates_t = gates.T
        xl = _combine(dest[:, :SEQ].reshape(-1), y, gates_t[:SEQ], x_mid, g2, fng, fnb)
        if not last:
            xc = _combine(dest[:, SEQ:].reshape(-1), y, gates_t[SEQ:], xc_mid, cg2, fng, fnb)
    return xl[None]
```

```python
import functools

import jax
import jax.numpy as jnp
import numpy as np
from jax import lax
from jax.experimental import pallas as pl
from jax.experimental.pallas import tpu as pltpu

F32 = jnp.float32
BF16 = jnp.bfloat16
I32 = jnp.int32

D_MODEL = 2048
SEQ = 8192
DEPTH = 2
GRID_W = 64
ROWS = SEQ // GRID_W
CTX_LEN = 256
HEAD_DIM = 128

NA_HEADS = 4
WIN_H = 8
WIN_W = 16
SG_GROUPS = 4
SG_CHUNK = 128
GQA_Q_HEADS = 8
GQA_KV_HEADS = 2
ROPE_THETA = 10000.0
ROPE_PAIRS = HEAD_DIM // 4

NA_WIDTH = NA_HEADS * HEAD_DIM
SG_WIDTH = SG_GROUPS * HEAD_DIM
GQA_WIDTH = GQA_Q_HEADS * HEAD_DIM
KV_WIDTH = GQA_KV_HEADS * HEAD_DIM
MIX_WIDTH = NA_WIDTH + SG_WIDTH + GQA_WIDTH
IN_WIDTH = 3 * NA_WIDTH + 2 * SG_WIDTH + GQA_WIDTH + 2 * KV_WIDTH

N_EXPERTS = 32
N_GROUPS = 4
EXPERTS_PER_GROUP = N_EXPERTS // N_GROUPS
TOP_K = 2
D_EXPERT = 1024

ALPHA = (2 * DEPTH) ** 0.25
LN_EPS = 1e-6
ATTN_SCALE = HEAD_DIM ** -0.5
LOG2E = 1.4426950408889634
NEG_BIG = -1e30
KEYS = SEQ + CTX_LEN

COL_NA_Q = 0
COL_NA_K = NA_HEADS
COL_NA_V = 2 * NA_HEADS
COL_SG = 3 * NA_HEADS
COL_GQ = COL_SG + 2 * SG_GROUPS
COL_GK = COL_GQ + GQA_Q_HEADS
COL_GV = COL_GK + GQA_KV_HEADS

PROJ_TN = 512
PROJ_NJ = IN_WIDTH // PROJ_TN
PROJ_ROWS = 256

NA_R = 4
NA_TQ = NA_R * GRID_W
NA_KR = NA_R + WIN_H - 1
NA_TK = NA_KR * GRID_W
NA_NB = ROWS // NA_R

MOE_TM = 768
MOE_SUB = 128
MOE_SUB_LOG2 = 7
MOE_TF = 512
MOE_NF = D_EXPERT // MOE_TF
META_TILE_EXPERT = 0
META_TILE_ROWS = 1
META_N_USED = 2
META_LANES = 128
POS_CHUNK = 256

ROW_TILE = 256
DISPATCH_TILE = 1024


def _cparams(semantics, vmem_mib):
    return pltpu.CompilerParams(dimension_semantics=semantics,
                                vmem_limit_bytes=vmem_mib * 1024 * 1024)


def _layer_norm_rows(x):
    mu = jnp.mean(x, axis=-1, keepdims=True)
    xc = x - mu
    var = jnp.mean(xc * xc, axis=-1, keepdims=True)
    return xc * lax.rsqrt(var + LN_EPS)


def _dot(a, b):
    return jnp.dot(a, b, preferred_element_type=F32)


def _dot_nt(a, b):
    return lax.dot_general(a, b, (((1,), (1,)), ((), ())), preferred_element_type=F32)


ADA_TN = 1024


def _ada_kernel(c_ref, w_ref, b_ref, o_ref):
    cv = c_ref[...]
    act = (cv * jax.nn.sigmoid(cv)).astype(BF16)
    o_ref[...] = _dot(act, w_ref[...].astype(BF16)) + b_ref[...]


def _ada(cvec, ada_w, ada_b):
    n_out = 6 * D_MODEL
    return pl.pallas_call(
        _ada_kernel,
        grid=(DEPTH, n_out // ADA_TN),
        in_specs=[
            pl.BlockSpec((8, D_MODEL), lambda l, j: (0, 0)),
            pl.BlockSpec((None, D_MODEL, ADA_TN), lambda l, j: (l, 0, j)),
            pl.BlockSpec((None, 1, ADA_TN), lambda l, j: (l, 0, j)),
        ],
        out_specs=pl.BlockSpec((None, 8, ADA_TN), lambda l, j: (l, 0, j)),
        out_shape=jax.ShapeDtypeStruct((DEPTH, 8, n_out), F32),
        compiler_params=_cparams(("arbitrary", "arbitrary"), 40),
        name="ada_mod",
    )(cvec, ada_w, ada_b.reshape(DEPTH, 1, n_out))


def _rope_partner(y):
    lane = lax.broadcasted_iota(I32, y.shape, 1)
    first = (lane % (2 * ROPE_PAIRS)) < ROPE_PAIRS
    return jnp.where(first, pltpu.roll(y, HEAD_DIM - ROPE_PAIRS, 1), pltpu.roll(y, ROPE_PAIRS, 1))


def _norm_rope(zh, gain, cos, sin):
    ms = jnp.mean(zh * zh, axis=-1, keepdims=True)
    y = zh * lax.rsqrt(ms + LN_EPS) * gain
    return y * cos + _rope_partner(y) * sin


def _proj_in_kernel(x_ref, sh_ref, sc_ref, w_ref, cos_ref, sin_ref, qg_ref, kg_ref,
                    proj_ref, sgz_ref, h_scr):
    j = pl.program_id(1)
    tm = x_ref.shape[0]
    chunks = [slice(r, r + min(PROJ_ROWS, tm)) for r in range(0, tm, PROJ_ROWS)]

    def zed(rows):
        return _dot(h_scr[rows, :], w_ref[...])

    @pl.when(j == 0)
    def _():
        for rows in chunks:
            h = _layer_norm_rows(x_ref[rows, :]) * (1.0 + sc_ref[...]) + sh_ref[...]
            h_scr[rows, :] = h.astype(BF16)
            proj_ref[rows, :] = (zed(rows) * ATTN_SCALE).astype(BF16)

    @pl.when((j == 1) | (j == 2))
    def _():
        for rows in chunks:
            proj_ref[rows, :] = zed(rows).astype(BF16)

    @pl.when((j == 3) | (j == 4))
    def _():
        for rows in chunks:
            z = zed(rows)
            proj_ref[rows, :] = z.astype(BF16)
            sgz_ref[rows, :] = z

    @pl.when((j == 5) | (j == 6))
    def _():
        for rows in chunks:
            z = zed(rows)
            cos = cos_ref[rows, :]
            sin = sin_ref[rows, :]
            for hh in range(PROJ_TN // HEAD_DIM):
                cols = slice(hh * HEAD_DIM, (hh + 1) * HEAD_DIM)
                q = _norm_rope(z[:, cols], qg_ref[...], cos, sin)
                proj_ref[rows, cols] = (q * (ATTN_SCALE * LOG2E)).astype(BF16)

    @pl.when(j == 7)
    def _():
        for rows in chunks:
            z = zed(rows)
            cos = cos_ref[rows, :]
            sin = sin_ref[rows, :]
            for hh in range(GQA_KV_HEADS):
                cols = slice(hh * HEAD_DIM, (hh + 1) * HEAD_DIM)
                proj_ref[rows, cols] = _norm_rope(z[:, cols], kg_ref[...], cos, sin).astype(BF16)
            proj_ref[rows, KV_WIDTH:] = z[:, KV_WIDTH:].astype(BF16)


def _proj_in(x, sh, sc, w_bf16, cos, sin, q_gain, k_gain, tm):
    n = x.shape[0]
    return pl.pallas_call(
        _proj_in_kernel,
        grid=(n // tm, PROJ_NJ),
        in_specs=[
            pl.BlockSpec((tm, D_MODEL), lambda i, j: (i, 0)),
            pl.BlockSpec((1, D_MODEL), lambda i, j: (0, 0)),
            pl.BlockSpec((1, D_MODEL), lambda i, j: (0, 0)),
            pl.BlockSpec((D_MODEL, PROJ_TN), lambda i, j: (0, j)),
            pl.BlockSpec((tm, HEAD_DIM), lambda i, j: (i, 0)),
            pl.BlockSpec((tm, HEAD_DIM), lambda i, j: (i, 0)),
            pl.BlockSpec((1, HEAD_DIM), lambda i, j: (0, 0)),
            pl.BlockSpec((1, HEAD_DIM), lambda i, j: (0, 0)),
        ],
        out_specs=[
            pl.BlockSpec((tm, PROJ_TN), lambda i, j: (i, j)),
            pl.BlockSpec((tm, PROJ_TN), lambda i, j: (i, jnp.clip(j - 3, 0, 1))),
        ],
        out_shape=[
            jax.ShapeDtypeStruct((n, IN_WIDTH), BF16),
            jax.ShapeDtypeStruct((n, 2 * SG_WIDTH), F32),
        ],
        scratch_shapes=[pltpu.VMEM((tm, D_MODEL), BF16)],
        compiler_params=_cparams(("arbitrary", "arbitrary"), 48),
        name="proj_in",
    )(x, sh, sc, w_bf16, cos, sin, q_gain, k_gain)


def _na_kernel(q_ref, k_ref, v_ref, kc_ref, vc_ref, bias_ref, o_ref):
    b = pl.program_id(0)
    key_row0 = jnp.clip(b * NA_R - WIN_H // 2, 0, ROWS - NA_KR)
    start = pl.multiple_of(key_row0 * GRID_W, GRID_W)
    for h in range(NA_HEADS):
        cols = slice(h * HEAD_DIM, (h + 1) * HEAD_DIM)
        q = q_ref[:, cols]
        kw = k_ref[pl.ds(start, NA_TK), cols]
        vw = v_ref[pl.ds(start, NA_TK), cols]
        s_loc = _dot_nt(q, kw) + bias_ref[h]
        s_ctx = _dot_nt(q, kc_ref[:, cols])
        m = jnp.maximum(jnp.max(s_loc, axis=-1, keepdims=True), jnp.max(s_ctx, axis=-1, keepdims=True))
        p_loc = jnp.exp(s_loc - m)
        p_ctx = jnp.exp(s_ctx - m)
        denom = jnp.sum(p_loc, axis=-1, keepdims=True) + jnp.sum(p_ctx, axis=-1, keepdims=True)
        o = _dot(p_loc.astype(BF16), vw) + _dot(p_ctx.astype(BF16), vc_ref[:, cols])
        o_ref[:, cols] = (o / denom).astype(BF16)


NA_BIAS_LANE0 = 64
NA_BIAS_ROLL = HEAD_DIM - NA_BIAS_LANE0 - (WIN_W - 1)


def _na_bias_kernel(rows_ref, o_ref):
    c = lax.broadcasted_iota(I32, (GRID_W, GRID_W), 0)
    kc = lax.broadcasted_iota(I32, (GRID_W, GRID_W), 1)
    cs = jnp.clip(c - WIN_W // 2, 0, GRID_W - WIN_W)
    in_cols = (kc >= cs) & (kc < cs + WIN_W)
    for ri in range(NA_R):
        for kri in range(NA_KR):
            row = rows_ref[ri * NA_KR + kri:ri * NA_KR + kri + 1, :]
            spread = pltpu.roll(jnp.broadcast_to(row, (GRID_W, HEAD_DIM)), NA_BIAS_ROLL, 1,
                                stride=1, stride_axis=0)
            o_ref[ri * GRID_W:(ri + 1) * GRID_W, kri * GRID_W:(kri + 1) * GRID_W] = jnp.where(
                in_cols, spread[:, :GRID_W], NEG_BIG)


def _na_bias_tables(rpb):
    picked = []
    for r0, k0 in ((0, 0), (2 * NA_R, 2 * NA_R - WIN_H // 2), (ROWS - NA_R, ROWS - NA_KR)):
        r = r0 + np.arange(NA_R)
        rs = np.clip(r - WIN_H // 2, 0, ROWS - WIN_H)
        kr = k0 + np.arange(NA_KR)
        ok_r = (kr[None, :] >= rs[:, None]) & (kr[None, :] < rs[:, None] + WIN_H)
        drow = kr[None, :] - r[:, None] + (WIN_H - 1)
        pick_r = (np.clip(drow, 0, 2 * WIN_H - 2)[..., None] == np.arange(2 * WIN_H - 1)).astype(np.float32)
        rows = jnp.einsum('rka,lhab->lhrkb', pick_r, rpb.astype(F32), precision=lax.Precision.HIGHEST)
        rows = jnp.pad(rows, ((0, 0),) * 4 + ((NA_BIAS_LANE0, HEAD_DIM - NA_BIAS_LANE0 - (2 * WIN_W - 1)),))
        picked.append(jnp.where(ok_r[None, None, :, :, None], rows, NEG_BIG))
    rows_all = jnp.stack(picked, axis=1).reshape(DEPTH * 3 * NA_HEADS, NA_R * NA_KR, HEAD_DIM)
    tables = pl.pallas_call(
        _na_bias_kernel,
        grid=(DEPTH * 3 * NA_HEADS,),
        in_specs=[pl.BlockSpec((None, NA_R * NA_KR, HEAD_DIM), lambda t: (t, 0, 0))],
        out_specs=pl.BlockSpec((None, NA_TQ, NA_TK), lambda t: (t, 0, 0)),
        out_shape=jax.ShapeDtypeStruct((DEPTH * 3 * NA_HEADS, NA_TQ, NA_TK), F32),
        compiler_params=_cparams(("arbitrary",), 16),
        name="na_bias",
    )(rows_all)
    return tables.reshape(DEPTH, 3, NA_HEADS, NA_TQ, NA_TK)


def _na_attention(proj, proj_ctx, bias_tab, layer):
    def bias_idx(b):
        return (layer, jnp.where(b == 0, 0, jnp.where(b == NA_NB - 1, 2, 1)), 0, 0, 0)

    return pl.pallas_call(
        _na_kernel,
        grid=(NA_NB,),
        in_specs=[
            pl.BlockSpec((NA_TQ, NA_WIDTH), lambda b: (b, 0)),
            pl.BlockSpec((SEQ, NA_WIDTH), lambda b: (0, 1)),
            pl.BlockSpec((SEQ, NA_WIDTH), lambda b: (0, 2)),
            pl.BlockSpec((CTX_LEN, NA_WIDTH), lambda b: (0, 1)),
            pl.BlockSpec((CTX_LEN, NA_WIDTH), lambda b: (0, 2)),
            pl.BlockSpec((None, None, NA_HEADS, NA_TQ, NA_TK), bias_idx),
        ],
        out_specs=pl.BlockSpec((NA_TQ, NA_WIDTH), lambda b: (b, 0)),
        out_shape=jax.ShapeDtypeStruct((SEQ, NA_WIDTH), BF16),
        compiler_params=_cparams(("arbitrary",), 48),
        name="na_attn",
    )(proj, proj, proj, proj_ctx, proj_ctx, bias_tab)


def _ctx_attn_kernel(q_ref, k_ref, v_ref, o_ref, *, log2_domain):
    s = _dot_nt(q_ref[...], k_ref[...])
    m = jnp.max(s, axis=-1, keepdims=True)
    p = jnp.exp2(s - m) if log2_domain else jnp.exp(s - m)
    denom = jnp.sum(p, axis=-1, keepdims=True)
    o_ref[...] = (_dot(p.astype(BF16), v_ref[...]) / denom).astype(BF16)


def _ctx_attention(proj_ctx, n_heads, q_col, k_col, v_col, q_per_kv, log2_domain):
    return pl.pallas_call(
        functools.partial(_ctx_attn_kernel, log2_domain=log2_domain),
        grid=(n_heads,),
        in_specs=[
            pl.BlockSpec((CTX_LEN, HEAD_DIM), lambda h: (0, q_col + h)),
            pl.BlockSpec((CTX_LEN, HEAD_DIM), lambda h: (0, k_col + h // q_per_kv)),
            pl.BlockSpec((CTX_LEN, HEAD_DIM), lambda h: (0, v_col + h // q_per_kv)),
        ],
        out_specs=pl.BlockSpec((CTX_LEN, HEAD_DIM), lambda h: (0, h)),
        out_shape=jax.ShapeDtypeStruct((CTX_LEN, n_heads * HEAD_DIM), BF16),
        compiler_params=_cparams(("arbitrary",), 16),
        name="ctx_attn",
    )(proj_ctx, proj_ctx, proj_ctx)


SG_ROWS = 256


def _sg_kernel(zu_ref, zv_ref, g_ref, b_ref, w_ref, bs_ref, o_ref):
    for c in range(SG_ROWS // SG_CHUNK):
        rows = slice(c * SG_CHUNK, (c + 1) * SG_CHUNK)
        for g in range(SG_GROUPS):
            cols = slice(g * HEAD_DIM, (g + 1) * HEAD_DIM)
            v = jax.nn.gelu(zv_ref[rows, cols])
            vn = _layer_norm_rows(v) * g_ref[:, cols] + b_ref[:, cols]
            t = _dot(w_ref[g], vn.astype(BF16)) + bs_ref[:, g:g + 1]
            o_ref[rows, cols] = (jax.nn.gelu(zu_ref[rows, cols]) * t).astype(BF16)


def _spatial_gating(sgz, ln_g, ln_b, w_bf16, b_s):
    n = sgz.shape[0]
    return pl.pallas_call(
        _sg_kernel,
        grid=(n // SG_ROWS,),
        in_specs=[
            pl.BlockSpec((SG_ROWS, SG_WIDTH), lambda i: (i, 0)),
            pl.BlockSpec((SG_ROWS, SG_WIDTH), lambda i: (i, 1)),
            pl.BlockSpec((1, SG_WIDTH), lambda i: (0, 0)),
            pl.BlockSpec((1, SG_WIDTH), lambda i: (0, 0)),
            pl.BlockSpec((SG_GROUPS, SG_CHUNK, SG_CHUNK), lambda i: (0, 0, 0)),
            pl.BlockSpec((SG_CHUNK, SG_GROUPS), lambda i: (0, 0)),
        ],
        out_specs=pl.BlockSpec((SG_ROWS, SG_WIDTH), lambda i: (i, 0)),
        out_shape=jax.ShapeDtypeStruct((n, SG_WIDTH), BF16),
        compiler_params=_cparams(("arbitrary",), 16),
        name="spatial_gating",
    )(sgz, sgz, ln_g.reshape(1, SG_WIDTH), ln_b.reshape(1, SG_WIDTH), w_bf16, b_s.T)


GQA_TQ = 512
GQA_TK = 1024


def _gqa_kernel(qt_ref, k_ref, vt_ref, ot_ref, slots_x_ref, slots_y_ref, sa_ref, sb_ref, sc_ref,
                zero_ref, zero_sem):
    qt = qt_ref[...]
    n_chunks = SEQ // GQA_TK

    step = pl.program_id(0) * pl.num_programs(1) + pl.program_id(1)
    n_steps = pl.num_programs(0) * pl.num_programs(1)
    n_zero_blocks = slots_x_ref.shape[0] // MOE_TM

    def zero_copy(dst_ref, c):
        off = pl.multiple_of(c * MOE_TM, MOE_TM)
        return pltpu.make_async_copy(zero_ref, dst_ref.at[pl.ds(off, MOE_TM), :], zero_sem)

    @pl.when(step == 0)
    def _():
        zero_ref[...] = jnp.zeros_like(zero_ref)

    @pl.when(step < n_zero_blocks)
    def _():
        zero_copy(slots_x_ref, step).start()

    @pl.when((step >= n_zero_blocks) & (step < 2 * n_zero_blocks))
    def _():
        zero_copy(slots_y_ref, step - n_zero_blocks).start()

    def scores(t, dst_ref):
        off = pl.multiple_of(t * GQA_TK, GQA_TK)
        s = _dot(k_ref[pl.ds(off, GQA_TK), :], qt)
        dst_ref[...] = s
        return jnp.max(s, axis=0, keepdims=True)

    def consume(t, src_ref, m_chunk, m, l, acc):
        off = pl.multiple_of(t * GQA_TK, GQA_TK)
        m_new = jnp.maximum(m, m_chunk)
        alpha = jnp.exp2(m - m_new)
        p = jnp.exp2(src_ref[...] - m_new)
        l = alpha * l + jnp.sum(p, axis=0, keepdims=True)
        acc = alpha * acc + _dot(vt_ref[:, pl.ds(off, GQA_TK)], p.astype(BF16))
        return m_new, l, acc

    m = jnp.full((1, GQA_TQ), -jnp.inf, F32)
    l = jnp.zeros((1, GQA_TQ), F32)
    acc = jnp.zeros((HEAD_DIM, GQA_TQ), F32)
    mc_a = scores(0, sa_ref)

    def body(u, carry):
        m, l, acc, mc_a = carry
        mc_b = scores(2 * u + 1, sb_ref)
        m, l, acc = consume(2 * u, sa_ref, mc_a, m, l, acc)
        mc_a = scores(2 * u + 2, sa_ref)
        m, l, acc = consume(2 * u + 1, sb_ref, mc_b, m, l, acc)
        return m, l, acc, mc_a

    m, l, acc, mc_a = lax.fori_loop(0, n_chunks // 2 - 1, body, (m, l, acc, mc_a))
    mc_b = scores(n_chunks - 1, sb_ref)
    m, l, acc = consume(n_chunks - 2, sa_ref, mc_a, m, l, acc)
    s_ctx = _dot(k_ref[pl.ds(SEQ, CTX_LEN), :], qt)
    sc_ref[...] = s_ctx
    mc_ctx = jnp.max(s_ctx, axis=0, keepdims=True)
    m, l, acc = consume(n_chunks - 1, sb_ref, mc_b, m, l, acc)
    m_new = jnp.maximum(m, mc_ctx)
    alpha = jnp.exp2(m - m_new)
    p = jnp.exp2(sc_ref[...] - m_new)
    l = alpha * l + jnp.sum(p, axis=0, keepdims=True)
    acc = alpha * acc + _dot(vt_ref[:, pl.ds(SEQ, CTX_LEN)], p.astype(BF16))
    ot_ref[...] = (acc / l).astype(BF16)

    @pl.when(step == n_steps - 1)
    def _():
        def wait(c, carry):
            zero_copy(slots_x_ref, c).wait()
            zero_copy(slots_y_ref, c).wait()
            return carry
        lax.fori_loop(0, n_zero_blocks, wait, 0)


def _gqa_attention(q_t, k_all, v_t, n_slot_rows):
    grp = GQA_Q_HEADS // GQA_KV_HEADS
    grid = (GQA_Q_HEADS, SEQ // GQA_TQ)
    assert n_slot_rows % MOE_TM == 0 and 2 * (n_slot_rows // MOE_TM) <= grid[0] * grid[1]
    return pl.pallas_call(
        _gqa_kernel,
        grid=grid,
        in_specs=[
            pl.BlockSpec((HEAD_DIM, GQA_TQ), lambda h, i: (h, i)),
            pl.BlockSpec((KEYS, HEAD_DIM), lambda h, i: (0, h // grp)),
            pl.BlockSpec((HEAD_DIM, KEYS), lambda h, i: (h // grp, 0)),
        ],
        out_specs=[pl.BlockSpec((HEAD_DIM, GQA_TQ), lambda h, i: (h, i)),
                   pl.BlockSpec(memory_space=pl.ANY),
                   pl.BlockSpec(memory_space=pl.ANY)],
        out_shape=[jax.ShapeDtypeStruct((GQA_WIDTH, SEQ), BF16),
                   jax.ShapeDtypeStruct((n_slot_rows, D_MODEL), F32),
                   jax.ShapeDtypeStruct((n_slot_rows, D_MODEL), F32)],
        scratch_shapes=[pltpu.VMEM((GQA_TK, GQA_TQ), F32), pltpu.VMEM((GQA_TK, GQA_TQ), F32),
                        pltpu.VMEM((CTX_LEN, GQA_TQ), F32),
                        pltpu.VMEM((MOE_TM, D_MODEL), F32), pltpu.SemaphoreType.DMA(())],
        compiler_params=_cparams(("arbitrary", "arbitrary"), 32),
        name="gqa_attn",
    )(q_t, k_all, v_t)


def _proj_out_kernel(ona_ref, osg_ref, ogq_ref, w_ref, x_ref, g1_ref, lng_ref, lnb_ref,
                     sh2_ref, sc2_ref, rwt_ref, rb_ref, xmid_ref, h2_ref, e_ref, gate_ref, y_scr):
    i = pl.program_id(0)

    @pl.when(i == 0)
    def _():
        y_scr[1] = jnp.zeros_like(y_scr[1])

    for parity in range(2):
        @pl.when(i % 2 == parity)
        def _(parity=parity):
            y_scr[parity] = (_dot(ona_ref[...], w_ref[0:NA_WIDTH, :])
                             + _dot(osg_ref[...], w_ref[NA_WIDTH:NA_WIDTH + SG_WIDTH, :])
                             + _dot(ogq_ref[...], w_ref[NA_WIDTH + SG_WIDTH:, :]))
            z = ALPHA * x_ref[...] + g1_ref[...] * y_scr[1 - parity]
            xm = _layer_norm_rows(z) * lng_ref[...] + lnb_ref[...]
            xmid_ref[...] = xm
            h2 = _layer_norm_rows(xm) * (1.0 + sc2_ref[...]) + sh2_ref[...]
            h2_ref[...] = h2
            e_ref[...], gate_ref[...] = _route(h2, rwt_ref[...], rb_ref[...])


def _proj_out(o_na, o_sg, o_gq, w_bf16, x, g1, ln_g, ln_b, sh2, sc2, rw_t_bf16, rb_col):
    n = x.shape[0]
    tm = ROW_TILE
    n_tiles = n // tm
    vec = pl.BlockSpec((1, D_MODEL), lambda i: (0, 0))

    def mm_tile(i):
        return jnp.minimum(i, n_tiles - 1)

    def ep_tile(i):
        return jnp.maximum(i - 1, 0)

    return pl.pallas_call(
        _proj_out_kernel,
        grid=(n_tiles + 1,),
        in_specs=[
            pl.BlockSpec((tm, NA_WIDTH), lambda i: (mm_tile(i), 0)),
            pl.BlockSpec((tm, SG_WIDTH), lambda i: (mm_tile(i), 0)),
            pl.BlockSpec((tm, GQA_WIDTH), lambda i: (mm_tile(i), 0)),
            pl.BlockSpec((MIX_WIDTH, D_MODEL), lambda i: (0, 0)),
            pl.BlockSpec((tm, D_MODEL), lambda i: (ep_tile(i), 0)),
            vec, vec, vec, vec, vec,
            pl.BlockSpec((N_EXPERTS, D_MODEL), lambda i: (0, 0)),
            pl.BlockSpec((N_EXPERTS, 1), lambda i: (0, 0)),
        ],
        out_specs=[pl.BlockSpec((tm, D_MODEL), lambda i: (ep_tile(i), 0)),
                   pl.BlockSpec((tm, D_MODEL), lambda i: (ep_tile(i), 0)),
                   pl.BlockSpec((TOP_K, tm), lambda i: (0, ep_tile(i))),
                   pl.BlockSpec((TOP_K, tm), lambda i: (0, ep_tile(i)))],
        out_shape=[jax.ShapeDtypeStruct((n, D_MODEL), F32),
                   jax.ShapeDtypeStruct((n, D_MODEL), F32),
                   jax.ShapeDtypeStruct((TOP_K, n), I32),
                   jax.ShapeDtypeStruct((TOP_K, n), F32)],
        scratch_shapes=[pltpu.VMEM((2, tm, D_MODEL), F32)],
        compiler_params=_cparams(("arbitrary",), 48),
        name="proj_out",
    )(o_na, o_sg, o_gq, w_bf16, x, g1, ln_g, ln_b, sh2, sc2, rw_t_bf16, rb_col)


def _route(h, rw_t, rb_col):
    tm = h.shape[0]
    logits = _dot_nt(rw_t, h.astype(BF16))
    scores = jax.nn.sigmoid(logits)
    sel = scores + rb_col
    sub = lax.broadcasted_iota(I32, (EXPERTS_PER_GROUP, tm), 0)
    best = None
    for g in range(N_GROUPS):
        rows = slice(g * EXPERTS_PER_GROUP, (g + 1) * EXPERTS_PER_GROUP)
        v = sel[rows, :]
        sc = scores[rows, :]
        m1 = jnp.max(v, axis=0, keepdims=True)
        i1 = jnp.min(jnp.where(v == m1, sub, EXPERTS_PER_GROUP), axis=0, keepdims=True)
        v2 = jnp.where(sub == i1, -jnp.inf, v)
        m2 = jnp.max(v2, axis=0, keepdims=True)
        i2 = jnp.min(jnp.where(v2 == m2, sub, EXPERTS_PER_GROUP), axis=0, keepdims=True)
        s1 = jnp.sum(jnp.where(sub == i1, sc, 0.0), axis=0, keepdims=True)
        s2 = jnp.sum(jnp.where(sub == i2, sc, 0.0), axis=0, keepdims=True)
        cand = (m1 + m2, i1 + g * EXPERTS_PER_GROUP, i2 + g * EXPERTS_PER_GROUP, s1, s2)
        if best is None:
            best = cand
        else:
            better = cand[0] > best[0]
            best = tuple(jnp.where(better, cn, bs) for cn, bs in zip(cand, best))
    _, e1, e2, s1, s2 = best
    tot = s1 + s2
    return jnp.concatenate([e1, e2], axis=0), jnp.concatenate([s1 / tot, s2 / tot], axis=0)


def _num_tiles(n_tok):
    return (n_tok * TOP_K + N_EXPERTS * (MOE_TM - 1) + MOE_TM - 1) // MOE_TM


def _positions_kernel(e_ref, dest_ref, meta_ref, rank_scr, *, n_tok):
    ch = POS_CHUNK
    n_chunks = n_tok // ch
    iota_e = lax.broadcasted_iota(I32, (N_EXPERTS, ch), 0)
    upper = (lax.broadcasted_iota(I32, (ch, ch), 0) <= lax.broadcasted_iota(I32, (ch, ch), 1)).astype(BF16)

    def one_hots(off):
        e0 = e_ref[0:1, pl.ds(off, ch)]
        e1 = e_ref[1:2, pl.ds(off, ch)]
        return (iota_e == e0).astype(F32), (iota_e == e1).astype(F32)

    def rank_body(c, carry):
        off = pl.multiple_of(c * ch, ch)
        oh0, oh1 = one_hots(off)
        both = oh0 + oh1
        incl = _dot(both.astype(BF16), upper)
        before = incl - both + carry
        rank_scr[0:1, pl.ds(off, ch)] = jnp.sum(oh0 * before, axis=0, keepdims=True)
        rank_scr[1:2, pl.ds(off, ch)] = jnp.sum(oh1 * (before + oh0), axis=0, keepdims=True)
        return carry + jnp.sum(both, axis=1, keepdims=True)

    counts = lax.fori_loop(0, n_chunks, rank_body, jnp.zeros((N_EXPERTS, 1), F32))
    tiles = jnp.zeros((N_EXPERTS, 1), F32)
    for k in range(n_tok * TOP_K // MOE_TM + 1):
        tiles = tiles + (counts > float(k * MOE_TM)).astype(F32)
    tiles_b = jnp.broadcast_to(tiles, (N_EXPERTS, META_LANES)).astype(BF16)
    lower = (lax.broadcasted_iota(I32, (N_EXPERTS, N_EXPERTS), 1)
             < lax.broadcasted_iota(I32, (N_EXPERTS, N_EXPERTS), 0)).astype(BF16)
    tile_start = _dot(lower, tiles_b)
    tile_end = tile_start + tiles
    pad_start = tile_start[:, 0:1] * float(MOE_TM)

    def dest_body(c, carry):
        off = pl.multiple_of(c * ch, ch)
        oh0, oh1 = one_hots(off)
        d0 = rank_scr[0:1, pl.ds(off, ch)] + jnp.sum(oh0 * pad_start, axis=0, keepdims=True)
        d1 = rank_scr[1:2, pl.ds(off, ch)] + jnp.sum(oh1 * pad_start, axis=0, keepdims=True)
        dest_ref[0:1, pl.ds(off, ch)] = d0.astype(I32)
        dest_ref[1:2, pl.ds(off, ch)] = d1.astype(I32)
        return carry

    lax.fori_loop(0, n_chunks, dest_body, 0)

    tile_id = lax.broadcasted_iota(I32, (N_EXPERTS, META_LANES), 1).astype(F32)
    tile_expert = jnp.sum((tile_end <= tile_id).astype(F32), axis=0, keepdims=True)
    tile_expert = jnp.minimum(tile_expert, float(N_EXPERTS - 1))
    inside = (tile_start <= tile_id) & (tile_id < tile_end)
    rows_here = jnp.clip(counts - float(MOE_TM) * (tile_id - tile_start), 0.0, float(MOE_TM))
    tile_rows = jnp.sum(jnp.where(inside, rows_here, 0.0), axis=0, keepdims=True)
    n_used = jnp.sum(tiles, axis=0, keepdims=True)
    row = lax.broadcasted_iota(I32, (8, META_LANES), 0)
    shape = (8, META_LANES)
    meta = jnp.where(row == META_TILE_EXPERT, jnp.broadcast_to(tile_expert, shape),
                     jnp.where(row == META_TILE_ROWS, jnp.broadcast_to(tile_rows, shape),
                               jnp.where(row == META_N_USED, jnp.broadcast_to(n_used, shape), 0.0)))
    meta_ref[...] = meta.astype(I32)


def _positions(e_all):
    n_tok = e_all.shape[1]
    return pl.pallas_call(
        functools.partial(_positions_kernel, n_tok=n_tok),
        out_shape=[jax.ShapeDtypeStruct((TOP_K, n_tok), I32),
                   jax.ShapeDtypeStruct((8, 128), I32)],
        scratch_shapes=[pltpu.VMEM((TOP_K, n_tok), F32)],
        compiler_params=pltpu.CompilerParams(vmem_limit_bytes=16 * 1024 * 1024),
        name="moe_positions",
    )(e_all)


def _dispatch_kernel(dest_ref, h_ref, xs_in_ref, xs_ref, sem, *, n_tok):
    del xs_in_ref
    tm = h_ref.shape[0]
    base = pl.program_id(0) * tm

    def row_copy(r, k):
        d = dest_ref[k * n_tok + base + r]
        return pltpu.make_async_copy(h_ref.at[pl.ds(r, 1), :], xs_ref.at[pl.ds(d, 1), :], sem)

    def start(r, carry):
        for k in range(TOP_K):
            row_copy(r, k).start(priority=k)
        return carry

    def wait(r, carry):
        for k in range(TOP_K):
            row_copy(r, k).wait()
        return carry

    lax.fori_loop(0, tm, start, 0)
    lax.fori_loop(0, tm, wait, 0)


def _dispatch(dest_flat, h2, xs):
    n = h2.shape[0]
    tm = min(DISPATCH_TILE, n)
    return pl.pallas_call(
        functools.partial(_dispatch_kernel, n_tok=n),
        grid_spec=pltpu.PrefetchScalarGridSpec(
            num_scalar_prefetch=1,
            grid=(n // tm,),
            in_specs=[pl.BlockSpec((tm, D_MODEL), lambda i, d: (i, 0)),
                      pl.BlockSpec(memory_space=pl.ANY)],
            out_specs=pl.BlockSpec(memory_space=pl.ANY),
            scratch_shapes=[pltpu.SemaphoreType.DMA(())],
        ),
        out_shape=jax.ShapeDtypeStruct(xs.shape, xs.dtype),
        input_output_aliases={2: 0},
        compiler_params=_cparams(("arbitrary",), 32),
        name="moe_dispatch",
    )(dest_flat, h2, xs)


def _moe_kernel(meta_ref, x_ref, wg_ref, wu_ref, wd_ref, y_in_ref, y_ref, xb_scr):
    del y_in_ref
    i = pl.program_id(0)
    f = pl.program_id(1)
    used = i < meta_ref[META_N_USED * META_LANES]
    rows = meta_ref[META_TILE_ROWS * META_LANES + i]
    n_sub = (rows + (MOE_SUB - 1)) >> MOE_SUB_LOG2

    @pl.when(used & (f == 0))
    def _():
        y_ref[...] = jnp.zeros_like(y_ref)

    for occupied in range(1, MOE_TM // MOE_SUB + 1):
        m_rows = occupied * MOE_SUB

        @pl.when(used & (n_sub == occupied))
        def _(m_rows=m_rows):
            @pl.when(f == 0)
            def _():
                xb_scr[0:m_rows, :] = x_ref[0:m_rows, :].astype(BF16)

            xb = xb_scr[0:m_rows, :]
            gate = _dot(xb, wg_ref[...].astype(BF16))
            up = _dot(xb, wu_ref[...].astype(BF16))
            hidden = (gate * jax.nn.sigmoid(gate) * up).astype(BF16)
            y_ref[0:m_rows, :] += _dot(hidden, wd_ref[...].astype(BF16))


def _moe_ffn(meta_flat, xs, w_gate, w_up, w_down, y_zero, layer, n_tiles):
    n_used_at = META_N_USED * META_LANES

    def last_used(i, m):
        return jnp.minimum(i, m[n_used_at] - 1)

    def f_eff(i, f, m):
        return jnp.where(i < m[n_used_at], f, MOE_NF - 1)

    return pl.pallas_call(
        _moe_kernel,
        grid_spec=pltpu.PrefetchScalarGridSpec(
            num_scalar_prefetch=1,
            grid=(n_tiles, MOE_NF),
            in_specs=[
                pl.BlockSpec((MOE_TM, D_MODEL), lambda i, f, m: (last_used(i, m), 0)),
                pl.BlockSpec((None, None, D_MODEL, MOE_TF),
                             lambda i, f, m: (layer, m[last_used(i, m)], 0, f_eff(i, f, m))),
                pl.BlockSpec((None, None, D_MODEL, MOE_TF),
                             lambda i, f, m: (layer, m[last_used(i, m)], 0, f_eff(i, f, m))),
                pl.BlockSpec((None, None, MOE_TF, D_MODEL),
                             lambda i, f, m: (layer, m[last_used(i, m)], f_eff(i, f, m), 0)),
                pl.BlockSpec(memory_space=pl.ANY),
            ],
            out_specs=pl.BlockSpec((MOE_TM, D_MODEL), lambda i, f, m: (last_used(i, m), 0)),
            scratch_shapes=[pltpu.VMEM((MOE_TM, D_MODEL), BF16)],
        ),
        out_shape=jax.ShapeDtypeStruct((n_tiles * MOE_TM, D_MODEL), F32),
        input_output_aliases={5: 0},
        compiler_params=_cparams(("arbitrary", "arbitrary"), 56),
        name="moe_ffn",
    )(meta_flat, xs, w_gate, w_up, w_down, y_zero)


def _combine_kernel(dest_ref, y_ref, gt_ref, x_ref, g2_ref, lng_ref, lnb_ref, o_ref,
                    buf_ref, sem, *, n_tok):
    tm = x_ref.shape[0]
    i = pl.program_id(0)

    def row_copy(tile, slot, r, k):
        d = dest_ref[k * n_tok + tile * tm + r]
        return pltpu.make_async_copy(y_ref.at[pl.ds(d, 1), :],
                                     buf_ref.at[slot, k, pl.ds(r, 1), :], sem.at[slot])

    def start_tile(tile, slot):
        def start(r, carry):
            for k in range(TOP_K):
                row_copy(tile, slot, r, k).start(priority=k)
            return carry
        lax.fori_loop(0, tm, start, 0)

    def wait_tile(tile, slot):
        def wait(r, carry):
            for k in range(TOP_K):
                row_copy(tile, slot, r, k).wait()
            return carry
        lax.fori_loop(0, tm, wait, 0)

    @pl.when(i == 0)
    def _():
        start_tile(i, 0)

    for slot in range(2):
        @pl.when((i + 1 < pl.num_programs(0)) & ((i + 1) % 2 == slot))
        def _(slot=slot):
            start_tile(i + 1, slot)

    for slot in range(2):
        @pl.when(i % 2 == slot)
        def _(slot=slot):
            wait_tile(i, slot)
            gt = gt_ref[...]
            moe = buf_ref[slot, 0] * gt[:, 0:1] + buf_ref[slot, 1] * gt[:, 1:2]
            z = ALPHA * x_ref[...] + g2_ref[...] * moe
            o_ref[...] = _layer_norm_rows(z) * lng_ref[...] + lnb_ref[...]


def _combine(dest_flat, y, gates_t, x_mid, g2, ln_g, ln_b):
    n = x_mid.shape[0]
    tm = ROW_TILE
    vec = pl.BlockSpec((1, D_MODEL), lambda i, d: (0, 0))
    return pl.pallas_call(
        functools.partial(_combine_kernel, n_tok=n),
        grid_spec=pltpu.PrefetchScalarGridSpec(
            num_scalar_prefetch=1,
            grid=(n // tm,),
            in_specs=[pl.BlockSpec(memory_space=pl.ANY),
                      pl.BlockSpec((tm, TOP_K), lambda i, d: (i, 0)),
                      pl.BlockSpec((tm, D_MODEL), lambda i, d: (i, 0)),
                      vec, vec, vec],
            out_specs=pl.BlockSpec((tm, D_MODEL), lambda i, d: (i, 0)),
            scratch_shapes=[pltpu.VMEM((2, TOP_K, tm, D_MODEL), F32),
                            pltpu.SemaphoreType.DMA((2,))],
        ),
        out_shape=jax.ShapeDtypeStruct((n, D_MODEL), F32),
        compiler_params=_cparams(("arbitrary",), 32),
        name="moe_combine",
    )(dest_flat, y, gates_t, x_mid, g2, ln_g, ln_b)


def _rope_tables():
    t = jnp.arange(SEQ, dtype=jnp.int32)
    row = (t // GRID_W).astype(F32)
    col = (t % GRID_W).astype(F32)
    inv_freq = ROPE_THETA ** (-jnp.arange(ROPE_PAIRS, dtype=F32) / ROPE_PAIRS)
    ang_r = row[:, None] * inv_freq
    ang_c = col[:, None] * inv_freq
    cos = jnp.concatenate([jnp.cos(ang_r), jnp.cos(ang_r), jnp.cos(ang_c), jnp.cos(ang_c)], axis=-1)
    sin = jnp.concatenate([-jnp.sin(ang_r), jnp.sin(ang_r), -jnp.sin(ang_c), jnp.sin(ang_c)], axis=-1)
    return cos, sin


def kernel(x, c, ctx, c_ctx, ada_w, ada_b, w_in, na_rpb, sg_ln_g, sg_ln_b, sg_w, sg_b, q_norm_g, k_norm_g, w_out, ln_mix_g, ln_mix_b, router_w, router_bias, moe_w_gate, moe_w_up, moe_w_down, ln_ffn_g, ln_ffn_b):
    cos, sin = _rope_tables()
    cos_ctx = jnp.ones((CTX_LEN, HEAD_DIM), F32)
    sin_ctx = jnp.zeros((CTX_LEN, HEAD_DIM), F32)

    cvec = jnp.zeros((8, D_MODEL), F32).at[0].set(c[0]).at[1].set(c_ctx)
    mods = _ada(cvec, ada_w, ada_b)

    na_bias = _na_bias_tables(na_rpb)
    rw_t = router_w.T.astype(BF16)
    rb_col = router_bias.reshape(N_EXPERTS, 1).astype(F32)

    xl = x[0]
    xc = ctx[0]
    for layer in range(DEPTH):
        last = layer == DEPTH - 1
        sh1, sc1, g1, sh2, sc2, g2 = jnp.split(mods[layer, 0:1], 6, axis=-1)
        csh1, csc1, cg1, csh2, csc2, cg2 = jnp.split(mods[layer, 1:2], 6, axis=-1)
        w_in_b = w_in[layer].astype(BF16)
        w_out_b = w_out[layer].astype(BF16)
        sg_w_b = sg_w[layer].astype(BF16)
        q_gain = q_norm_g[layer].reshape(1, HEAD_DIM)
        k_gain = k_norm_g[layer].reshape(1, HEAD_DIM)
        lng = ln_mix_g[layer].reshape(1, D_MODEL)
        lnb = ln_mix_b[layer].reshape(1, D_MODEL)
        fng = ln_ffn_g[layer].reshape(1, D_MODEL)
        fnb = ln_ffn_b[layer].reshape(1, D_MODEL)

        proj, sgz = _proj_in(xl, sh1, sc1, w_in_b, cos, sin, q_gain, k_gain, tm=1024)
        proj_c, sgz_c = _proj_in(xc, csh1, csc1, w_in_b, cos_ctx, sin_ctx, q_gain, k_gain, tm=CTX_LEN)

        o_na = _na_attention(proj, proj_c, na_bias, layer)
        o_sg = _spatial_gating(sgz, sg_ln_g[layer], sg_ln_b[layer], sg_w_b, sg_b[layer])
        q_t = proj[:, COL_GQ * HEAD_DIM:COL_GK * HEAD_DIM].T
        k_all = jnp.concatenate([proj[:, COL_GK * HEAD_DIM:COL_GV * HEAD_DIM],
                                 proj_c[:, COL_GK * HEAD_DIM:COL_GV * HEAD_DIM]], axis=0)
        v_t = jnp.concatenate([proj[:, COL_GV * HEAD_DIM:], proj_c[:, COL_GV * HEAD_DIM:]], axis=0).T
        n_tok = SEQ if last else SEQ + CTX_LEN
        n_tiles = _num_tiles(n_tok)
        o_gq_t, xs, y_zero = _gqa_attention(q_t, k_all, v_t, n_tiles * MOE_TM)
        x_mid, h2, e_all, gates = _proj_out(o_na, o_sg, o_gq_t.T, w_out_b, xl, g1, lng, lnb, sh2, sc2,
                                            rw_t, rb_col)

        if not last:
            c_na = _ctx_attention(proj_c, NA_HEADS, COL_NA_Q, COL_NA_K, COL_NA_V, 1, False)
            c_sg = _spatial_gating(sgz_c, sg_ln_g[layer], sg_ln_b[layer], sg_w_b, sg_b[layer])
            c_gq = _ctx_attention(proj_c, GQA_Q_HEADS, COL_GQ, COL_GK, COL_GV,
                                  GQA_Q_HEADS // GQA_KV_HEADS, True)
            xc_mid, h2_c, e_ctx, gates_ctx = _proj_out(c_na, c_sg, c_gq, w_out_b, xc, cg1, lng, lnb,
                                                       csh2, csc2, rw_t, rb_col)
            e_all = jnp.concatenate([e_all, e_ctx], axis=1)
            gates = jnp.concatenate([gates, gates_ctx], axis=1)

        dest, meta = _positions(e_all)
        meta_flat = meta.reshape(-1)
        xs = _dispatch(dest[:, :SEQ].reshape(-1), h2, xs)
        if not last:
            xs = _dispatch(dest[:, SEQ:].reshape(-1), h2_c, xs)
        y = _moe_ffn(meta_flat, xs, moe_w_gate, moe_w_up, moe_w_down, y_zero, layer, n_tiles)
        gates_t = gates.T
        xl = _combine(dest[:, :SEQ].reshape(-1), y, gates_t[:SEQ], x_mid, g2, fng, fnb)
        if not last:
            xc = _combine(dest[:, SEQ:].reshape(-1), y, gates_t[SEQ:], xc_mid, cg2, fng, fnb)
    return xl[None]
```
